```python
import math
import jax
import jax.numpy as jnp
from jax import lax
import numpy as np

D_MODEL = 2048
BATCH = 2
SEQ = 8192
DEPTH = 1
DEC_BATCH = 32
DEC_SEQ = 4
PAST_LEN = 16384
PAGE_SIZE = 128

HEAD_DIM = 64
D_ATTN = D_MODEL // 2
D_RWKV = D_MODEL - D_ATTN
H_ATTN = D_ATTN // HEAD_DIM
H_RWKV = D_RWKV // HEAD_DIM
BLOCK = 256
TOP_BLOCKS = 3
Q_CHUNK = 64
N_BUCKETS = 32
MAX_DISTANCE = 128
DECAY_LORA = 64
AAA_LORA = 64
GATE_LORA = 128
C_SHIFT = 3 * D_RWKV + DECAY_LORA + AAA_LORA + GATE_LORA
D_IN = 3 * D_ATTN + C_SHIFT
LNX_EPS = 64e-5
LN_EPS = 1e-5
N_EXPERTS = 64
TOP_K = 8
N_GROUPS = 8
TOPK_GROUPS = 4
D_EXPERT = 512
D_SHARED = 512
ROUTED_SCALE = 2.5
EXPERT_BLOCK = 128
ALPHA = (2.0 * DEPTH) ** 0.25
BETA = (8.0 * DEPTH) ** -0.25

kernel_name = 'hymba_moba_rwkv7_moe_deepnorm_step'


def layer_norm(x, g, b):
    xf = x.astype(jnp.float32)
    mu = xf.mean(-1, keepdims=True)
    var = jnp.square(xf - mu).mean(-1, keepdims=True)
    return ((xf - mu) * lax.rsqrt(var + LN_EPS) * g + b).astype(x.dtype)


def t5_bucket(dist):
    n = jnp.maximum(dist, 0)
    max_exact = N_BUCKETS // 2
    nf = jnp.maximum(n, 1).astype(jnp.float32)
    large = max_exact + (jnp.log(nf / max_exact) / math.log(MAX_DISTANCE / max_exact)
                         * (N_BUCKETS - max_exact)).astype(jnp.int32)
    large = jnp.minimum(large, N_BUCKETS - 1)
    return jnp.where(n < max_exact, n, large)


def moba_prompt(q, k, v, rel_bias):
    B_, T_, H, Dh = q.shape
    f32 = jnp.float32
    n_blk = -(-T_ // BLOCK)
    pad = n_blk * BLOCK - T_
    kp = jnp.pad(k, ((0, 0), (0, pad), (0, 0), (0, 0)))
    vp = jnp.pad(v, ((0, 0), (0, pad), (0, 0), (0, 0)))
    kb = kp.reshape(B_, n_blk, BLOCK, H, Dh)
    vb = vp.reshape(B_, n_blk, BLOCK, H, Dh)
    kmean = kb.astype(f32).mean(axis=2)
    n_sel = min(TOP_BLOCKS, n_blk - 1)
    bias_t = rel_bias.astype(f32).T
    scale = Dh ** -0.5
    b_ix = jnp.arange(B_)[:, None, None, None]
    h_ix = jnp.arange(H)[None, :, None, None]
    offs = jnp.arange(BLOCK)

    def chunk(c):
        q0 = c * Q_CHUNK
        qc = lax.dynamic_slice_in_dim(q, q0, Q_CHUNK, axis=1).astype(f32).transpose(0, 2, 1, 3)
        q_pos = q0 + jnp.arange(Q_CHUNK)
        own = q0 // BLOCK
        k_own = lax.dynamic_slice_in_dim(kp, own * BLOCK, BLOCK, axis=1)
        v_own = lax.dynamic_slice_in_dim(vp, own * BLOCK, BLOCK, axis=1)
        dist_own = q_pos[:, None] - (own * BLOCK + offs)[None, :]
        s_own = jnp.einsum('bhqd,bkhd->bhqk', qc, k_own) * scale + bias_t[:, t5_bucket(dist_own)][None]
        s_own = jnp.where(dist_own >= 0, s_own, -jnp.inf)
        if n_sel == 0:
            p = jax.nn.softmax(s_own, axis=-1)
            return jnp.einsum('bhqk,bkhd->bhqd', p, v_own.astype(f32))
        gate = jnp.einsum('bhqd,bnhd->bhqn', qc, kmean)
        gate = jnp.where(jnp.arange(n_blk) < own, gate, -jnp.inf)
        _, idx = lax.top_k(gate, n_sel)
        k_sel = kb[b_ix, idx, :, h_ix, :]
        v_sel = vb[b_ix, idx, :, h_ix, :]
        dist_sel = q_pos[None, None, :, None, None] - (idx[..., None] * BLOCK + offs)
        s_sel = (jnp.einsum('bhqd,bhqskd->bhqsk', qc, k_sel) * scale
                 + bias_t[h_ix[..., None], t5_bucket(dist_sel)])
        s_sel = jnp.where((idx < own)[..., None], s_sel, -jnp.inf)
        n_s = n_sel * BLOCK
        p = jax.nn.softmax(jnp.concatenate([s_sel.reshape(B_, H, Q_CHUNK, n_s), s_own], axis=-1), axis=-1)
        return (jnp.einsum('bhqsk,bhqskd->bhqd', p[..., :n_s].reshape(s_sel.shape), v_sel.astype(f32))
                + jnp.einsum('bhqk,bkhd->bhqd', p[..., n_s:], v_own.astype(f32)))

    o = lax.map(chunk, jnp.arange(T_ // Q_CHUNK))
    return o.transpose(1, 0, 3, 2, 4).reshape(B_, T_, H * Dh).astype(q.dtype)


def moba_sample(q, k, v, cache_k, cache_v, page_table, layer, rel_bias):
    DB, DS, H, Dh = q.shape
    f32 = jnp.float32
    ppb = BLOCK // PAGE_SIZE
    own = PAST_LEN // BLOCK
    r0 = PAST_LEN - own * BLOCK
    bias_t = rel_bias.astype(f32).T
    scale = Dh ** -0.5
    qh = q.astype(f32).transpose(0, 2, 1, 3)
    q_pos = PAST_LEN + jnp.arange(DS)
    offs = jnp.arange(BLOCK)
    k_own, v_own = k, v
    if r0 > 0:
        own_pages = page_table[:, own * ppb: own * ppb + r0 // PAGE_SIZE]
        k_own = jnp.concatenate([cache_k[layer, own_pages].reshape(DB, r0, H, Dh).astype(k.dtype), k], axis=1)
        v_own = jnp.concatenate([cache_v[layer, own_pages].reshape(DB, r0, H, Dh).astype(v.dtype), v], axis=1)
    dist_own = q_pos[:, None] - (own * BLOCK + jnp.arange(r0 + DS))[None, :]
    s_own = jnp.einsum('bhqd,bkhd->bhqk', qh, k_own) * scale + bias_t[:, t5_bucket(dist_own)][None]
    s_own = jnp.where(dist_own >= 0, s_own, -jnp.inf)
    n_sel = min(TOP_BLOCKS, own)
    if n_sel == 0:
        p = jax.nn.softmax(s_own, axis=-1)
        o = jnp.einsum('bhqk,bkhd->bhqd', p, v_own.astype(f32))
        return o.transpose(0, 2, 1, 3).reshape(DB, DS, H * Dh).astype(q.dtype)
    past_pages = page_table[:, : own * ppb]
    kmean = cache_k[layer, past_pages].astype(f32).reshape(DB, own, BLOCK, H, Dh).mean(axis=2)
    gate = jnp.einsum('bhqd,bnhd->bhqn', qh, kmean)
    _, idx = lax.top_k(gate, n_sel)
    b_ix = jnp.arange(DB)[:, None, None, None, None]
    h_ix = jnp.arange(H)[None, :, None, None, None]
    phys = page_table[b_ix, idx[..., None] * ppb + jnp.arange(ppb)]
    sel_shape = (DB, H, DS, n_sel, BLOCK, Dh)
    k_sel = cache_k[layer, phys, :, h_ix, :].reshape(sel_shape)
    v_sel = cache_v[layer, phys, :, h_ix, :].reshape(sel_shape)
    dist_sel = q_pos[None, None, :, None, None] - (idx[..., None] * BLOCK + offs)
    s_sel = (jnp.einsum('bhqd,bhqskd->bhqsk', qh, k_sel) * scale
             + bias_t[h_ix, t5_bucket(dist_sel)])
    n_s = n_sel * BLOCK
    p = jax.nn.softmax(jnp.concatenate([s_sel.reshape(DB, H, DS, n_s), s_own], axis=-1), axis=-1)
    o = (jnp.einsum('bhqsk,bhqskd->bhqd', p[..., :n_s].reshape(s_sel.shape), v_sel.astype(f32))
         + jnp.einsum('bhqk,bkhd->bhqd', p[..., n_s:], v_own.astype(f32)))
    return o.transpose(0, 2, 1, 3).reshape(DB, DS, H * Dh).astype(q.dtype)


def rwkv_group(zr, shift0, S0, lw):
    B_, T_, _ = zr.shape
    f32 = jnp.float32
    z = zr.astype(f32)
    prev = jnp.concatenate([shift0.astype(f32)[:, None], z[:, :-1]], axis=1)
    zs = z + (prev - z) * lw['mu_shift']
    o = D_RWKV
    r, k, v = zs[..., :o], zs[..., o:2 * o], zs[..., 2 * o:3 * o]
    wl = zs[..., 3 * o:3 * o + DECAY_LORA]
    al = zs[..., 3 * o + DECAY_LORA:3 * o + DECAY_LORA + AAA_LORA]
    gl = zs[..., 3 * o + DECAY_LORA + AAA_LORA:]
    w = -jax.nn.softplus(-(lw['w0'] + jnp.tanh(wl) @ lw['w2'])) - 0.5
    decay = jnp.exp(-jnp.exp(w))
    a = jax.nn.sigmoid(lw['a0'] + al @ lw['a2'])
    g = jax.nn.sigmoid(gl) @ lw['g2']
    hs = lambda t: t.reshape(B_, T_, H_RWKV, HEAD_DIM)
    kk = hs(k * lw['k_k'])
    kk = kk / jnp.maximum(jnp.linalg.norm(kk, axis=-1, keepdims=True), 1e-12)
    k = k * (1.0 + (a - 1.0) * lw['k_a'])
    r_h, k_h, v_h, d_h, a_h = hs(r), hs(k), hs(v), hs(decay), hs(a)

    def step(S, inp):
        r_t, d_t, k_t, v_t, kk_t, a_t = inp
        sa = jnp.einsum('bhvk,bhk->bhv', S, -kk_t)
        S = (S * d_t[:, :, None, :] + sa[..., None] * (kk_t * a_t)[:, :, None, :]
             + v_t[..., None] * k_t[:, :, None, :])
        return S, jnp.einsum('bhvk,bhk->bhv', S, r_t)

    tm = lambda t: jnp.moveaxis(t, 1, 0)
    S_fin, y = lax.scan(step, S0.astype(f32), (tm(r_h), tm(d_h), tm(k_h), tm(v_h), tm(kk), tm(a_h)))
    y = jnp.moveaxis(y, 0, 1)
    mu = y.mean(-1, keepdims=True)
    var = jnp.square(y - mu).mean(-1, keepdims=True)
    y = ((y - mu) * lax.rsqrt(var + LNX_EPS)).reshape(B_, T_, D_RWKV) * lw['lnx_g'] + lw['lnx_b']
    bonus = jnp.sum(r_h * k_h * lw['r_k'], axis=-1, keepdims=True) * v_h
    y = (y + bonus.reshape(B_, T_, D_RWKV)) * g
    return y.astype(zr.dtype), S_fin.astype(S0.dtype), zr[:, -1]


def moe_ffn(x, lw):
    N, D = x.shape
    f32 = jnp.float32
    scores = jax.nn.sigmoid(x.astype(f32) @ lw['router_w'].astype(f32))
    choice = scores + lw['router_bias'].astype(f32)
    grp = lax.top_k(choice.reshape(N, N_GROUPS, N_EXPERTS // N_GROUPS), 2)[0].sum(-1)
    _, top_g = lax.top_k(grp, TOPK_GROUPS)
    g_mask = (top_g[:, :, None] == jnp.arange(N_GROUPS)).any(axis=1)
    choice = jnp.where(jnp.repeat(g_mask, N_EXPERTS // N_GROUPS, axis=1), choice, -jnp.inf)
    _, top_e = lax.top_k(choice, TOP_K)
    gate = jnp.take_along_axis(scores, top_e, axis=1)
    gate = gate / jnp.sum(gate, -1, keepdims=True) * ROUTED_SCALE
    A = N * TOP_K
    flat_e = top_e.reshape(A)
    order = jnp.argsort(flat_e)
    se = flat_e[order]
    stok = (order // TOP_K).astype(jnp.int32)
    sw = gate.reshape(A)[order]
    counts = jnp.bincount(flat_e, length=N_EXPERTS)
    starts = jnp.cumsum(counts) - counts
    pcounts = (counts + EXPERT_BLOCK - 1) // EXPERT_BLOCK * EXPERT_BLOCK
    pends = jnp.cumsum(pcounts)
    dest = (pends - pcounts)[se] + jnp.arange(A) - starts[se]
    n_blocks = -(-A // EXPERT_BLOCK) + N_EXPERTS
    n_slots = n_blocks * EXPERT_BLOCK
    slot_tok = jnp.full((n_slots,), N, jnp.int32).at[dest].set(stok)
    slot_w = jnp.zeros((n_slots,), f32).at[dest].set(sw)
    blk_e = jnp.minimum(jnp.searchsorted(pends, jnp.arange(n_blocks) * EXPERT_BLOCK, side='right'), N_EXPERTS - 1)
    xs = jnp.concatenate([x, jnp.zeros((1, D), x.dtype)], axis=0)[slot_tok].reshape(n_blocks, EXPERT_BLOCK, D)

    def expert_block(args):
        xb, e = args
        h = jax.nn.silu(xb @ lw['e_gate'][e]) * (xb @ lw['e_up'][e])
        return h @ lw['e_down'][e]

    ys = lax.map(expert_block, (xs, blk_e)).reshape(n_slots, D)
    routed = jax.ops.segment_sum(ys.astype(f32) * slot_w[:, None], slot_tok, num_segments=N + 1)[:N]
    shared = (jax.nn.silu(x @ lw['s_gate']) * (x @ lw['s_up'])) @ lw['s_down']
    return (routed + shared.astype(f32)).astype(x.dtype)


def hybrid_layer(x, attend, shift0, S0, lw):
    B_, T_, _ = x.shape
    z = x @ lw['w_in']
    heads = lambda t: t.reshape(B_, T_, H_ATTN, HEAD_DIM)
    q = heads(z[..., :D_ATTN])
    k = heads(z[..., D_ATTN:2 * D_ATTN])
    v = heads(z[..., 2 * D_ATTN:3 * D_ATTN])
    o_attn = attend(q, k, v)
    o_rwkv, S_new, shift_new = rwkv_group(z[..., 3 * D_ATTN:], shift0, S0, lw)
    mix = jnp.concatenate([o_attn, o_rwkv], axis=-1) @ lw['w_out']
    x = layer_norm(ALPHA * x + mix, lw['ln1_g'], lw['ln1_b'])
    f = moe_ffn(x.reshape(B_ * T_, D_MODEL), lw).reshape(B_, T_, D_MODEL)
    x = layer_norm(ALPHA * x + f, lw['ln2_g'], lw['ln2_b'])
    return x, k, v, S_new, shift_new


def setup_inputs(seed: int = 0) -> dict:
    key = jax.random.key(seed)
    ks = iter(jax.random.split(key, 48))
    f32 = jnp.float32
    L = DEPTH
    n_pages = PAST_LEN // PAGE_SIZE
    n_pool = (5 * DEC_BATCH * n_pages) // 4

    def nrm(shape, scale=1.0):
        return jax.random.normal(next(ks), shape, f32) * scale

    def unif(shape, lo, hi):
        return jax.random.uniform(next(ks), shape, f32, lo, hi)

    page_table = jax.random.permutation(next(ks), n_pool)[: DEC_BATCH * n_pages].reshape(DEC_BATCH, n_pages).astype(jnp.int32)
    return {
        'x_prompt': nrm((BATCH, SEQ, D_MODEL)),
        'x_sample': nrm((DEC_BATCH, DEC_SEQ, D_MODEL)),
        'cache_k': nrm((L, n_pool, PAGE_SIZE, H_ATTN, HEAD_DIM)),
        'cache_v': nrm((L, n_pool, PAGE_SIZE, H_ATTN, HEAD_DIM)),
        'state_wkv': nrm((L, DEC_BATCH, H_RWKV, HEAD_DIM, HEAD_DIM), 0.3),
        'state_shift': nrm((L, DEC_BATCH, C_SHIFT)),
        'page_table': page_table,
        'rel_bias': nrm((N_BUCKETS, H_ATTN), 0.5),
        'w_in': nrm((L, D_MODEL, D_IN), D_MODEL ** -0.5),
        'mu_shift': unif((L, C_SHIFT), 0.0, 1.0),
        'w0': unif((L, D_RWKV), -5.0, -1.0),
        'w2': nrm((L, DECAY_LORA, D_RWKV), 0.1),
        'a0': nrm((L, D_RWKV), 0.5),
        'a2': nrm((L, AAA_LORA, D_RWKV), AAA_LORA ** -0.5),
        'g2': nrm((L, GATE_LORA, D_RWKV), GATE_LORA ** -0.5),
        'k_k': 0.85 + nrm((L, D_RWKV), 0.05),
        'k_a': 1.0 + nrm((L, D_RWKV), 0.05),
        'r_k': nrm((L, H_RWKV, HEAD_DIM), 0.1),
        'lnx_g': 1.0 + nrm((L, D_RWKV), 0.05),
        'lnx_b': nrm((L, D_RWKV), 0.02),
        'w_out': nrm((L, D_MODEL, D_MODEL), BETA * D_MODEL ** -0.5),
        'ln1_g': 1.0 + nrm((L, D_MODEL), 0.05),
        'ln1_b': nrm((L, D_MODEL), 0.02),
        'router_w': nrm((L, D_MODEL, N_EXPERTS), D_MODEL ** -0.5),
        'router_bias': nrm((L, N_EXPERTS), 0.01),
        'e_gate': nrm((L, N_EXPERTS, D_MODEL, D_EXPERT), D_MODEL ** -0.5),
        'e_up': nrm((L, N_EXPERTS, D_MODEL, D_EXPERT), D_MODEL ** -0.5),
        'e_down': nrm((L, N_EXPERTS, D_EXPERT, D_MODEL), BETA * D_EXPERT ** -0.5),
        's_gate': nrm((L, D_MODEL, D_SHARED), D_MODEL ** -0.5),
        's_up': nrm((L, D_MODEL, D_SHARED), D_MODEL ** -0.5),
        's_down': nrm((L, D_SHARED, D_MODEL), BETA * D_SHARED ** -0.5),
        'ln2_g': 1.0 + nrm((L, D_MODEL), 0.05),
        'ln2_b': nrm((L, D_MODEL), 0.02),
    }


def reference(x_prompt, x_sample, cache_k, cache_v, state_wkv, state_shift, page_table, rel_bias,
              w_in, mu_shift, w0, w2, a0, a2, g2, k_k, k_a, r_k, lnx_g, lnx_b, w_out, ln1_g, ln1_b,
              router_w, router_bias, e_gate, e_up, e_down, s_gate, s_up, s_down, ln2_g, ln2_b):
    y_prompt, y_sample = x_prompt, x_sample
    kp_l, vp_l, wp_l, sp_l = [], [], [], []
    ks_l, vs_l, ws_l, ss_l = [], [], [], []
    for l in range(DEPTH):
        lw = {'w_in': w_in[l], 'mu_shift': mu_shift[l], 'w0': w0[l], 'w2': w2[l], 'a0': a0[l], 'a2': a2[l],
              'g2': g2[l], 'k_k': k_k[l], 'k_a': k_a[l], 'r_k': r_k[l], 'lnx_g': lnx_g[l], 'lnx_b': lnx_b[l],
              'w_out': w_out[l], 'ln1_g': ln1_g[l], 'ln1_b': ln1_b[l], 'router_w': router_w[l],
              'router_bias': router_bias[l], 'e_gate': e_gate[l], 'e_up': e_up[l], 'e_down': e_down[l],
              's_gate': s_gate[l], 's_up': s_up[l], 's_down': s_down[l], 'ln2_g': ln2_g[l], 'ln2_b': ln2_b[l]}
        b = y_prompt.shape[0]
        y_prompt, kp, vp, wp, sp = hybrid_layer(
            y_prompt, lambda q, k, v: moba_prompt(q, k, v, rel_bias),
            jnp.zeros((b, C_SHIFT), x_prompt.dtype),
            jnp.zeros((b, H_RWKV, HEAD_DIM, HEAD_DIM), x_prompt.dtype), lw)
        y_sample, ks_, vs_, ws_, ss_ = hybrid_layer(
            y_sample, lambda q, k, v, l=l: moba_sample(q, k, v, cache_k, cache_v, page_table, l, rel_bias),
            state_shift[l], state_wkv[l], lw)
        kp_l.append(kp); vp_l.append(vp); wp_l.append(wp); sp_l.append(sp)
        ks_l.append(ks_); vs_l.append(vs_); ws_l.append(ws_); ss_l.append(ss_)
    return (y_prompt, y_sample,
            jnp.stack(kp_l), jnp.stack(vp_l), jnp.stack(wp_l), jnp.stack(sp_l),
            jnp.stack(ks_l), jnp.stack(vs_l), jnp.stack(ws_l), jnp.stack(ss_l))
```

```python
import functools
import math

import numpy as np
import jax
import jax.numpy as jnp
from jax import lax
from jax.experimental import pallas as pl
from jax.experimental.pallas import tpu as pltpu

F32 = jnp.float32
BF16 = jnp.bfloat16

D_MODEL = 2048
HEAD_DIM = 64
D_ATTN = 1024
D_RWKV = 1024
H_ATTN = 16
H_RWKV = 16
N_PAIRS = 8
LANES = 128
BLOCK = 256
TOP_BLOCKS = 3
PAGE_SIZE = 128
N_BUCKETS = 32
MAX_DISTANCE = 128
DECAY_LORA = 64
AAA_LORA = 64
GATE_LORA = 128
C_SHIFT = 3 * D_RWKV + DECAY_LORA + AAA_LORA + GATE_LORA
LNX_EPS = 64e-5
LN_EPS = 1e-5
N_EXPERTS = 64
TOP_K = 8
N_GROUPS = 8
TOPK_GROUPS = 4
D_EXPERT = 512
ROUTED_SCALE = 2.5
DEPTH = 1
ALPHA = (2.0 * DEPTH) ** 0.25
CHUNK = 64
MOE_TM = 256
VMEM_LIMIT = 56 * 1024 * 1024
NEG_INF = float("-inf")


def _cparams(*sem):
    return pltpu.CompilerParams(dimension_semantics=sem, vmem_limit_bytes=VMEM_LIMIT)


def _dot(a, b):
    return jnp.dot(a.astype(BF16), b.astype(BF16), preferred_element_type=F32)


def _dot_nt(a, b):
    return lax.dot_general(a.astype(BF16), b.astype(BF16), (((1,), (1,)), ((), ())), preferred_element_type=F32)


def _dot_tn(a, b):
    return lax.dot_general(a.astype(BF16), b.astype(BF16), (((0,), (0,)), ((), ())), preferred_element_type=F32)


def _dot_hi(a, b):
    return jnp.dot(a, b, precision=lax.Precision.HIGHEST, preferred_element_type=F32)


def _sigmoid(x):
    return 1.0 / (1.0 + jnp.exp(-x))


def _mm_kernel(x_ref, w_ref, o_ref):
    o_ref[...] = jnp.dot(x_ref[...], w_ref[...], preferred_element_type=F32)


def _matmul(x, w, tm, tn):
    m, k = x.shape
    n = w.shape[1]
    return pl.pallas_call(
        _mm_kernel,
        grid=(m // tm, n // tn),
        in_specs=[pl.BlockSpec((tm, k), lambda i, j: (i, 0)),
                  pl.BlockSpec((k, tn), lambda i, j: (0, j))],
        out_specs=pl.BlockSpec((tm, tn), lambda i, j: (i, j)),
        out_shape=jax.ShapeDtypeStruct((m, n), F32),
        compiler_params=_cparams("parallel", "parallel"),
        name="in_proj",
    )(x, w)


def _bucket_table(max_dist):
    n = np.arange(max_dist + 1)
    max_exact = N_BUCKETS // 2
    nf = np.maximum(n, 1).astype(np.float64)
    large = max_exact + (np.log(nf / max_exact) / math.log(MAX_DISTANCE / max_exact)
                         * (N_BUCKETS - max_exact)).astype(np.int64)
    large = np.minimum(large, N_BUCKETS - 1)
    return np.where(n < max_exact, n, large).astype(np.int32)


def _prompt_bias_tiles(rel_bias):
    kpos = np.arange(BLOCK)[:, None]
    qpos = np.arange(BLOCK)[None, :]
    bt = _bucket_table(2 * BLOCK)
    d_own = qpos - kpos
    b_own = bt[np.maximum(d_own, 0)]
    b_prev = bt[BLOCK + qpos - kpos]
    rb = rel_bias.astype(F32)
    own = jnp.where(jnp.asarray(d_own >= 0)[..., None], rb[b_own], NEG_INF)
    prev = rb[b_prev]
    t = jnp.stack([own, prev], axis=0)
    t = t.transpose(3, 0, 1, 2).reshape(N_PAIRS, 2, 2, BLOCK, BLOCK)
    return t


def _moba_prompt_kernel(far_ref, q_ref, k_ref, v_ref, bias_ref, o_ref, kb_s, vt_s, km_s, sel_s, *, n_blk):
    p = pl.program_id(1)
    i = pl.program_id(2)

    @pl.when(i == 0)
    def _():
        def fill(j, c):
            r0 = pl.multiple_of(j * BLOCK, BLOCK)
            kblk = k_ref[pl.ds(r0, BLOCK), :]
            vblk = v_ref[pl.ds(r0, BLOCK), :]
            kb_s[pl.ds(r0, BLOCK), :] = kblk.astype(BF16)
            vt_s[j] = vblk.T.astype(BF16)
            km_s[pl.ds(j, 1), :] = jnp.sum(kblk, axis=0, keepdims=True) * (1.0 / BLOCK)
            return c
        lax.fori_loop(0, n_blk, fill, 0)

    q = q_ref[...]
    qt = q.T * (HEAD_DIM ** -0.5)
    row = lax.broadcasted_iota(jnp.int32, (LANES, BLOCK), 0)
    head0 = row < HEAD_DIM
    qts = [jnp.where(head0, qt, 0.0), jnp.where(head0, 0.0, qt)]
    qtb = [x.astype(BF16) for x in qts]

    blk = lax.broadcasted_iota(jnp.int32, (n_blk, BLOCK), 0)
    valid = blk < i
    km = km_s[...]
    for h in range(2):
        g = jnp.where(valid, _dot_hi(km, qts[h]), NEG_INF)
        sel = jnp.zeros((n_blk, BLOCK), F32)
        for _ in range(TOP_BLOCKS):
            m = jnp.max(g, axis=0, keepdims=True)
            first = jnp.min(jnp.where(g == m, blk, n_blk), axis=0, keepdims=True)
            hit = blk == first
            sel = jnp.where(hit & valid, 1.0, sel)
            g = jnp.where(hit, NEG_INF, g)
        sel_s[h] = sel

    far = [far_ref[2 * p], far_ref[2 * p + 1]]

    def scores(j, h):
        r0 = pl.multiple_of(j * BLOCK, BLOCK)
        return jnp.dot(kb_s[pl.ds(r0, BLOCK), :], qtb[h], preferred_element_type=F32)

    def pv(j, pb0, pb1):
        vt = vt_s[j]
        return jnp.concatenate(
            [jnp.dot(vt[:HEAD_DIM], pb0, preferred_element_type=F32),
             jnp.dot(vt[HEAD_DIM:], pb1, preferred_element_type=F32)], axis=0)

    def update(j, s0, s1, carry):
        m0, m1, l0, l1, acc = carry
        outs = []
        for s, m, l in ((s0, m0, l0), (s1, m1, l1)):
            mn = jnp.maximum(m, jnp.max(s, axis=0, keepdims=True))
            a = jnp.exp(m - mn)
            pr = jnp.exp(s - mn)
            outs.append((mn, a, a * l + jnp.sum(pr, axis=0, keepdims=True), pr.astype(BF16)))
        (mn0, a0, ln0, pb0), (mn1, a1, ln1, pb1) = outs
        acc = jnp.where(head0, a0, a1) * acc + pv(j, pb0, pb1)
        return mn0, mn1, ln0, ln1, acc

    s0 = scores(i, 0) + bias_ref[0, 0, 0]
    s1 = scores(i, 1) + bias_ref[0, 1, 0]
    m0 = jnp.max(s0, axis=0, keepdims=True)
    m1 = jnp.max(s1, axis=0, keepdims=True)
    p0 = jnp.exp(s0 - m0)
    p1 = jnp.exp(s1 - m1)
    carry = (m0, m1, jnp.sum(p0, axis=0, keepdims=True), jnp.sum(p1, axis=0, keepdims=True),
             pv(i, p0.astype(BF16), p1.astype(BF16)))

    jp = jnp.maximum(i - 1, 0)
    has_prev = i > 0
    keep0 = (sel_s[0, pl.ds(jp, 1), :] > 0.0) & has_prev
    keep1 = (sel_s[1, pl.ds(jp, 1), :] > 0.0) & has_prev
    s0 = jnp.where(keep0, scores(jp, 0) + bias_ref[0, 0, 1], NEG_INF)
    s1 = jnp.where(keep1, scores(jp, 1) + bias_ref[0, 1, 1], NEG_INF)
    carry = update(jp, s0, s1, carry)

    def far_body(j, carry):
        s0 = jnp.where(sel_s[0, pl.ds(j, 1), :] > 0.0, scores(j, 0) + far[0], NEG_INF)
        s1 = jnp.where(sel_s[1, pl.ds(j, 1), :] > 0.0, scores(j, 1) + far[1], NEG_INF)
        return update(j, s0, s1, carry)

    m0, m1, l0, l1, acc = lax.fori_loop(0, jnp.maximum(i - 1, 0), far_body, carry)
    ot = acc / jnp.where(head0, l0, l1)
    o_ref[...] = ot.T


def _moba_prompt(zqkv, rel_bias, n_batch, seq):
    n_blk = seq // BLOCK
    bias = _prompt_bias_tiles(rel_bias)
    far = rel_bias[N_BUCKETS - 1].astype(F32)
    kern = functools.partial(_moba_prompt_kernel, n_blk=n_blk)
    return pl.pallas_call(
        kern,
        grid=(n_batch, N_PAIRS, n_blk),
        in_specs=[
            pl.BlockSpec(memory_space=pltpu.SMEM),
            pl.BlockSpec((BLOCK, LANES), lambda b, p, i: (b * n_blk + i, p)),
            pl.BlockSpec((seq, LANES), lambda b, p, i: (b, N_PAIRS + p)),
            pl.BlockSpec((seq, LANES), lambda b, p, i: (b, 2 * N_PAIRS + p)),
            pl.BlockSpec((1, 2, 2, BLOCK, BLOCK), lambda b, p, i: (p, 0, 0, 0, 0)),
        ],
        out_specs=pl.BlockSpec((BLOCK, LANES), lambda b, p, i: (b * n_blk + i, p)),
        out_shape=jax.ShapeDtypeStruct((n_batch * seq, D_ATTN), F32),
        scratch_shapes=[pltpu.VMEM((seq, LANES), BF16),
                        pltpu.VMEM((n_blk, LANES, BLOCK), BF16),
                        pltpu.VMEM((n_blk, LANES), F32),
                        pltpu.VMEM((2, n_blk, BLOCK), F32)],
        compiler_params=_cparams("parallel", "parallel", "arbitrary"),
        name="moba_prompt",
    )(far, zqkv, zqkv, zqkv, bias)


PAGES_PER_STEP = 8


def _sample_gate_kernel(pt_ref, q_ref, *refs, n_past_blk, n_steps):
    page_refs = refs[:PAGES_PER_STEP]
    idx_ref = refs[PAGES_PER_STEP]
    acc_s = refs[PAGES_PER_STEP + 1]
    s = pl.program_id(1)
    lane = lax.broadcasted_iota(jnp.int32, (H_ATTN * HEAD_DIM, LANES), 1)

    @pl.when(s == 0)
    def _():
        acc_s[...] = jnp.zeros_like(acc_s)

    acc = acc_s[...]
    ppb = BLOCK // PAGE_SIZE
    for bl in range(PAGES_PER_STEP // ppb):
        x = page_refs[ppb * bl][0]
        for e in range(1, ppb):
            x = x + page_refs[ppb * bl + e][0]
        cs = jnp.sum(x, axis=-1, keepdims=True)
        acc = acc + jnp.where(lane == s * (PAGES_PER_STEP // ppb) + bl, cs, 0.0)
    acc_s[...] = acc

    @pl.when(s == n_steps - 1)
    def _():
        lane8 = lax.broadcasted_iota(jnp.int32, (8, LANES), 1)
        valid = lane8 < n_past_blk
        for h in range(H_ATTN):
            kmt = acc_s[h * HEAD_DIM:(h + 1) * HEAD_DIM, :] * (1.0 / BLOCK)
            qh = q_ref[0, :, h * HEAD_DIM:(h + 1) * HEAD_DIM]
            g = jnp.where(valid, _dot_hi(qh, kmt), NEG_INF)
            out = jnp.zeros((8, LANES), jnp.int32)
            for t in range(TOP_BLOCKS):
                m = jnp.max(g, axis=-1, keepdims=True)
                first = jnp.min(jnp.where(g == m, lane8, LANES), axis=-1, keepdims=True)
                out = jnp.where(lane8 == t, first, out)
                g = jnp.where(lane8 == first, NEG_INF, g)
            idx_ref[0, h] = out


def _sample_gate(page_table, q8, cache_k_pages, n_past_blk):
    db = q8.shape[0]
    n_pages = page_table.shape[1]
    n_steps = n_pages // PAGES_PER_STEP
    kern = functools.partial(_sample_gate_kernel, n_past_blk=n_past_blk, n_steps=n_steps)

    def page_spec(r):
        return pl.BlockSpec((1, H_ATTN * HEAD_DIM, PAGE_SIZE),
                            lambda b, s, pt: (pt[b, s * PAGES_PER_STEP + r], 0, 0))

    return pl.pallas_call(
        kern,
        grid_spec=pltpu.PrefetchScalarGridSpec(
            num_scalar_prefetch=1,
            grid=(db, n_steps),
            in_specs=[pl.BlockSpec((1, 8, H_ATTN * HEAD_DIM), lambda b, s, pt: (b, 0, 0))]
            + [page_spec(r) for r in range(PAGES_PER_STEP)],
            out_specs=pl.BlockSpec((1, H_ATTN, 8, LANES), lambda b, s, pt: (b, 0, 0, 0)),
            scratch_shapes=[pltpu.VMEM((H_ATTN * HEAD_DIM, LANES), F32)],
        ),
        out_shape=jax.ShapeDtypeStruct((db, H_ATTN, 8, LANES), jnp.int32),
        compiler_params=_cparams("parallel", "arbitrary"),
        name="sample_gate",
    )(page_table, q8, *([cache_k_pages] * PAGES_PER_STEP))


def _sample_attn_kernel(slab_ref, idx_ref, qt_ref, kn_ref, vn_ref, bown_ref, blast_ref, far_ref, *refs,
                        dec_seq, n_past_blk):
    n_slab = dec_seq * TOP_BLOCKS * (BLOCK // PAGE_SIZE)
    k_refs = refs[:n_slab]
    v_refs = refs[n_slab:2 * n_slab]
    o_ref = refs[2 * n_slab]
    b = pl.program_id(0)
    h = pl.program_id(1)
    ppb = BLOCK // PAGE_SIZE
    lane8 = lax.broadcasted_iota(jnp.int32, (HEAD_DIM, 8), 1)
    out = jnp.zeros((HEAD_DIM, 8), F32)
    kn = kn_ref[0, 0]
    vn = vn_ref[0, 0]
    far = far_ref[h]
    for qi in range(dec_seq):
        qc = qt_ref[0, 0][:, qi:qi + 1] * (HEAD_DIM ** -0.5)
        rows = []
        for t in range(TOP_BLOCKS):
            sel_blk = idx_ref[((b * H_ATTN + h) * dec_seq + qi) * TOP_BLOCKS + t]
            is_last = sel_blk == n_past_blk - 1
            for e in range(ppb):
                slab = k_refs[(qi * TOP_BLOCKS + t) * ppb + e][0]
                srow = jnp.sum(slab * qc, axis=0, keepdims=True)
                rows.append(srow + jnp.where(is_last, blast_ref[0, qi, e:e + 1, :], far))
        s_own = jnp.sum(kn * qc, axis=0, keepdims=True) + bown_ref[0, qi:qi + 1, :]
        m = jnp.max(s_own, axis=-1, keepdims=True)
        for r in rows:
            m = jnp.maximum(m, jnp.max(r, axis=-1, keepdims=True))
        p_own = jnp.exp(s_own - m)
        l = jnp.sum(p_own, axis=-1, keepdims=True)
        acc = jnp.zeros((HEAD_DIM, PAGE_SIZE), F32)
        for n, r in enumerate(rows):
            pr = jnp.exp(r - m)
            l = l + jnp.sum(pr, axis=-1, keepdims=True)
            acc = acc + v_refs[qi * TOP_BLOCKS * ppb + n][0] * pr
        o = jnp.sum(acc, axis=-1, keepdims=True) + jnp.sum(vn * p_own, axis=-1, keepdims=True)
        out = jnp.where(lane8 == qi, o / l, out)
    o_ref[0, 0] = out


def _sample_attn(slabs, idx_flat, qt, knt, vnt, bias_own, bias_last, far, ck_slabs, cv_slabs, dec_seq, n_past_blk):
    db = qt.shape[0]
    ppb = BLOCK // PAGE_SIZE
    n_slab = dec_seq * TOP_BLOCKS * ppb
    kern = functools.partial(_sample_attn_kernel, dec_seq=dec_seq, n_past_blk=n_past_blk)

    def slab_spec(n):
        return pl.BlockSpec((1, HEAD_DIM, PAGE_SIZE),
                            lambda b, h, sl, ix: (sl[(b * H_ATTN + h) * n_slab + n], 0, 0))

    small = lambda shape: pl.BlockSpec(shape, lambda b, h, sl, ix: (b, h, 0, 0))
    return pl.pallas_call(
        kern,
        grid_spec=pltpu.PrefetchScalarGridSpec(
            num_scalar_prefetch=2,
            grid=(db, H_ATTN),
            in_specs=[small((1, 1, HEAD_DIM, 8)), small((1, 1, HEAD_DIM, 8)), small((1, 1, HEAD_DIM, 8)),
                      pl.BlockSpec((1, dec_seq, 8), lambda b, h, sl, ix: (h, 0, 0)),
                      pl.BlockSpec((1, dec_seq, ppb, PAGE_SIZE), lambda b, h, sl, ix: (h, 0, 0, 0)),
                      pl.BlockSpec(memory_space=pltpu.SMEM)]
            + [slab_spec(n) for n in range(n_slab)] * 2,
            out_specs=small((1, 1, HEAD_DIM, 8)),
        ),
        out_shape=jax.ShapeDtypeStruct((db, H_ATTN, HEAD_DIM, 8), F32),
        compiler_params=_cparams("parallel", "parallel"),
        name="sample_attn",
    )(slabs, idx_flat, qt, knt, vnt, bias_own, bias_last, far, *([ck_slabs] * n_slab), *([cv_slabs] * n_slab))


def _moba_sample(zqkv_s, cache_k, cache_v, page_table, rel_bias, db, dec_seq, past_len):
    n_past_blk = past_len // BLOCK
    assert past_len % BLOCK == 0 and n_past_blk >= TOP_BLOCKS and n_past_blk <= LANES
    ppb = BLOCK // PAGE_SIZE
    pool = cache_k.shape[1]
    ckt = cache_k[0].transpose(0, 2, 3, 1)
    cvt = cache_v[0].transpose(0, 2, 3, 1)
    z3 = zqkv_s.reshape(db, dec_seq, 3 * D_ATTN)
    q = z3[..., :D_ATTN]
    q8 = jnp.pad(q, ((0, 0), (0, 8 - dec_seq), (0, 0)))
    idx = _sample_gate(page_table, q8, ckt.reshape(pool, H_ATTN * HEAD_DIM, PAGE_SIZE), n_past_blk)
    idx = idx[:, :, :dec_seq, :TOP_BLOCKS]
    pages = jnp.take_along_axis(page_table[:, None, None, None, :],
                                (idx[..., None] * ppb + jnp.arange(ppb)).reshape(db, H_ATTN, dec_seq, 1, -1),
                                axis=-1)
    slabs = (pages.reshape(db, H_ATTN, -1) * H_ATTN + jnp.arange(H_ATTN)[None, :, None]).reshape(-1).astype(jnp.int32)

    def heads_t(t):
        t = t.reshape(db, dec_seq, H_ATTN, HEAD_DIM).transpose(0, 2, 3, 1)
        return jnp.pad(t, ((0, 0), (0, 0), (0, 0), (0, 8 - dec_seq)))

    bt = _bucket_table(BLOCK + dec_seq)
    rb = rel_bias.astype(F32)
    qi = np.arange(dec_seq)[:, None]
    t8 = np.arange(8)[None, :]
    own_ok = (t8 <= qi) & (t8 < dec_seq)
    bias_own = jnp.where(jnp.asarray(own_ok)[..., None], rb[bt[np.maximum(qi - t8, 0)]], NEG_INF).transpose(2, 0, 1)
    off = np.arange(BLOCK).reshape(ppb, PAGE_SIZE)[None]
    bias_last = rb[bt[BLOCK + qi[:, :, None] - off]].transpose(3, 0, 1, 2)
    o = _sample_attn(slabs, idx.reshape(-1).astype(jnp.int32), heads_t(q), heads_t(z3[..., D_ATTN:2 * D_ATTN]),
                     heads_t(z3[..., 2 * D_ATTN:]), bias_own, bias_last, rb[N_BUCKETS - 1],
                     ckt.reshape(pool * H_ATTN, HEAD_DIM, PAGE_SIZE), cvt.reshape(pool * H_ATTN, HEAD_DIM, PAGE_SIZE),
                     dec_seq, n_past_blk)
    return o[..., :dec_seq].transpose(0, 3, 1, 2).reshape(db * dec_seq, D_ATTN)


def _pair_ones():
    r = np.arange(LANES)
    return jnp.asarray((r[:, None] // HEAD_DIM) == (r[None, :] // HEAD_DIM), BF16)


def _head_sum(x, ones_pair):
    hi = x.astype(BF16)
    lo = (x - hi.astype(F32)).astype(BF16)
    outs = []
    for g in range(N_PAIRS):
        sl = slice(g * LANES, (g + 1) * LANES)
        outs.append(jnp.dot(hi[:, sl], ones_pair, preferred_element_type=F32)
                    + jnp.dot(lo[:, sl], ones_pair, preferred_element_type=F32))
    return jnp.concatenate(outs, axis=-1)


def _rwkv_prep_kernel(z_ref, prev_ref, mu_ref, w0_ref, w2_ref, a0_ref, a2_ref, g2_ref, kk_ref, ka_ref, ones_ref,
                      r_ref, ld_ref, k_ref, v_ref, kap_ref, b_ref, g_ref):
    z = z_ref[...]
    zs = z + (prev_ref[...] - z) * mu_ref[...]
    o = D_RWKV
    r = zs[:, :o]
    k = zs[:, o:2 * o]
    v = zs[:, 2 * o:3 * o]
    lora = zs[:, 3 * o:3 * o + LANES]
    gl = zs[:, 3 * o + LANES:]
    x = w0_ref[...] + _dot(jnp.tanh(lora), w2_ref[...])
    nx = -x
    w = -(jnp.maximum(nx, 0.0) + jnp.log(1.0 + jnp.exp(-jnp.abs(nx)))) - 0.5
    logd = -jnp.exp(w)
    a = _sigmoid(a0_ref[...] + _dot(lora, a2_ref[...]))
    g = _dot(_sigmoid(gl), g2_ref[...])
    kk = k * kk_ref[...]
    nrm = jnp.sqrt(_head_sum(kk * kk, ones_ref[...]))
    kap = kk / jnp.maximum(nrm, 1e-12)
    k2 = k * (1.0 + (a - 1.0) * ka_ref[...])
    for p in range(N_PAIRS):
        sl = slice(p * LANES, (p + 1) * LANES)
        r_ref[p] = r[:, sl]
        ld_ref[p] = logd[:, sl]
        k_ref[p] = k2[:, sl]
        v_ref[p] = v[:, sl]
        kap_ref[p] = kap[:, sl]
        b_ref[p] = (kap * a)[:, sl]
        g_ref[p] = g[:, sl]


def _rwkv_prep(zr, prev, lw, tm):
    n = zr.shape[0]
    row = lambda a: a.reshape(1, -1).astype(F32)
    zpad = jnp.zeros((DECAY_LORA, D_RWKV), F32)
    w2p = jnp.concatenate([lw['w2'], zpad], axis=0).astype(BF16)
    a2p = jnp.concatenate([zpad, lw['a2']], axis=0).astype(BF16)
    full = lambda shape: pl.BlockSpec(shape, lambda i: (0,) * len(shape))
    out_sd = jax.ShapeDtypeStruct((N_PAIRS, n, LANES), F32)
    out_spec = pl.BlockSpec((N_PAIRS, tm, LANES), lambda i: (0, i, 0))
    return pl.pallas_call(
        _rwkv_prep_kernel,
        grid=(n // tm,),
        in_specs=[pl.BlockSpec((tm, C_SHIFT), lambda i: (i, 0)), pl.BlockSpec((tm, C_SHIFT), lambda i: (i, 0)),
                  full((1, C_SHIFT)), full((1, D_RWKV)), full((LANES, D_RWKV)), full((1, D_RWKV)),
                  full((LANES, D_RWKV)), full((GATE_LORA, D_RWKV)), full((1, D_RWKV)), full((1, D_RWKV)),
                  full((LANES, LANES))],
        out_specs=[out_spec] * 7,
        out_shape=[out_sd] * 7,
        compiler_params=_cparams("parallel"),
        name="rwkv_prep",
    )(zr, prev, row(lw['mu_shift']), row(lw['w0']), w2p, row(lw['a0']), a2p, lw['g2'].astype(BF16),
      row(lw['k_k']), row(lw['k_a']), _pair_ones())


def _rwkv_chunk_kernel(r_ref, ld_ref, k_ref, v_ref, kap_ref, b_ref, g_ref, s0_ref, lg_ref, lb_ref, rk_ref,
                       y_ref, s_ref):
    c = pl.program_id(1)

    @pl.when(c == 0)
    def _():
        s_ref[...] = s0_ref[...]

    ti = lax.broadcasted_iota(jnp.int32, (CHUNK, CHUNK), 0)
    si = lax.broadcasted_iota(jnp.int32, (CHUNK, CHUNK), 1)
    tril_incl = (si <= ti)
    tril_strict = (si < ti)
    ltri = tril_incl.astype(F32)
    eye = (si == ti).astype(F32)

    def pair_body(p, carry):
        r2 = r_ref[p]
        ld = ld_ref[p]
        k2 = k_ref[p]
        v2 = v_ref[p]
        kap2 = kap_ref[p]
        b2 = b_ref[p]
        cum = _dot_hi(ltri, ld)
        last = cum[CHUNK - 1:CHUNK, :]
        gam = jnp.exp(cum)
        gam_prev = jnp.exp(cum - ld)
        gam_inv = jnp.exp(-cum)
        gam_tail = jnp.exp(last - cum)
        gam_all = jnp.exp(last)
        kt2 = kap2 * gam_prev
        rt2 = r2 * gam
        bi2 = b2 * gam_inv
        ki2 = k2 * gam_inv
        bh2 = b2 * gam_tail
        kh2 = k2 * gam_tail
        ys = []
        for h in range(2):
            sl = slice(h * HEAD_DIM, (h + 1) * HEAD_DIM)
            kt, rt, bi, ki, bh, kh, vv = kt2[:, sl], rt2[:, sl], bi2[:, sl], ki2[:, sl], bh2[:, sl], kh2[:, sl], v2[:, sl]
            kr = jnp.concatenate([kt, rt], axis=0)
            xb = _dot_nt(kr, bi)
            xk = _dot_nt(kr, ki)
            a_ab = jnp.where(tril_strict, xb[:CHUNK], 0.0)
            a_rb = jnp.where(tril_incl, xb[CHUNK:], 0.0)
            a_ak = jnp.where(tril_strict, xk[:CHUNK], 0.0)
            a_rk = jnp.where(tril_incl, xk[CHUNK:], 0.0)
            tm = eye - a_ab
            pw = _dot(a_ab, a_ab)
            n_sq = int(math.log2(CHUNK)) - 1
            for it in range(n_sq):
                tm = tm + _dot(tm, pw)
                if it < n_sq - 1:
                    pw = _dot(pw, pw)
            w = -_dot(tm, kt)
            u0 = -_dot(tm, _dot(a_ak, vv))
            pm = rt + _dot(a_rb, w)
            y0 = _dot(a_rb, u0) + _dot(a_rk, vv)
            s = s_ref[0, 2 * p + h]
            y = _dot_nt(pm, s) + y0
            mt = _dot_tn(bh, w)
            nt = _dot_tn(jnp.concatenate([u0, vv], axis=0), jnp.concatenate([bh, kh], axis=0))
            s_ref[0, 2 * p + h] = s * gam_all[:, sl] + _dot_nt(s, mt) + nt
            mu = jnp.mean(y, axis=-1, keepdims=True)
            var = jnp.mean(jnp.square(y - mu), axis=-1, keepdims=True)
            yn = (y - mu) * lax.rsqrt(var + LNX_EPS)
            bonus = jnp.sum(r2[:, sl] * k2[:, sl] * rk_ref[p][:, sl], axis=-1, keepdims=True) * vv
            ys.append(yn * lg_ref[p][:, sl] + lb_ref[p][:, sl] + bonus)
        y_ref[p] = jnp.concatenate(ys, axis=-1) * g_ref[p]
        return carry

    lax.fori_loop(0, N_PAIRS, pair_body, 0)


def _rwkv_chunks(prep, s0, lw, n_batch, seq):
    n_chunks = seq // CHUNK
    pairs = lambda a: a.reshape(N_PAIRS, 1, LANES).astype(F32)
    tile = pl.BlockSpec((N_PAIRS, CHUNK, LANES), lambda b, c: (0, b * n_chunks + c, 0))
    state = pl.BlockSpec((1, H_RWKV, HEAD_DIM, HEAD_DIM), lambda b, c: (b, 0, 0, 0))
    par = pl.BlockSpec((N_PAIRS, 1, LANES), lambda b, c: (0, 0, 0))
    return pl.pallas_call(
        _rwkv_chunk_kernel,
        grid=(n_batch, n_chunks),
        in_specs=[tile] * 7 + [state, par, par, par],
        out_specs=[tile, state],
        out_shape=[jax.ShapeDtypeStruct((N_PAIRS, n_batch * seq, LANES), F32),
                   jax.ShapeDtypeStruct((n_batch, H_RWKV, HEAD_DIM, HEAD_DIM), F32)],
        compiler_params=_cparams("parallel", "arbitrary"),
        name="rwkv_chunks",
    )(*prep, s0.astype(F32), pairs(lw['lnx_g']), pairs(lw['lnx_b']), pairs(lw['r_k']))


def _rwkv(zr, shift0, s0, lw, n_batch, seq):
    z3 = zr.reshape(n_batch, seq, C_SHIFT)
    prev = jnp.concatenate([shift0[:, None].astype(F32), z3[:, :-1]], axis=1).reshape(n_batch * seq, C_SHIFT)
    prep = _rwkv_prep(zr, prev, lw, min(256, n_batch * seq))
    pad = (-seq) % CHUNK
    if pad:
        prep = [jnp.pad(a.reshape(N_PAIRS, n_batch, seq, LANES), ((0, 0), (0, 0), (0, pad), (0, 0)))
                .reshape(N_PAIRS, n_batch * (seq + pad), LANES) for a in prep]
    y, s_fin = _rwkv_chunks(prep, s0, lw, n_batch, seq + pad)
    if pad:
        y = y.reshape(N_PAIRS, n_batch, seq + pad, LANES)[:, :, :seq].reshape(N_PAIRS, n_batch * seq, LANES)
    return y, s_fin, z3[:, -1]


def _layer_norm(h, g, b):
    mu = jnp.mean(h, axis=-1, keepdims=True)
    d = h - mu
    var = jnp.mean(d * d, axis=-1, keepdims=True)
    return d * lax.rsqrt(var + LN_EPS) * g + b


def _out_proj_kernel(oa_ref, or_ref, x_ref, wa_ref, wr_ref, g_ref, b_ref, rw_ref, x1_ref, x1b_ref, sc_ref):
    orw = jnp.concatenate([or_ref[p] for p in range(N_PAIRS)], axis=-1)
    mix = _dot(oa_ref[...], wa_ref[...]) + _dot(orw, wr_ref[...])
    x1 = _layer_norm(ALPHA * x_ref[...] + mix, g_ref[...], b_ref[...])
    x1_ref[...] = x1
    x1b_ref[...] = x1.astype(BF16)
    sc_ref[...] = _sigmoid(_dot_hi(x1, rw_ref[...]))


def _out_proj(o_attn, o_rwkv, x, lw, tm):
    n = x.shape[0]
    wo = lw['w_out'].astype(BF16)
    full = lambda shape: pl.BlockSpec(shape, lambda i: (0,) * len(shape))
    return pl.pallas_call(
        _out_proj_kernel,
        grid=(n // tm,),
        in_specs=[pl.BlockSpec((tm, D_ATTN), lambda i: (i, 0)),
                  pl.BlockSpec((N_PAIRS, tm, LANES), lambda i: (0, i, 0)),
                  pl.BlockSpec((tm, D_MODEL), lambda i: (i, 0)),
                  full((D_ATTN, D_MODEL)), full((D_RWKV, D_MODEL)), full((1, D_MODEL)), full((1, D_MODEL)),
                  full((D_MODEL, N_EXPERTS))],
        out_specs=[pl.BlockSpec((tm, D_MODEL), lambda i: (i, 0)), pl.BlockSpec((tm, D_MODEL), lambda i: (i, 0)),
                   pl.BlockSpec((tm, N_EXPERTS), lambda i: (i, 0))],
        out_shape=[jax.ShapeDtypeStruct((n, D_MODEL), F32), jax.ShapeDtypeStruct((n, D_MODEL), BF16),
                   jax.ShapeDtypeStruct((n, N_EXPERTS), F32)],
        compiler_params=_cparams("parallel"),
        name="out_proj_ln_router",
    )(o_attn, o_rwkv, x, wo[:D_ATTN], wo[D_ATTN:], lw['ln1_g'].reshape(1, -1), lw['ln1_b'].reshape(1, -1),
      lw['router_w'].astype(F32))


def _expert_kernel(be_ref, na_ref, x_ref, wg_ref, wu_ref, wd_ref, y_ref):
    g = pl.program_id(0)

    @pl.when(g < na_ref[0])
    def _():
        x = x_ref[...]
        hg = jnp.dot(x, wg_ref[0], preferred_element_type=F32)
        hu = jnp.dot(x, wu_ref[0], preferred_element_type=F32)
        h = hg * _sigmoid(hg) * hu
        y_ref[...] = jnp.dot(h.astype(BF16), wd_ref[0], preferred_element_type=F32)

    @pl.when(g >= na_ref[0])
    def _():
        y_ref[...] = jnp.zeros_like(y_ref)


def _expert_blocks(xs, blk_e, n_active, wg, wu, wd, tm):
    rows = xs.shape[0]
    n_blocks = rows // tm
    d_e = wg.shape[-1]
    return pl.pallas_call(
        _expert_kernel,
        grid_spec=pltpu.PrefetchScalarGridSpec(
            num_scalar_prefetch=2,
            grid=(n_blocks,),
            in_specs=[pl.BlockSpec((tm, D_MODEL), lambda g, be, na: (g, 0)),
                      pl.BlockSpec((1, D_MODEL, d_e), lambda g, be, na: (be[g], 0, 0)),
                      pl.BlockSpec((1, D_MODEL, d_e), lambda g, be, na: (be[g], 0, 0)),
                      pl.BlockSpec((1, d_e, D_MODEL), lambda g, be, na: (be[g], 0, 0))],
            out_specs=pl.BlockSpec((tm, D_MODEL), lambda g, be, na: (g, 0)),
        ),
        out_shape=jax.ShapeDtypeStruct((rows, D_MODEL), F32),
        compiler_params=_cparams("arbitrary"),
        name="expert_blocks",
    )(blk_e, n_active, xs, wg, wu, wd)


def _route(scores, router_bias):
    n = scores.shape[0]
    choice = scores + router_bias.astype(F32)
    grp = lax.top_k(choice.reshape(n, N_GROUPS, N_EXPERTS // N_GROUPS), 2)[0].sum(-1)
    _, top_g = lax.top_k(grp, TOPK_GROUPS)
    g_mask = (top_g[:, :, None] == jnp.arange(N_GROUPS)).any(axis=1)
    choice = jnp.where(jnp.repeat(g_mask, N_EXPERTS // N_GROUPS, axis=1), choice, NEG_INF)
    _, top_e = lax.top_k(choice, TOP_K)
    gate = jnp.take_along_axis(scores, top_e, axis=1)
    gate = gate / jnp.sum(gate, -1, keepdims=True) * ROUTED_SCALE
    onehot = (top_e[:, :, None] == jnp.arange(N_EXPERTS)).any(axis=1).astype(jnp.int32)
    csum = jnp.cumsum(onehot, axis=0)
    counts = csum[-1]
    rank = jnp.take_along_axis(csum - onehot, top_e, axis=1)
    pcounts = (counts + MOE_TM - 1) // MOE_TM * MOE_TM
    pends = jnp.cumsum(pcounts)
    dest = (pends - pcounts)[top_e] + rank
    n_blocks = -(-(n * TOP_K) // MOE_TM) + N_EXPERTS
    slot_tok = jnp.zeros((n_blocks * MOE_TM,), jnp.int32).at[dest.reshape(-1)].set(
        jnp.repeat(jnp.arange(n, dtype=jnp.int32), TOP_K))
    blk_e = jnp.minimum(jnp.searchsorted(pends, jnp.arange(n_blocks) * MOE_TM, side='right'),
                        N_EXPERTS - 1).astype(jnp.int32)
    n_active = (pends[-1] // MOE_TM).astype(jnp.int32).reshape(1)
    return gate, dest.astype(jnp.int32), slot_tok, blk_e, n_active


def _combine_kernel(x1_ref, sh_ref, ys_ref, gate_ref, g_ref, b_ref, o_ref):
    gate = gate_ref[...]
    f = sh_ref[...]
    for j in range(TOP_K):
        f = f + ys_ref[:, j, :] * gate[:, j:j + 1]
    o_ref[...] = _layer_norm(ALPHA * x1_ref[...] + f, g_ref[...], b_ref[...])


def _combine(x1, shared, ysg, gate, lw, tm):
    n = x1.shape[0]
    row = pl.BlockSpec((tm, D_MODEL), lambda i: (i, 0))
    par = pl.BlockSpec((1, D_MODEL), lambda i: (0, 0))
    return pl.pallas_call(
        _combine_kernel,
        grid=(n // tm,),
        in_specs=[row, row, pl.BlockSpec((tm, TOP_K, D_MODEL), lambda i: (i, 0, 0)),
                  pl.BlockSpec((tm, TOP_K), lambda i: (i, 0)), par, par],
        out_specs=row,
        out_shape=jax.ShapeDtypeStruct((n, D_MODEL), F32),
        compiler_params=_cparams("parallel"),
        name="moe_combine_ln",
    )(x1, shared, ysg, gate, lw['ln2_g'].reshape(1, -1), lw['ln2_b'].reshape(1, -1))


def _moe(x1, x1b, scores, lw, ew):
    n = x1.shape[0]
    gate, dest, slot_tok, blk_e, n_active = _route(scores, lw['router_bias'])
    xs = x1b[slot_tok]
    ys = _expert_blocks(xs, blk_e, n_active, ew['e_gate'], ew['e_up'], ew['e_down'], MOE_TM)
    tm_s = min(MOE_TM, n)
    shared = _expert_blocks(x1b, jnp.zeros((n // tm_s,), jnp.int32), jnp.full((1,), n // tm_s, jnp.int32),
                            ew['s_gate'], ew['s_up'], ew['s_down'], tm_s)
    ysg = ys[dest]
    return _combine(x1, shared, ysg, gate, lw, min(64, n))


def _layer(x, attend, shift0, s0, lw, ew, w_in_b, n_batch, seq):
    n = n_batch * seq
    x2 = x.reshape(n, D_MODEL)
    xb = x2.astype(BF16)
    tm = min(512, n)
    zqkv = _matmul(xb, w_in_b[:, :3 * D_ATTN], tm, 768)
    zr = _matmul(xb, w_in_b[:, 3 * D_ATTN:], tm, C_SHIFT // 2)
    o_attn = attend(zqkv)
    o_rwkv, s_new, shift_new = _rwkv(zr, shift0, s0, lw, n_batch, seq)
    x1, x1b, scores = _out_proj(o_attn, o_rwkv, x2, lw, min(256, n))
    y = _moe(x1, x1b, scores, lw, ew)
    k = zqkv[:, D_ATTN:2 * D_ATTN].reshape(n_batch, seq, H_ATTN, HEAD_DIM)
    v = zqkv[:, 2 * D_ATTN:].reshape(n_batch, seq, H_ATTN, HEAD_DIM)
    return y.reshape(n_batch, seq, D_MODEL), k, v, s_new, shift_new


def kernel(x_prompt, x_sample, cache_k, cache_v, state_wkv, state_shift, page_table, rel_bias, w_in, mu_shift, w0, w2,
           a0, a2, g2, k_k, k_a, r_k, lnx_g, lnx_b, w_out, ln1_g, ln1_b, router_w, router_bias, e_gate, e_up, e_down,
           s_gate, s_up, s_down, ln2_g, ln2_b):
    assert w_in.shape[0] == DEPTH == 1
    n_b, seq, _ = x_prompt.shape
    db, dec_seq, _ = x_sample.shape
    past_len = page_table.shape[1] * PAGE_SIZE
    lw = {'mu_shift': mu_shift[0], 'w0': w0[0], 'w2': w2[0], 'a0': a0[0], 'a2': a2[0], 'g2': g2[0], 'k_k': k_k[0],
          'k_a': k_a[0], 'r_k': r_k[0], 'lnx_g': lnx_g[0], 'lnx_b': lnx_b[0], 'w_out': w_out[0], 'ln1_g': ln1_g[0],
          'ln1_b': ln1_b[0], 'router_w': router_w[0], 'router_bias': router_bias[0], 'ln2_g': ln2_g[0],
          'ln2_b': ln2_b[0]}
    ew = {'e_gate': e_gate[0].astype(BF16), 'e_up': e_up[0].astype(BF16), 'e_down': e_down[0].astype(BF16),
          's_gate': s_gate.astype(BF16), 's_up': s_up.astype(BF16), 's_down': s_down.astype(BF16)}
    w_in_b = w_in[0].astype(BF16)

    y_p, k_p, v_p, w_p, s_p = _layer(
        x_prompt, lambda z: _moba_prompt(z, rel_bias, n_b, seq),
        jnp.zeros((n_b, C_SHIFT), F32), jnp.zeros((n_b, H_RWKV, HEAD_DIM, HEAD_DIM), F32), lw, ew, w_in_b, n_b, seq)
    y_s, k_s, v_s, w_s, s_s = _layer(
        x_sample, lambda z: _moba_sample(z, cache_k, cache_v, page_table, rel_bias, db, dec_seq, past_len),
        state_shift[0], state_wkv[0], lw, ew, w_in_b, db, dec_seq)
    return (y_p, y_s, k_p[None], v_p[None], w_p[None], s_p[None], k_s[None], v_s[None], w_s[None], s_s[None])
```

```python
import functools
import math

import numpy as np
import jax
import jax.numpy as jnp
from jax import lax
from jax.experimental import pallas as pl
from jax.experimental.pallas import tpu as pltpu

F32 = jnp.float32
BF16 = jnp.bfloat16

D_MODEL = 2048
HEAD_DIM = 64
D_ATTN = 1024
D_RWKV = 1024
H_ATTN = 16
H_RWKV = 16
N_PAIRS = 8
LANES = 128
BLOCK = 256
TOP_BLOCKS = 3
PAGE_SIZE = 128
N_BUCKETS = 32
MAX_DISTANCE = 128
DECAY_LORA = 64
AAA_LORA = 64
GATE_LORA = 128
C_SHIFT = 3 * D_RWKV + DECAY_LORA + AAA_LORA + GATE_LORA
LNX_EPS = 64e-5
LN_EPS = 1e-5
N_EXPERTS = 64
TOP_K = 8
N_GROUPS = 8
TOPK_GROUPS = 4
D_EXPERT = 512
ROUTED_SCALE = 2.5
DEPTH = 1
ALPHA = (2.0 * DEPTH) ** 0.25
CHUNK = 64
MOE_TM = 256
SHORT_CHUNK = 16
FAR_BLOCKS_PER_STEP = 4
VMEM_LIMIT = 56 * 1024 * 1024
NEG_INF = float("-inf")
LOG2E = math.log2(math.e)


def _cparams(*sem):
    return pltpu.CompilerParams(dimension_semantics=sem, vmem_limit_bytes=VMEM_LIMIT)


def _dot(a, b):
    return jnp.dot(a.astype(BF16), b.astype(BF16), preferred_element_type=F32)


def _dot_nt(a, b):
    return lax.dot_general(a.astype(BF16), b.astype(BF16), (((1,), (1,)), ((), ())), preferred_element_type=F32)


def _dot_tn(a, b):
    return lax.dot_general(a.astype(BF16), b.astype(BF16), (((0,), (0,)), ((), ())), preferred_element_type=F32)


def _dot_hi(a, b):
    return jnp.dot(a, b, precision=lax.Precision.HIGHEST, preferred_element_type=F32)


def _sigmoid(x):
    return 1.0 / (1.0 + jnp.exp(-x))


def _mm_kernel(x_ref, w_ref, o_ref):
    o_ref[...] = jnp.dot(x_ref[...], w_ref[...], preferred_element_type=F32)


def _matmul(x, w, tm, tn):
    m, k = x.shape
    n = w.shape[1]
    return pl.pallas_call(
        _mm_kernel,
        grid=(m // tm, n // tn),
        in_specs=[pl.BlockSpec((tm, k), lambda i, j: (i, 0)),
                  pl.BlockSpec((k, tn), lambda i, j: (0, j))],
        out_specs=pl.BlockSpec((tm, tn), lambda i, j: (i, j)),
        out_shape=jax.ShapeDtypeStruct((m, n), F32),
        compiler_params=_cparams("parallel", "parallel"),
        name="in_proj",
    )(x, w)


def _bucket_table(max_dist):
    n = np.arange(max_dist + 1)
    max_exact = N_BUCKETS // 2
    nf = np.maximum(n, 1).astype(np.float64)
    large = max_exact + (np.log(nf / max_exact) / math.log(MAX_DISTANCE / max_exact)
                         * (N_BUCKETS - max_exact)).astype(np.int64)
    large = np.minimum(large, N_BUCKETS - 1)
    return np.where(n < max_exact, n, large).astype(np.int32)


def _prompt_bias_tiles(rel_bias):
    kpos = np.arange(BLOCK)[:, None]
    qpos = np.arange(BLOCK)[None, :]
    bt = _bucket_table(2 * BLOCK)
    d_own = qpos - kpos
    b_own = bt[np.maximum(d_own, 0)]
    b_prev = bt[BLOCK + qpos - kpos]
    rb = rel_bias.astype(F32)
    rb = (rb - rb[N_BUCKETS - 1:]) * LOG2E
    own = jnp.where(jnp.asarray(d_own >= 0)[..., None], rb[b_own], NEG_INF)
    prev = rb[b_prev]
    t = jnp.stack([own, prev], axis=0)
    t = t.transpose(3, 0, 1, 2).reshape(N_PAIRS, 2, 2, BLOCK, BLOCK)
    return t


def _moba_prompt_kernel(q_ref, k_ref, v_ref, bias_ref, o_ref, kb_s, vt_s, km_s, sel_s, *, n_blk):
    i = pl.program_id(2)

    @pl.when(i == 0)
    def _():
        def fill(j, c):
            r0 = pl.multiple_of(j * BLOCK, BLOCK)
            kblk = k_ref[pl.ds(r0, BLOCK), :]
            vblk = v_ref[pl.ds(r0, BLOCK), :]
            kb_s[pl.ds(r0, BLOCK), :] = kblk.astype(BF16)
            vt_s[j] = vblk.T.astype(BF16)
            km_s[pl.ds(j, 1), :] = jnp.sum(kblk, axis=0, keepdims=True) * (1.0 / BLOCK)
            return c
        lax.fori_loop(0, n_blk, fill, 0)

    qt = q_ref[...].T
    row = lax.broadcasted_iota(jnp.int32, (LANES, BLOCK), 0)
    head0 = row < HEAD_DIM
    qts = [jnp.where(head0, qt, 0.0), jnp.where(head0, 0.0, qt)]
    qtb = [(x * (HEAD_DIM ** -0.5 * LOG2E)).astype(BF16) for x in qts]

    blk = lax.broadcasted_iota(jnp.int32, (n_blk, BLOCK), 0)
    valid = blk < i
    km = km_s[...]
    for h in range(2):
        g = jnp.where(valid, _dot_hi(km, qts[h]), NEG_INF)
        sel = jnp.zeros((n_blk, BLOCK), F32)
        for _ in range(TOP_BLOCKS):
            m = jnp.max(g, axis=0, keepdims=True)
            first = jnp.min(jnp.where(g == m, blk, n_blk), axis=0, keepdims=True)
            hit = blk == first
            sel = jnp.where(hit & valid, 1.0, sel)
            g = jnp.where(hit, NEG_INF, g)
        sel_s[h] = sel

    def scores(j, h):
        r0 = pl.multiple_of(j * BLOCK, BLOCK)
        return jnp.dot(kb_s[pl.ds(r0, BLOCK), :], qtb[h], preferred_element_type=F32)

    def pv(j, pb0, pb1):
        vt = vt_s[j]
        return jnp.concatenate(
            [jnp.dot(vt[:HEAD_DIM], pb0, preferred_element_type=F32),
             jnp.dot(vt[HEAD_DIM:], pb1, preferred_element_type=F32)], axis=0)

    def update(blocks, carry):
        m0, m1, l0, l1, acc = carry
        new = []
        for h, (m, l) in enumerate(((m0, l0), (m1, l1))):
            mn = m
            for blk_ in blocks:
                mn = jnp.where(blk_[3 + h], jnp.maximum(mn, jnp.max(blk_[1 + h], axis=0, keepdims=True)), mn)
            a = jnp.exp2(m - mn)
            ln = a * l
            pbs = []
            for blk_ in blocks:
                pr = jnp.exp2(blk_[1 + h] - jnp.where(blk_[3 + h], mn, jnp.inf))
                ln = ln + jnp.sum(pr, axis=0, keepdims=True)
                pbs.append(pr.astype(BF16))
            new.append((mn, a, ln, pbs))
        acc = jnp.where(head0, new[0][1], new[1][1]) * acc
        for n, blk_ in enumerate(blocks):
            acc = acc + pv(blk_[0], new[0][3][n], new[1][3][n])
        return new[0][0], new[1][0], new[0][2], new[1][2], acc

    s0 = scores(i, 0) + bias_ref[0, 0, 0]
    s1 = scores(i, 1) + bias_ref[0, 1, 0]
    m0 = jnp.max(s0, axis=0, keepdims=True)
    m1 = jnp.max(s1, axis=0, keepdims=True)
    p0 = jnp.exp2(s0 - m0)
    p1 = jnp.exp2(s1 - m1)
    carry = (m0, m1, jnp.sum(p0, axis=0, keepdims=True), jnp.sum(p1, axis=0, keepdims=True),
             pv(i, p0.astype(BF16), p1.astype(BF16)))

    jp = jnp.maximum(i - 1, 0)
    has_prev = i > 0
    keep0 = (sel_s[0, pl.ds(jp, 1), :] > 0.0) & has_prev
    keep1 = (sel_s[1, pl.ds(jp, 1), :] > 0.0) & has_prev
    carry = update([(jp, scores(jp, 0) + bias_ref[0, 0, 1], scores(jp, 1) + bias_ref[0, 1, 1], keep0, keep1)], carry)

    n_far = jnp.maximum(i - 1, 0)

    def far_body(t, carry):
        blocks = []
        for u in range(FAR_BLOCKS_PER_STEP):
            j = t * FAR_BLOCKS_PER_STEP + u
            in_range = j < n_far
            jc = jnp.minimum(j, n_blk - 1)
            blocks.append((jc, scores(jc, 0), scores(jc, 1),
                           (sel_s[0, pl.ds(jc, 1), :] > 0.0) & in_range,
                           (sel_s[1, pl.ds(jc, 1), :] > 0.0) & in_range))
        return update(blocks, carry)

    n_steps = (n_far + FAR_BLOCKS_PER_STEP - 1) // FAR_BLOCKS_PER_STEP
    m0, m1, l0, l1, acc = lax.fori_loop(0, n_steps, far_body, carry)
    ot = acc / jnp.where(head0, l0, l1)
    o_ref[...] = ot.T


def _moba_prompt(zqkv, rel_bias, n_batch, seq):
    n_blk = seq // BLOCK
    bias = _prompt_bias_tiles(rel_bias)
    kern = functools.partial(_moba_prompt_kernel, n_blk=n_blk)
    return pl.pallas_call(
        kern,
        grid=(n_batch, N_PAIRS, n_blk),
        in_specs=[
            pl.BlockSpec((BLOCK, LANES), lambda b, p, i: (b * n_blk + i, p)),
            pl.BlockSpec((seq, LANES), lambda b, p, i: (b, N_PAIRS + p)),
            pl.BlockSpec((seq, LANES), lambda b, p, i: (b, 2 * N_PAIRS + p)),
            pl.BlockSpec((1, 2, 2, BLOCK, BLOCK), lambda b, p, i: (p, 0, 0, 0, 0)),
        ],
        out_specs=pl.BlockSpec((BLOCK, LANES), lambda b, p, i: (b * n_blk + i, p)),
        out_shape=jax.ShapeDtypeStruct((n_batch * seq, D_ATTN), F32),
        scratch_shapes=[pltpu.VMEM((seq, LANES), BF16),
                        pltpu.VMEM((n_blk, LANES, BLOCK), BF16),
                        pltpu.VMEM((n_blk, LANES), F32),
                        pltpu.VMEM((2, n_blk, BLOCK), F32)],
        compiler_params=_cparams("parallel", "parallel", "arbitrary"),
        name="moba_prompt",
    )(zqkv, zqkv, zqkv, bias)


PAGES_PER_STEP = 8


def _sample_gate_kernel(pt_ref, q_ref, *refs, n_past_blk, n_steps):
    page_refs = refs[:PAGES_PER_STEP]
    idx_ref = refs[PAGES_PER_STEP]
    acc_s = refs[PAGES_PER_STEP + 1]
    s = pl.program_id(1)
    lane = lax.broadcasted_iota(jnp.int32, (H_ATTN * HEAD_DIM, LANES), 1)

    @pl.when(s == 0)
    def _():
        acc_s[...] = jnp.zeros_like(acc_s)

    acc = acc_s[...]
    ppb = BLOCK // PAGE_SIZE
    for bl in range(PAGES_PER_STEP // ppb):
        x = page_refs[ppb * bl][0]
        for e in range(1, ppb):
            x = x + page_refs[ppb * bl + e][0]
        cs = jnp.sum(x, axis=-1, keepdims=True)
        acc = acc + jnp.where(lane == s * (PAGES_PER_STEP // ppb) + bl, cs, 0.0)
    acc_s[...] = acc

    @pl.when(s == n_steps - 1)
    def _():
        lane8 = lax.broadcasted_iota(jnp.int32, (8, LANES), 1)
        valid = lane8 < n_past_blk
        for h in range(H_ATTN):
            kmt = acc_s[h * HEAD_DIM:(h + 1) * HEAD_DIM, :] * (1.0 / BLOCK)
            qh = q_ref[0, :, h * HEAD_DIM:(h + 1) * HEAD_DIM]
            g = jnp.where(valid, _dot_hi(qh, kmt), NEG_INF)
            out = jnp.zeros((8, LANES), jnp.int32)
            for t in range(TOP_BLOCKS):
                m = jnp.max(g, axis=-1, keepdims=True)
                first = jnp.min(jnp.where(g == m, lane8, LANES), axis=-1, keepdims=True)
                out = jnp.where(lane8 == t, first, out)
                g = jnp.where(lane8 == first, NEG_INF, g)
            idx_ref[0, h] = out


def _sample_gate(page_table, q8, cache_k_pages, n_past_blk):
    db = q8.shape[0]
    n_pages = page_table.shape[1]
    n_steps = n_pages // PAGES_PER_STEP
    kern = functools.partial(_sample_gate_kernel, n_past_blk=n_past_blk, n_steps=n_steps)

    def page_spec(r):
        return pl.BlockSpec((1, H_ATTN * HEAD_DIM, PAGE_SIZE),
                            lambda b, s, pt: (pt[b, s * PAGES_PER_STEP + r], 0, 0))

    return pl.pallas_call(
        kern,
        grid_spec=pltpu.PrefetchScalarGridSpec(
            num_scalar_prefetch=1,
            grid=(db, n_steps),
            in_specs=[pl.BlockSpec((1, 8, H_ATTN * HEAD_DIM), lambda b, s, pt: (b, 0, 0))]
            + [page_spec(r) for r in range(PAGES_PER_STEP)],
            out_specs=pl.BlockSpec((1, H_ATTN, 8, LANES), lambda b, s, pt: (b, 0, 0, 0)),
            scratch_shapes=[pltpu.VMEM((H_ATTN * HEAD_DIM, LANES), F32)],
        ),
        out_shape=jax.ShapeDtypeStruct((db, H_ATTN, 8, LANES), jnp.int32),
        compiler_params=_cparams("parallel", "arbitrary"),
        name="sample_gate",
    )(page_table, q8, *([cache_k_pages] * PAGES_PER_STEP))


def _sample_attn_kernel(slab_ref, idx_ref, qt_ref, kn_ref, vn_ref, bown_ref, blast_ref, far_ref, *refs,
                        dec_seq, n_past_blk):
    n_slab = dec_seq * TOP_BLOCKS * (BLOCK // PAGE_SIZE)
    k_refs = refs[:n_slab]
    v_refs = refs[n_slab:2 * n_slab]
    o_ref = refs[2 * n_slab]
    b = pl.program_id(0)
    h = pl.program_id(1)
    ppb = BLOCK // PAGE_SIZE
    lane8 = lax.broadcasted_iota(jnp.int32, (HEAD_DIM, 8), 1)
    out = jnp.zeros((HEAD_DIM, 8), F32)
    kn = kn_ref[0, 0]
    vn = vn_ref[0, 0]
    far = far_ref[h]
    for qi in range(dec_seq):
        qc = qt_ref[0, 0][:, qi:qi + 1] * (HEAD_DIM ** -0.5)
        rows = []
        for t in range(TOP_BLOCKS):
            sel_blk = idx_ref[((b * H_ATTN + h) * dec_seq + qi) * TOP_BLOCKS + t]
            is_last = sel_blk == n_past_blk - 1
            for e in range(ppb):
                slab = k_refs[(qi * TOP_BLOCKS + t) * ppb + e][0]
                srow = jnp.sum(slab * qc, axis=0, keepdims=True)
                rows.append(srow + jnp.where(is_last, blast_ref[0, qi, e:e + 1, :], far))
        s_own = jnp.sum(kn * qc, axis=0, keepdims=True) + bown_ref[0, qi:qi + 1, :]
        m = jnp.max(s_own, axis=-1, keepdims=True)
        for r in rows:
            m = jnp.maximum(m, jnp.max(r, axis=-1, keepdims=True))
        p_own = jnp.exp(s_own - m)
        l = jnp.sum(p_own, axis=-1, keepdims=True)
        acc = jnp.zeros((HEAD_DIM, PAGE_SIZE), F32)
        for n, r in enumerate(rows):
            pr = jnp.exp(r - m)
            l = l + jnp.sum(pr, axis=-1, keepdims=True)
            acc = acc + v_refs[qi * TOP_BLOCKS * ppb + n][0] * pr
        o = jnp.sum(acc, axis=-1, keepdims=True) + jnp.sum(vn * p_own, axis=-1, keepdims=True)
        out = jnp.where(lane8 == qi, o / l, out)
    o_ref[0, 0] = out


def _sample_attn(slabs, idx_flat, qt, knt, vnt, bias_own, bias_last, far, ck_slabs, cv_slabs, dec_seq, n_past_blk):
    db = qt.shape[0]
    ppb = BLOCK // PAGE_SIZE
    n_slab = dec_seq * TOP_BLOCKS * ppb
    kern = functools.partial(_sample_attn_kernel, dec_seq=dec_seq, n_past_blk=n_past_blk)

    def slab_spec(n):
        return pl.BlockSpec((1, HEAD_DIM, PAGE_SIZE),
                            lambda b, h, sl, ix: (sl[(b * H_ATTN + h) * n_slab + n], 0, 0))

    small = lambda shape: pl.BlockSpec(shape, lambda b, h, sl, ix: (b, h, 0, 0))
    return pl.pallas_call(
        kern,
        grid_spec=pltpu.PrefetchScalarGridSpec(
            num_scalar_prefetch=2,
            grid=(db, H_ATTN),
            in_specs=[small((1, 1, HEAD_DIM, 8)), small((1, 1, HEAD_DIM, 8)), small((1, 1, HEAD_DIM, 8)),
                      pl.BlockSpec((1, dec_seq, 8), lambda b, h, sl, ix: (h, 0, 0)),
                      pl.BlockSpec((1, dec_seq, ppb, PAGE_SIZE), lambda b, h, sl, ix: (h, 0, 0, 0)),
                      pl.BlockSpec(memory_space=pltpu.SMEM)]
            + [slab_spec(n) for n in range(n_slab)] * 2,
            out_specs=small((1, 1, HEAD_DIM, 8)),
        ),
        out_shape=jax.ShapeDtypeStruct((db, H_ATTN, HEAD_DIM, 8), F32),
        compiler_params=_cparams("parallel", "parallel"),
        name="sample_attn",
    )(slabs, idx_flat, qt, knt, vnt, bias_own, bias_last, far, *([ck_slabs] * n_slab), *([cv_slabs] * n_slab))


def _moba_sample(zqkv_s, cache_k, cache_v, page_table, rel_bias, db, dec_seq, past_len):
    n_past_blk = past_len // BLOCK
    assert past_len % BLOCK == 0 and n_past_blk >= TOP_BLOCKS and n_past_blk <= LANES
    ppb = BLOCK // PAGE_SIZE
    pool = cache_k.shape[1]
    ckt = cache_k[0].transpose(0, 2, 3, 1)
    cvt = cache_v[0].transpose(0, 2, 3, 1)
    z3 = zqkv_s.reshape(db, dec_seq, 3 * D_ATTN)
    q = z3[..., :D_ATTN]
    q8 = jnp.pad(q, ((0, 0), (0, 8 - dec_seq), (0, 0)))
    idx = _sample_gate(page_table, q8, ckt.reshape(pool, H_ATTN * HEAD_DIM, PAGE_SIZE), n_past_blk)
    idx = idx[:, :, :dec_seq, :TOP_BLOCKS]
    pages = jnp.take_along_axis(page_table[:, None, None, None, :],
                                (idx[..., None] * ppb + jnp.arange(ppb)).reshape(db, H_ATTN, dec_seq, 1, -1),
                                axis=-1)
    slabs = (pages.reshape(db, H_ATTN, -1) * H_ATTN + jnp.arange(H_ATTN)[None, :, None]).reshape(-1).astype(jnp.int32)

    def heads_t(t):
        t = t.reshape(db, dec_seq, H_ATTN, HEAD_DIM).transpose(0, 2, 3, 1)
        return jnp.pad(t, ((0, 0), (0, 0), (0, 0), (0, 8 - dec_seq)))

    bt = _bucket_table(BLOCK + dec_seq)
    rb = rel_bias.astype(F32)
    qi = np.arange(dec_seq)[:, None]
    t8 = np.arange(8)[None, :]
    own_ok = (t8 <= qi) & (t8 < dec_seq)
    bias_own = jnp.where(jnp.asarray(own_ok)[..., None], rb[bt[np.maximum(qi - t8, 0)]], NEG_INF).transpose(2, 0, 1)
    off = np.arange(BLOCK).reshape(ppb, PAGE_SIZE)[None]
    bias_last = rb[bt[BLOCK + qi[:, :, None] - off]].transpose(3, 0, 1, 2)
    o = _sample_attn(slabs, idx.reshape(-1).astype(jnp.int32), heads_t(q), heads_t(z3[..., D_ATTN:2 * D_ATTN]),
                     heads_t(z3[..., 2 * D_ATTN:]), bias_own, bias_last, rb[N_BUCKETS - 1],
                     ckt.reshape(pool * H_ATTN, HEAD_DIM, PAGE_SIZE), cvt.reshape(pool * H_ATTN, HEAD_DIM, PAGE_SIZE),
                     dec_seq, n_past_blk)
    return o[..., :dec_seq].transpose(0, 3, 1, 2).reshape(db * dec_seq, D_ATTN)


def _pair_ones():
    r = np.arange(LANES)
    return jnp.asarray((r[:, None] // HEAD_DIM) == (r[None, :] // HEAD_DIM), BF16)


def _head_sum(x, ones_pair):
    hi = x.astype(BF16)
    lo = (x - hi.astype(F32)).astype(BF16)
    outs = []
    for g in range(N_PAIRS):
        sl = slice(g * LANES, (g + 1) * LANES)
        outs.append(jnp.dot(hi[:, sl], ones_pair, preferred_element_type=F32)
                    + jnp.dot(lo[:, sl], ones_pair, preferred_element_type=F32))
    return jnp.concatenate(outs, axis=-1)


def _rwkv_prep_kernel(z_ref, prev_ref, mu_ref, w0_ref, w2_ref, a0_ref, a2_ref, g2_ref, kk_ref, ka_ref, ones_ref,
                      r_ref, ld_ref, k_ref, v_ref, kap_ref, b_ref, g_ref):
    z = z_ref[...]
    zs = z + (prev_ref[...] - z) * mu_ref[...]
    o = D_RWKV
    r = zs[:, :o]
    k = zs[:, o:2 * o]
    v = zs[:, 2 * o:3 * o]
    lora = zs[:, 3 * o:3 * o + LANES]
    gl = zs[:, 3 * o + LANES:]
    x = w0_ref[...] + _dot(jnp.tanh(lora), w2_ref[...])
    nx = -x
    w = -(jnp.maximum(nx, 0.0) + jnp.log(1.0 + jnp.exp(-jnp.abs(nx)))) - 0.5
    logd = -jnp.exp(w)
    a = _sigmoid(a0_ref[...] + _dot(lora, a2_ref[...]))
    g = _dot(_sigmoid(gl), g2_ref[...])
    kk = k * kk_ref[...]
    nrm = jnp.sqrt(_head_sum(kk * kk, ones_ref[...]))
    kap = kk / jnp.maximum(nrm, 1e-12)
    k2 = k * (1.0 + (a - 1.0) * ka_ref[...])
    for p in range(N_PAIRS):
        sl = slice(p * LANES, (p + 1) * LANES)
        r_ref[p] = r[:, sl]
        ld_ref[p] = logd[:, sl]
        k_ref[p] = k2[:, sl]
        v_ref[p] = v[:, sl]
        kap_ref[p] = kap[:, sl]
        b_ref[p] = (kap * a)[:, sl]
        g_ref[p] = g[:, sl]


def _rwkv_prep(zr, prev, lw, tm):
    n = zr.shape[0]
    row = lambda a: a.reshape(1, -1).astype(F32)
    zpad = jnp.zeros((DECAY_LORA, D_RWKV), F32)
    w2p = jnp.concatenate([lw['w2'], zpad], axis=0).astype(BF16)
    a2p = jnp.concatenate([zpad, lw['a2']], axis=0).astype(BF16)
    full = lambda shape: pl.BlockSpec(shape, lambda i: (0,) * len(shape))
    out_sd = jax.ShapeDtypeStruct((N_PAIRS, n, LANES), F32)
    out_spec = pl.BlockSpec((N_PAIRS, tm, LANES), lambda i: (0, i, 0))
    return pl.pallas_call(
        _rwkv_prep_kernel,
        grid=(n // tm,),
        in_specs=[pl.BlockSpec((tm, C_SHIFT), lambda i: (i, 0)), pl.BlockSpec((tm, C_SHIFT), lambda i: (i, 0)),
                  full((1, C_SHIFT)), full((1, D_RWKV)), full((LANES, D_RWKV)), full((1, D_RWKV)),
                  full((LANES, D_RWKV)), full((GATE_LORA, D_RWKV)), full((1, D_RWKV)), full((1, D_RWKV)),
                  full((LANES, LANES))],
        out_specs=[out_spec] * 7,
        out_shape=[out_sd] * 7,
        compiler_params=_cparams("parallel"),
        name="rwkv_prep",
    )(zr, prev, row(lw['mu_shift']), row(lw['w0']), w2p, row(lw['a0']), a2p, lw['g2'].astype(BF16),
      row(lw['k_k']), row(lw['k_a']), _pair_ones())


def _rwkv_lockstep_kernel(r_ref, ld_ref, k_ref, v_ref, kap_ref, b_ref, g_ref, s0_ref, lg_ref, lb_ref, rk_ref,
                          y_ref, s_ref, *, chunk):
    c = pl.program_id(1)

    @pl.when(c == 0)
    def _():
        s_ref[...] = s0_ref[...]

    heads = range(H_RWKV)
    ti = lax.broadcasted_iota(jnp.int32, (chunk, chunk), 0)
    si = lax.broadcasted_iota(jnp.int32, (chunk, chunk), 1)
    tril_incl = (si <= ti)
    tril_strict = (si < ti)
    ltri = tril_incl.astype(F32)
    eye = (si == ti).astype(F32)
    hsl = [slice((hh % 2) * HEAD_DIM, (hh % 2 + 1) * HEAD_DIM) for hh in heads]

    ld = [ld_ref[p] for p in range(N_PAIRS)]
    cum = [_dot_hi(ltri, x) for x in ld]
    kt, rt, bi, ki, bh, kh, gall = [], [], [], [], [], [], []
    for p in range(N_PAIRS):
        last = cum[p][chunk - 1:chunk, :]
        gam_inv = jnp.exp(-cum[p])
        gam_tail = jnp.exp(last - cum[p])
        kap2, b2, k2 = kap_ref[p], b_ref[p], k_ref[p]
        kt2 = (kap2 * jnp.exp(cum[p] - ld[p])).astype(BF16)
        rt2 = r_ref[p] * jnp.exp(cum[p])
        bi2 = (b2 * gam_inv).astype(BF16)
        ki2 = (k2 * gam_inv).astype(BF16)
        bh2 = (b2 * gam_tail).astype(BF16)
        kh2 = (k2 * gam_tail).astype(BF16)
        g2 = jnp.exp(last)
        for h in range(2):
            sl = hsl[h]
            kt.append(kt2[:, sl]); rt.append(rt2[:, sl]); bi.append(bi2[:, sl]); ki.append(ki2[:, sl])
            bh.append(bh2[:, sl]); kh.append(kh2[:, sl]); gall.append(g2[:, sl])
    vv = [v_ref[hh // 2][:, hsl[hh]].astype(BF16) for hh in heads]
    kr = [jnp.concatenate([kt[hh], rt[hh].astype(BF16)], axis=0) for hh in heads]
    xb = [_dot_nt(kr[hh], bi[hh]) for hh in heads]
    xk = [_dot_nt(kr[hh], ki[hh]) for hh in heads]
    a_ab = [jnp.where(tril_strict, xb[hh][:chunk], 0.0) for hh in heads]
    a_rb = [jnp.where(tril_incl, xb[hh][chunk:], 0.0).astype(BF16) for hh in heads]
    a_ak = [jnp.where(tril_strict, xk[hh][:chunk], 0.0).astype(BF16) for hh in heads]
    a_rk = [jnp.where(tril_incl, xk[hh][chunk:], 0.0).astype(BF16) for hh in heads]
    akv = [_dot(a_ak[hh], vv[hh]) for hh in heads]
    ab = [x.astype(BF16) for x in a_ab]
    tm = [eye - a_ab[hh] for hh in heads]
    pw = [_dot(ab[hh], ab[hh]) for hh in heads]
    n_sq = int(math.log2(chunk)) - 1
    for it in range(n_sq):
        pwb = [x.astype(BF16) for x in pw]
        tm = [tm[hh] + _dot(tm[hh], pwb[hh]) for hh in heads]
        if it < n_sq - 1:
            pw = [_dot(pwb[hh], pwb[hh]) for hh in heads]
    tmb = [x.astype(BF16) for x in tm]
    w = [-_dot(tmb[hh], kt[hh]) for hh in heads]
    u0 = [-_dot(tmb[hh], akv[hh]) for hh in heads]
    wb = [x.astype(BF16) for x in w]
    u0b = [x.astype(BF16) for x in u0]
    pm = [rt[hh] + _dot(a_rb[hh], wb[hh]) for hh in heads]
    y0 = [_dot(a_rb[hh], u0b[hh]) + _dot(a_rk[hh], vv[hh]) for hh in heads]
    s_old = [s_ref[0, hh] for hh in heads]
    sb = [x.astype(BF16) for x in s_old]
    y = [_dot_nt(pm[hh], sb[hh]) + y0[hh] for hh in heads]
    mt = [_dot_tn(bh[hh], wb[hh]) for hh in heads]
    nt = [_dot_tn(jnp.concatenate([u0b[hh], vv[hh]], axis=0), jnp.concatenate([bh[hh], kh[hh]], axis=0))
          for hh in heads]
    for hh in heads:
        s_ref[0, hh] = s_old[hh] * gall[hh] + _dot_nt(sb[hh], mt[hh]) + nt[hh]
    for p in range(N_PAIRS):
        outs = []
        for h in range(2):
            hh = 2 * p + h
            sl = hsl[h]
            mu = jnp.mean(y[hh], axis=-1, keepdims=True)
            var = jnp.mean(jnp.square(y[hh] - mu), axis=-1, keepdims=True)
            yn = (y[hh] - mu) * lax.rsqrt(var + LNX_EPS)
            rk = r_ref[p][:, sl] * k_ref[p][:, sl] * rk_ref[p][:, sl]
            bonus = jnp.sum(rk, axis=-1, keepdims=True) * v_ref[p][:, sl]
            outs.append(yn * lg_ref[p][:, sl] + lb_ref[p][:, sl] + bonus)
        y_ref[p] = jnp.concatenate(outs, axis=-1) * g_ref[p]


def _rwkv_chunks(prep, s0, lw, n_batch, seq, chunk):
    n_chunks = seq // chunk
    pairs = lambda a: a.reshape(N_PAIRS, 1, LANES).astype(F32)
    tile = pl.BlockSpec((N_PAIRS, chunk, LANES), lambda b, c: (0, b * n_chunks + c, 0))
    state = pl.BlockSpec((1, H_RWKV, HEAD_DIM, HEAD_DIM), lambda b, c: (b, 0, 0, 0))
    par = pl.BlockSpec((N_PAIRS, 1, LANES), lambda b, c: (0, 0, 0))
    return pl.pallas_call(
        functools.partial(_rwkv_lockstep_kernel, chunk=chunk),
        grid=(n_batch, n_chunks),
        in_specs=[tile] * 7 + [state, par, par, par],
        out_specs=[tile, state],
        out_shape=[jax.ShapeDtypeStruct((N_PAIRS, n_batch * seq, LANES), F32),
                   jax.ShapeDtypeStruct((n_batch, H_RWKV, HEAD_DIM, HEAD_DIM), F32)],
        compiler_params=_cparams("parallel", "arbitrary"),
        name="rwkv_chunks",
    )(*prep, s0.astype(F32), pairs(lw['lnx_g']), pairs(lw['lnx_b']), pairs(lw['r_k']))


def _rwkv(zr, shift0, s0, lw, n_batch, seq):
    z3 = zr.reshape(n_batch, seq, C_SHIFT)
    prev = jnp.concatenate([shift0[:, None].astype(F32), z3[:, :-1]], axis=1).reshape(n_batch * seq, C_SHIFT)
    prep = _rwkv_prep(zr, prev, lw, min(256, n_batch * seq))
    chunk = CHUNK if seq >= CHUNK else SHORT_CHUNK
    pad = (-seq) % chunk
    if pad:
        prep = [jnp.pad(a.reshape(N_PAIRS, n_batch, seq, LANES), ((0, 0), (0, 0), (0, pad), (0, 0)))
                .reshape(N_PAIRS, n_batch * (seq + pad), LANES) for a in prep]
    y, s_fin = _rwkv_chunks(prep, s0, lw, n_batch, seq + pad, chunk)
    if pad:
        y = y.reshape(N_PAIRS, n_batch, seq + pad, LANES)[:, :, :seq].reshape(N_PAIRS, n_batch * seq, LANES)
    return y, s_fin, z3[:, -1]


def _layer_norm(h, g, b):
    mu = jnp.mean(h, axis=-1, keepdims=True)
    d = h - mu
    var = jnp.mean(d * d, axis=-1, keepdims=True)
    return d * lax.rsqrt(var + LN_EPS) * g + b


def _out_proj_kernel(oa_ref, or_ref, x_ref, wa_ref, wr_ref, g_ref, b_ref, rw_ref, x1_ref, x1b_ref, sc_ref):
    orw = jnp.concatenate([or_ref[p] for p in range(N_PAIRS)], axis=-1)
    mix = _dot(oa_ref[...], wa_ref[...]) + _dot(orw, wr_ref[...])
    x1 = _layer_norm(ALPHA * x_ref[...] + mix, g_ref[...], b_ref[...])
    x1_ref[...] = x1
    x1b_ref[...] = x1.astype(BF16)
    sc_ref[...] = _sigmoid(lax.dot_general(rw_ref[...], x1, (((1,), (1,)), ((), ())),
                                           precision=lax.Precision.HIGHEST, preferred_element_type=F32))


def _out_proj(o_attn, o_rwkv, x, lw, tm):
    n = x.shape[0]
    wo = lw['w_out'].astype(BF16)
    full = lambda shape: pl.BlockSpec(shape, lambda i: (0,) * len(shape))
    return pl.pallas_call(
        _out_proj_kernel,
        grid=(n // tm,),
        in_specs=[pl.BlockSpec((tm, D_ATTN), lambda i: (i, 0)),
                  pl.BlockSpec((N_PAIRS, tm, LANES), lambda i: (0, i, 0)),
                  pl.BlockSpec((tm, D_MODEL), lambda i: (i, 0)),
                  full((D_ATTN, D_MODEL)), full((D_RWKV, D_MODEL)), full((1, D_MODEL)), full((1, D_MODEL)),
                  full((N_EXPERTS, D_MODEL))],
        out_specs=[pl.BlockSpec((tm, D_MODEL), lambda i: (i, 0)), pl.BlockSpec((tm, D_MODEL), lambda i: (i, 0)),
                   pl.BlockSpec((N_EXPERTS, tm), lambda i: (0, i))],
        out_shape=[jax.ShapeDtypeStruct((n, D_MODEL), F32), jax.ShapeDtypeStruct((n, D_MODEL), BF16),
                   jax.ShapeDtypeStruct((N_EXPERTS, n), F32)],
        compiler_params=_cparams("parallel"),
        name="out_proj_ln_router",
    )(o_attn, o_rwkv, x, wo[:D_ATTN], wo[D_ATTN:], lw['ln1_g'].reshape(1, -1), lw['ln1_b'].reshape(1, -1),
      lw['router_w'].astype(F32).T)


def _expert_kernel(be_ref, na_ref, x_ref, wg_ref, wu_ref, wd_ref, y_ref, wg_s, wu_s, wd_s):
    g = pl.program_id(0)
    active = g < na_ref[0]

    @pl.when(active & ((g == 0) | (be_ref[g] != be_ref[jnp.maximum(g - 1, 0)])))
    def _():
        wg_s[...] = wg_ref[0].astype(BF16)
        wu_s[...] = wu_ref[0].astype(BF16)
        wd_s[...] = wd_ref[0].astype(BF16)

    @pl.when(active)
    def _():
        x = x_ref[...]
        hg = jnp.dot(x, wg_s[...], preferred_element_type=F32)
        hu = jnp.dot(x, wu_s[...], preferred_element_type=F32)
        h = hg * _sigmoid(hg) * hu
        y_ref[...] = jnp.dot(h.astype(BF16), wd_s[...], preferred_element_type=F32)

    @pl.when(g >= na_ref[0])
    def _():
        y_ref[...] = jnp.zeros_like(y_ref)


def _expert_blocks(xs, blk_e, n_active, wg, wu, wd, tm):
    rows = xs.shape[0]
    n_blocks = rows // tm
    d_e = wg.shape[-1]
    return pl.pallas_call(
        _expert_kernel,
        grid_spec=pltpu.PrefetchScalarGridSpec(
            num_scalar_prefetch=2,
            grid=(n_blocks,),
            in_specs=[pl.BlockSpec((tm, D_MODEL), lambda g, be, na: (g, 0)),
                      pl.BlockSpec((1, D_MODEL, d_e), lambda g, be, na: (be[g], 0, 0)),
                      pl.BlockSpec((1, D_MODEL, d_e), lambda g, be, na: (be[g], 0, 0)),
                      pl.BlockSpec((1, d_e, D_MODEL), lambda g, be, na: (be[g], 0, 0))],
            out_specs=pl.BlockSpec((tm, D_MODEL), lambda g, be, na: (g, 0)),
            scratch_shapes=[pltpu.VMEM((D_MODEL, d_e), BF16), pltpu.VMEM((D_MODEL, d_e), BF16),
                            pltpu.VMEM((d_e, D_MODEL), BF16)],
        ),
        out_shape=jax.ShapeDtypeStruct((rows, D_MODEL), F32),
        compiler_params=_cparams("arbitrary"),
        name="expert_blocks",
    )(blk_e, n_active, xs, wg, wu, wd)


def _first_max(x, idx, n):
    m = jnp.max(x, axis=0, keepdims=True)
    return m, jnp.min(jnp.where(x == m, idx, n), axis=0, keepdims=True)


def _route_kernel(sc_ref, bias_ref, tri_ref, e_ref, gate_ref, rank_ref, cnt_ref, carry_s):
    i = pl.program_id(0)
    tr = sc_ref.shape[1]
    per_group = N_EXPERTS // N_GROUPS

    @pl.when(i == 0)
    def _():
        carry_s[...] = jnp.zeros_like(carry_s)

    sc = sc_ref[...]
    choice = sc + bias_ref[...]
    in_g = lax.broadcasted_iota(jnp.int32, (per_group, tr), 0)
    g_iota = lax.broadcasted_iota(jnp.int32, (N_GROUPS, tr), 0)
    grp = jnp.zeros((N_GROUPS, tr), F32)
    for gi in range(N_GROUPS):
        cg = choice[gi * per_group:(gi + 1) * per_group]
        m1, i1 = _first_max(cg, in_g, per_group)
        m2 = jnp.max(jnp.where(in_g == i1, NEG_INF, cg), axis=0, keepdims=True)
        grp = jnp.where(g_iota == gi, m1 + m2, grp)
    keep = jnp.zeros((N_GROUPS, tr), F32)
    for _ in range(TOPK_GROUPS):
        _, f = _first_max(grp, g_iota, N_GROUPS)
        hit = g_iota == f
        keep = jnp.where(hit, 1.0, keep)
        grp = jnp.where(hit, NEG_INF, grp)
    masked = jnp.concatenate(
        [jnp.where(keep[gi:gi + 1] > 0.0, choice[gi * per_group:(gi + 1) * per_group], NEG_INF)
         for gi in range(N_GROUPS)], axis=0)
    e_iota = lax.broadcasted_iota(jnp.int32, (N_EXPERTS, tr), 0)
    hits, es, ss = [], [], []
    onehot = jnp.zeros((N_EXPERTS, tr), F32)
    for _ in range(TOP_K):
        _, f = _first_max(masked, e_iota, N_EXPERTS)
        hit = e_iota == f
        hits.append(hit)
        es.append(f)
        ss.append(jnp.sum(jnp.where(hit, sc, 0.0), axis=0, keepdims=True))
        onehot = jnp.where(hit, 1.0, onehot)
        masked = jnp.where(hit, NEG_INF, masked)
    total = ss[0]
    for s in ss[1:]:
        total = total + s
    e_ref[...] = jnp.concatenate(es, axis=0)
    gate_ref[...] = jnp.concatenate([s / total * ROUTED_SCALE for s in ss], axis=0)
    before = carry_s[...] + jnp.dot(onehot.astype(BF16), tri_ref[...], preferred_element_type=F32)
    rank_ref[...] = jnp.concatenate(
        [jnp.sum(jnp.where(h, before, 0.0), axis=0, keepdims=True) for h in hits], axis=0).astype(jnp.int32)
    carry = carry_s[...] + jnp.sum(onehot, axis=1, keepdims=True)
    carry_s[...] = carry
    cnt_ref[...] = jnp.broadcast_to(carry, cnt_ref.shape).astype(jnp.int32)


def _route(scores_t, router_bias):
    n = scores_t.shape[1]
    tr = min(256, n)
    tri = jnp.asarray(np.arange(tr)[:, None] < np.arange(tr)[None, :], BF16)
    col = pl.BlockSpec((TOP_K, tr), lambda i: (0, i))
    top_e, gate, rank, counts = pl.pallas_call(
        _route_kernel,
        grid=(n // tr,),
        in_specs=[pl.BlockSpec((N_EXPERTS, tr), lambda i: (0, i)),
                  pl.BlockSpec((N_EXPERTS, 1), lambda i: (0, 0)),
                  pl.BlockSpec((tr, tr), lambda i: (0, 0))],
        out_specs=[col, col, col, pl.BlockSpec((N_EXPERTS, LANES), lambda i: (0, 0))],
        out_shape=[jax.ShapeDtypeStruct((TOP_K, n), jnp.int32), jax.ShapeDtypeStruct((TOP_K, n), F32),
                   jax.ShapeDtypeStruct((TOP_K, n), jnp.int32), jax.ShapeDtypeStruct((N_EXPERTS, LANES), jnp.int32)],
        scratch_shapes=[pltpu.VMEM((N_EXPERTS, 1), F32)],
        compiler_params=_cparams("arbitrary"),
        name="route",
    )(scores_t, router_bias.astype(F32).reshape(N_EXPERTS, 1), tri)
    counts = counts[:, 0]
    top_e, gate, rank = top_e.T, gate.T, rank.T
    pcounts = (counts + MOE_TM - 1) // MOE_TM * MOE_TM
    pends = jnp.cumsum(pcounts)
    dest = (pends - pcounts)[top_e] + rank
    n_blocks = -(-(n * TOP_K) // MOE_TM) + N_EXPERTS
    slot_tok = jnp.zeros((n_blocks * MOE_TM,), jnp.int32).at[dest.reshape(-1)].set(
        jnp.repeat(jnp.arange(n, dtype=jnp.int32), TOP_K))
    blk_e = jnp.minimum(jnp.searchsorted(pends, jnp.arange(n_blocks) * MOE_TM, side='right'),
                        N_EXPERTS - 1).astype(jnp.int32)
    n_active = (pends[-1] // MOE_TM).astype(jnp.int32).reshape(1)
    return gate, dest.astype(jnp.int32), slot_tok, blk_e, n_active


def _combine_kernel(x1_ref, sh_ref, ys_ref, gate_ref, g_ref, b_ref, o_ref):
    gate = gate_ref[...]
    f = sh_ref[...]
    for j in range(TOP_K):
        f = f + ys_ref[:, j, :] * gate[:, j:j + 1]
    o_ref[...] = _layer_norm(ALPHA * x1_ref[...] + f, g_ref[...], b_ref[...])


def _combine(x1, shared, ysg, gate, lw, tm):
    n = x1.shape[0]
    row = pl.BlockSpec((tm, D_MODEL), lambda i: (i, 0))
    par = pl.BlockSpec((1, D_MODEL), lambda i: (0, 0))
    return pl.pallas_call(
        _combine_kernel,
        grid=(n // tm,),
        in_specs=[row, row, pl.BlockSpec((tm, TOP_K, D_MODEL), lambda i: (i, 0, 0)),
                  pl.BlockSpec((tm, TOP_K), lambda i: (i, 0)), par, par],
        out_specs=row,
        out_shape=jax.ShapeDtypeStruct((n, D_MODEL), F32),
        compiler_params=_cparams("parallel"),
        name="moe_combine_ln",
    )(x1, shared, ysg, gate, lw['ln2_g'].reshape(1, -1), lw['ln2_b'].reshape(1, -1))


def _moe(x1, x1b, scores_t, lw, ew):
    n = x1.shape[0]
    gate, dest, slot_tok, blk_e, n_active = _route(scores_t, lw['router_bias'])
    xs = x1b[slot_tok]
    ys = _expert_blocks(xs, blk_e, n_active, ew['e_gate'], ew['e_up'], ew['e_down'], MOE_TM)
    tm_s = min(MOE_TM, n)
    shared = _expert_blocks(x1b, jnp.zeros((n // tm_s,), jnp.int32), jnp.full((1,), n // tm_s, jnp.int32),
                            ew['s_gate'], ew['s_up'], ew['s_down'], tm_s)
    ysg = ys[dest]
    return _combine(x1, shared, ysg, gate, lw, min(64, n))


def _layer(x, attend, shift0, s0, lw, ew, w_in_b, n_batch, seq):
    n = n_batch * seq
    x2 = x.reshape(n, D_MODEL)
    xb = x2.astype(BF16)
    tm = min(512, n)
    zqkv = _matmul(xb, w_in_b[:, :3 * D_ATTN], tm, 768)
    zr = _matmul(xb, w_in_b[:, 3 * D_ATTN:], tm, C_SHIFT // 2)
    o_attn = attend(zqkv)
    o_rwkv, s_new, shift_new = _rwkv(zr, shift0, s0, lw, n_batch, seq)
    x1, x1b, scores = _out_proj(o_attn, o_rwkv, x2, lw, min(256, n))
    y = _moe(x1, x1b, scores, lw, ew)
    k = zqkv[:, D_ATTN:2 * D_ATTN].reshape(n_batch, seq, H_ATTN, HEAD_DIM)
    v = zqkv[:, 2 * D_ATTN:].reshape(n_batch, seq, H_ATTN, HEAD_DIM)
    return y.reshape(n_batch, seq, D_MODEL), k, v, s_new, shift_new


def kernel(x_prompt, x_sample, cache_k, cache_v, state_wkv, state_shift, page_table, rel_bias, w_in, mu_shift, w0, w2,
           a0, a2, g2, k_k, k_a, r_k, lnx_g, lnx_b, w_out, ln1_g, ln1_b, router_w, router_bias, e_gate, e_up, e_down,
           s_gate, s_up, s_down, ln2_g, ln2_b):
    assert w_in.shape[0] == DEPTH == 1
    n_b, seq, _ = x_prompt.shape
    db, dec_seq, _ = x_sample.shape
    past_len = page_table.shape[1] * PAGE_SIZE
    lw = {'mu_shift': mu_shift[0], 'w0': w0[0], 'w2': w2[0], 'a0': a0[0], 'a2': a2[0], 'g2': g2[0], 'k_k': k_k[0],
          'k_a': k_a[0], 'r_k': r_k[0], 'lnx_g': lnx_g[0], 'lnx_b': lnx_b[0], 'w_out': w_out[0], 'ln1_g': ln1_g[0],
          'ln1_b': ln1_b[0], 'router_w': router_w[0], 'router_bias': router_bias[0], 'ln2_g': ln2_g[0],
          'ln2_b': ln2_b[0]}
    ew = {'e_gate': e_gate[0], 'e_up': e_up[0], 'e_down': e_down[0],
          's_gate': s_gate, 's_up': s_up, 's_down': s_down}
    w_in_b = w_in[0].astype(BF16)

    y_p, k_p, v_p, w_p, s_p = _layer(
        x_prompt, lambda z: _moba_prompt(z, rel_bias, n_b, seq),
        jnp.zeros((n_b, C_SHIFT), F32), jnp.zeros((n_b, H_RWKV, HEAD_DIM, HEAD_DIM), F32), lw, ew, w_in_b, n_b, seq)
    y_s, k_s, v_s, w_s, s_s = _layer(
        x_sample, lambda z: _moba_sample(z, cache_k, cache_v, page_table, rel_bias, db, dec_seq, past_len),
        state_shift[0], state_wkv[0], lw, ew, w_in_b, db, dec_seq)
    return (y_p, y_s, k_p[None], v_p[None], w_p[None], s_p[None], k_s[None], v_s[None], w_s[None], s_s[None])
```

```python
import functools
import math

import numpy as np
import jax
import jax.numpy as jnp
from jax import lax
from jax.experimental import pallas as pl
from jax.experimental.pallas import tpu as pltpu

F32 = jnp.float32
BF16 = jnp.bfloat16

D_MODEL = 2048
HEAD_DIM = 64
D_ATTN = 1024
D_RWKV = 1024
H_ATTN = 16
H_RWKV = 16
N_PAIRS = 8
LANES = 128
BLOCK = 256
TOP_BLOCKS = 3
PAGE_SIZE = 128
N_BUCKETS = 32
MAX_DISTANCE = 128
DECAY_LORA = 64
AAA_LORA = 64
GATE_LORA = 128
C_SHIFT = 3 * D_RWKV + DECAY_LORA + AAA_LORA + GATE_LORA
LNX_EPS = 64e-5
LN_EPS = 1e-5
N_EXPERTS = 64
TOP_K = 8
N_GROUPS = 8
TOPK_GROUPS = 4
D_EXPERT = 512
ROUTED_SCALE = 2.5
DEPTH = 1
ALPHA = (2.0 * DEPTH) ** 0.25
CHUNK = 64
MOE_TM = 256
SHORT_CHUNK = 16
FAR_BLOCKS_PER_STEP = 2
VMEM_LIMIT = 56 * 1024 * 1024
NEG_INF = float("-inf")
LOG2E = math.log2(math.e)


def _cparams(*sem):
    return pltpu.CompilerParams(dimension_semantics=sem, vmem_limit_bytes=VMEM_LIMIT)


def _dot(a, b):
    return jnp.dot(a.astype(BF16), b.astype(BF16), preferred_element_type=F32)


def _dot_nt(a, b):
    return lax.dot_general(a.astype(BF16), b.astype(BF16), (((1,), (1,)), ((), ())), preferred_element_type=F32)


def _dot_tn(a, b):
    return lax.dot_general(a.astype(BF16), b.astype(BF16), (((0,), (0,)), ((), ())), preferred_element_type=F32)


def _dot_hi(a, b):
    return jnp.dot(a, b, precision=lax.Precision.HIGHEST, preferred_element_type=F32)


def _sigmoid(x):
    return 1.0 / (1.0 + jnp.exp(-x))


def _mm_kernel(x_ref, w_ref, o_ref):
    o_ref[...] = jnp.dot(x_ref[...], w_ref[...], preferred_element_type=F32)


def _matmul(x, w, tm, tn):
    m, k = x.shape
    n = w.shape[1]
    return pl.pallas_call(
        _mm_kernel,
        grid=(m // tm, n // tn),
        in_specs=[pl.BlockSpec((tm, k), lambda i, j: (i, 0)),
                  pl.BlockSpec((k, tn), lambda i, j: (0, j))],
        out_specs=pl.BlockSpec((tm, tn), lambda i, j: (i, j)),
        out_shape=jax.ShapeDtypeStruct((m, n), F32),
        compiler_params=_cparams("parallel", "parallel"),
        name="in_proj",
    )(x, w)


def _bucket_table(max_dist):
    n = np.arange(max_dist + 1)
    max_exact = N_BUCKETS // 2
    nf = np.maximum(n, 1).astype(np.float64)
    large = max_exact + (np.log(nf / max_exact) / math.log(MAX_DISTANCE / max_exact)
                         * (N_BUCKETS - max_exact)).astype(np.int64)
    large = np.minimum(large, N_BUCKETS - 1)
    return np.where(n < max_exact, n, large).astype(np.int32)


def _relative_bias_log2(rel_bias):
    rb = rel_bias.astype(F32)
    return (rb - rb[N_BUCKETS - 1:]) * LOG2E


def _prompt_bucket_tiles():
    kpos = np.arange(BLOCK)[:, None]
    qpos = np.arange(BLOCK)[None, :]
    bt = _bucket_table(2 * BLOCK)
    d_own = qpos - kpos
    own = np.where(d_own >= 0, bt[np.maximum(d_own, 0)], -1)
    return jnp.asarray(np.stack([own, bt[BLOCK + qpos - kpos]]).astype(np.int32))


def _moba_prompt_kernel(rb_ref, q_ref, k_ref, v_ref, bkt_ref, o_ref, kb_s, vt_s, km_s, sel_s, bias_s, sa_s, sb_s,
                        *, n_blk):
    p = pl.program_id(1)
    i = pl.program_id(2)

    @pl.when(i == 0)
    def _():
        for kind in range(2):
            bk = bkt_ref[kind]
            for h in range(2):
                t = jnp.zeros((BLOCK, BLOCK), F32)
                for b in range(N_BUCKETS):
                    t = jnp.where(bk == b, rb_ref[b * H_ATTN + 2 * p + h], t)
                bias_s[h, kind] = jnp.where(bk < 0, NEG_INF, t)

        def fill(j, c):
            r0 = pl.multiple_of(j * BLOCK, BLOCK)
            kblk = k_ref[pl.ds(r0, BLOCK), :]
            vblk = v_ref[pl.ds(r0, BLOCK), :]
            kb_s[pl.ds(r0, BLOCK), :] = kblk.astype(BF16)
            vt_s[j] = vblk.T.astype(BF16)
            km_s[pl.ds(j, 1), :] = jnp.sum(kblk, axis=0, keepdims=True) * (1.0 / BLOCK)
            return c
        lax.fori_loop(0, n_blk, fill, 0)

    qt = q_ref[...].T
    row = lax.broadcasted_iota(jnp.int32, (LANES, BLOCK), 0)
    head0 = row < HEAD_DIM
    qts = [jnp.where(head0, qt, 0.0), jnp.where(head0, 0.0, qt)]
    qtb = [(x * (HEAD_DIM ** -0.5 * LOG2E)).astype(BF16) for x in qts]

    blk = lax.broadcasted_iota(jnp.int32, (n_blk, BLOCK), 0)
    valid = blk < i
    km = km_s[...]
    for h in range(2):
        g = jnp.where(valid, _dot_hi(km, qts[h]), NEG_INF)
        sel = jnp.zeros((n_blk, BLOCK), F32)
        for _ in range(TOP_BLOCKS):
            m = jnp.max(g, axis=0, keepdims=True)
            first = jnp.min(jnp.where(g == m, blk, n_blk), axis=0, keepdims=True)
            hit = blk == first
            sel = jnp.where(hit & valid, 1.0, sel)
            g = jnp.where(hit, NEG_INF, g)
        sel_s[h] = sel

    def scores(j, h):
        r0 = pl.multiple_of(j * BLOCK, BLOCK)
        return jnp.dot(kb_s[pl.ds(r0, BLOCK), :], qtb[h], preferred_element_type=F32)

    def pv(j, pb0, pb1):
        vt = vt_s[j]
        return jnp.concatenate(
            [jnp.dot(vt[:HEAD_DIM], pb0, preferred_element_type=F32),
             jnp.dot(vt[HEAD_DIM:], pb1, preferred_element_type=F32)], axis=0)

    def update(blocks, carry):
        m0, m1, l0, l1, acc = carry
        new = []
        for h, (m, l) in enumerate(((m0, l0), (m1, l1))):
            mn = m
            for blk_ in blocks:
                mn = jnp.where(blk_[3 + h], jnp.maximum(mn, jnp.max(blk_[1 + h], axis=0, keepdims=True)), mn)
            a = jnp.exp2(m - mn)
            ln = a * l
            pbs = []
            for blk_ in blocks:
                pr = jnp.exp2(blk_[1 + h] - jnp.where(blk_[3 + h], mn, jnp.inf))
                ln = ln + jnp.sum(pr, axis=0, keepdims=True)
                pbs.append(pr.astype(BF16))
            new.append((mn, a, ln, pbs))
        acc = jnp.where(head0, new[0][1], new[1][1]) * acc
        for n, blk_ in enumerate(blocks):
            acc = acc + pv(blk_[0], new[0][3][n], new[1][3][n])
        return new[0][0], new[1][0], new[0][2], new[1][2], acc

    jp = jnp.maximum(i - 1, 0)
    has_prev = i > 0
    keep0 = (sel_s[0, pl.ds(jp, 1), :] > 0.0) & has_prev
    keep1 = (sel_s[1, pl.ds(jp, 1), :] > 0.0) & has_prev
    always = jnp.full((1, BLOCK), True)
    row_init = jnp.full((1, BLOCK), NEG_INF, F32)
    carry = (row_init, row_init, jnp.zeros((1, BLOCK), F32), jnp.zeros((1, BLOCK), F32),
             jnp.zeros((LANES, BLOCK), F32))
    n_far = jnp.maximum(i - 1, 0)

    def issue_scores(t, dst):
        for u in range(FAR_BLOCKS_PER_STEP):
            jc = jnp.minimum(t * FAR_BLOCKS_PER_STEP + u, n_blk - 1)
            for h in range(2):
                dst[u, h] = scores(jc, h)

    def consume(t, src, carry):
        blocks = []
        for u in range(FAR_BLOCKS_PER_STEP):
            j = t * FAR_BLOCKS_PER_STEP + u
            in_range = j < n_far
            jc = jnp.minimum(j, n_blk - 1)
            blocks.append((jc, src[u, 0], src[u, 1],
                           (sel_s[0, pl.ds(jc, 1), :] > 0.0) & in_range,
                           (sel_s[1, pl.ds(jc, 1), :] > 0.0) & in_range))
        return update(blocks, carry)

    issue_scores(0, sa_s)
    carry = update([(i, scores(i, 0) + bias_s[0, 0], scores(i, 1) + bias_s[1, 0], always, always),
                    (jp, scores(jp, 0) + bias_s[0, 1], scores(jp, 1) + bias_s[1, 1], keep0, keep1)], carry)

    def far_body(it, carry):
        issue_scores(2 * it + 1, sb_s)
        carry = consume(2 * it, sa_s, carry)
        issue_scores(2 * it + 2, sa_s)
        return consume(2 * it + 1, sb_s, carry)

    n_iter = (n_far + 2 * FAR_BLOCKS_PER_STEP - 1) // (2 * FAR_BLOCKS_PER_STEP)
    m0, m1, l0, l1, acc = lax.fori_loop(0, n_iter, far_body, carry)
    ot = acc / jnp.where(head0, l0, l1)
    o_ref[...] = ot.T


def _moba_prompt(zqkv, rel_bias, n_batch, seq):
    n_blk = seq // BLOCK
    kern = functools.partial(_moba_prompt_kernel, n_blk=n_blk)
    stage = pltpu.VMEM((FAR_BLOCKS_PER_STEP, 2, BLOCK, BLOCK), F32)
    return pl.pallas_call(
        kern,
        grid=(n_batch, N_PAIRS, n_blk),
        in_specs=[
            pl.BlockSpec(memory_space=pltpu.SMEM),
            pl.BlockSpec((BLOCK, LANES), lambda b, p, i: (b * n_blk + i, p)),
            pl.BlockSpec((seq, LANES), lambda b, p, i: (b, N_PAIRS + p)),
            pl.BlockSpec((seq, LANES), lambda b, p, i: (b, 2 * N_PAIRS + p)),
            pl.BlockSpec((2, BLOCK, BLOCK), lambda b, p, i: (0, 0, 0)),
        ],
        out_specs=pl.BlockSpec((BLOCK, LANES), lambda b, p, i: (b * n_blk + i, p)),
        out_shape=jax.ShapeDtypeStruct((n_batch * seq, D_ATTN), F32),
        scratch_shapes=[pltpu.VMEM((seq, LANES), BF16),
                        pltpu.VMEM((n_blk, LANES, BLOCK), BF16),
                        pltpu.VMEM((n_blk, LANES), F32),
                        pltpu.VMEM((2, n_blk, BLOCK), F32),
                        pltpu.VMEM((2, 2, BLOCK, BLOCK), F32),
                        stage, stage],
        compiler_params=_cparams("parallel", "parallel", "arbitrary"),
        name="moba_prompt",
    )(_relative_bias_log2(rel_bias).reshape(-1), zqkv, zqkv, zqkv, _prompt_bucket_tiles())


PAGES_PER_STEP = 8


def _split_bf16(x):
    hi = x.astype(BF16)
    return hi, (x - hi.astype(F32)).astype(BF16)


def _sample_gate_kernel(pt_ref, q_ref, seg_ref, *refs, n_past_blk, n_steps, dec_seq):
    page_refs = refs[:PAGES_PER_STEP]
    idx_ref = refs[PAGES_PER_STEP]
    sum_s = refs[PAGES_PER_STEP + 1]
    s = pl.program_id(1)
    ppb = BLOCK // PAGE_SIZE
    ones = jnp.ones((16, PAGE_SIZE), BF16)
    for bl in range(PAGES_PER_STEP // ppb):
        x = page_refs[ppb * bl][0]
        for e in range(1, ppb):
            x = x + page_refs[ppb * bl + e][0]
        hi, lo = _split_bf16(x)
        tot = _dot_nt(ones, hi) + _dot_nt(ones, lo)
        sum_s[pl.ds(s * (PAGES_PER_STEP // ppb) + bl, 1), :] = tot[:1]

    @pl.when(s == n_steps - 1)
    def _():
        blk = lax.broadcasted_iota(jnp.int32, (n_past_blk, LANES), 0)
        ksum = sum_s[...]
        for qi in range(dec_seq):
            hi, lo = _split_bf16(ksum * q_ref[0, qi:qi + 1, :])
            g = _dot(hi, seg_ref[...]) + _dot(lo, seg_ref[...])
            rows = []
            for _ in range(TOP_BLOCKS):
                m = jnp.max(g, axis=0, keepdims=True)
                first = jnp.min(jnp.where(g == m, blk, n_past_blk), axis=0, keepdims=True)
                rows.append(first)
                g = jnp.where(blk == first, NEG_INF, g)
            rows.append(jnp.zeros((8 - TOP_BLOCKS, LANES), jnp.int32))
            idx_ref[0, qi] = jnp.concatenate(rows, axis=0)


def _sample_gate(page_table, q8, cache_k_pages, n_past_blk, dec_seq):
    db = q8.shape[0]
    n_pages = page_table.shape[1]
    n_steps = n_pages // PAGES_PER_STEP
    kern = functools.partial(_sample_gate_kernel, n_past_blk=n_past_blk, n_steps=n_steps, dec_seq=dec_seq)
    col = np.arange(H_ATTN * HEAD_DIM)[:, None] // HEAD_DIM
    seg = jnp.asarray(col == np.arange(LANES)[None, :], BF16)

    def page_spec(r):
        return pl.BlockSpec((1, H_ATTN * HEAD_DIM, PAGE_SIZE),
                            lambda b, s, pt: (pt[b, s * PAGES_PER_STEP + r], 0, 0))

    return pl.pallas_call(
        kern,
        grid_spec=pltpu.PrefetchScalarGridSpec(
            num_scalar_prefetch=1,
            grid=(db, n_steps),
            in_specs=[pl.BlockSpec((1, 8, H_ATTN * HEAD_DIM), lambda b, s, pt: (b, 0, 0)),
                      pl.BlockSpec((H_ATTN * HEAD_DIM, LANES), lambda b, s, pt: (0, 0))]
            + [page_spec(r) for r in range(PAGES_PER_STEP)],
            out_specs=pl.BlockSpec((1, dec_seq, 8, LANES), lambda b, s, pt: (b, 0, 0, 0)),
            scratch_shapes=[pltpu.VMEM((n_past_blk, H_ATTN * HEAD_DIM), F32)],
        ),
        out_shape=jax.ShapeDtypeStruct((db, dec_seq, 8, LANES), jnp.int32),
        compiler_params=_cparams("parallel", "arbitrary"),
        name="sample_gate",
    )(page_table, q8, seg, *([cache_k_pages] * PAGES_PER_STEP))


def _sample_attn_kernel(slab_ref, idx_ref, qt_ref, kn_ref, vn_ref, bown_ref, blast_ref, far_ref, *refs,
                        dec_seq, n_past_blk):
    n_slab = dec_seq * TOP_BLOCKS * (BLOCK // PAGE_SIZE)
    k_refs = refs[:n_slab]
    v_refs = refs[n_slab:2 * n_slab]
    o_ref = refs[2 * n_slab]
    b = pl.program_id(0)
    h = pl.program_id(1)
    ppb = BLOCK // PAGE_SIZE
    lane8 = lax.broadcasted_iota(jnp.int32, (HEAD_DIM, 8), 1)
    out = jnp.zeros((HEAD_DIM, 8), F32)
    kn = kn_ref[0, 0]
    vn = vn_ref[0, 0]
    far = far_ref[h]
    for qi in range(dec_seq):
        qc = qt_ref[0, 0][:, qi:qi + 1] * (HEAD_DIM ** -0.5)
        rows = []
        for t in range(TOP_BLOCKS):
            sel_blk = idx_ref[((b * H_ATTN + h) * dec_seq + qi) * TOP_BLOCKS + t]
            is_last = sel_blk == n_past_blk - 1
            for e in range(ppb):
                slab = k_refs[(qi * TOP_BLOCKS + t) * ppb + e][0]
                srow = jnp.sum(slab * qc, axis=0, keepdims=True)
                rows.append(srow + jnp.where(is_last, blast_ref[0, qi, e:e + 1, :], far))
        s_own = jnp.sum(kn * qc, axis=0, keepdims=True) + bown_ref[0, qi:qi + 1, :]
        m = jnp.max(s_own, axis=-1, keepdims=True)
        for r in rows:
            m = jnp.maximum(m, jnp.max(r, axis=-1, keepdims=True))
        p_own = jnp.exp(s_own - m)
        l = jnp.sum(p_own, axis=-1, keepdims=True)
        acc = jnp.zeros((HEAD_DIM, PAGE_SIZE), F32)
        for n, r in enumerate(rows):
            pr = jnp.exp(r - m)
            l = l + jnp.sum(pr, axis=-1, keepdims=True)
            acc = acc + v_refs[qi * TOP_BLOCKS * ppb + n][0] * pr
        o = jnp.sum(acc, axis=-1, keepdims=True) + jnp.sum(vn * p_own, axis=-1, keepdims=True)
        out = jnp.where(lane8 == qi, o / l, out)
    o_ref[0, 0] = out


def _sample_attn(slabs, idx_flat, qt, knt, vnt, bias_own, bias_last, far, ck_slabs, cv_slabs, dec_seq, n_past_blk):
    db = qt.shape[0]
    ppb = BLOCK // PAGE_SIZE
    n_slab = dec_seq * TOP_BLOCKS * ppb
    kern = functools.partial(_sample_attn_kernel, dec_seq=dec_seq, n_past_blk=n_past_blk)

    def slab_spec(n):
        return pl.BlockSpec((1, HEAD_DIM, PAGE_SIZE),
                            lambda b, h, sl, ix: (sl[(b * H_ATTN + h) * n_slab + n], 0, 0))

    small = lambda shape: pl.BlockSpec(shape, lambda b, h, sl, ix: (b, h, 0, 0))
    return pl.pallas_call(
        kern,
        grid_spec=pltpu.PrefetchScalarGridSpec(
            num_scalar_prefetch=2,
            grid=(db, H_ATTN),
            in_specs=[small((1, 1, HEAD_DIM, 8)), small((1, 1, HEAD_DIM, 8)), small((1, 1, HEAD_DIM, 8)),
                      pl.BlockSpec((1, dec_seq, 8), lambda b, h, sl, ix: (h, 0, 0)),
                      pl.BlockSpec((1, dec_seq, ppb, PAGE_SIZE), lambda b, h, sl, ix: (h, 0, 0, 0)),
                      pl.BlockSpec(memory_space=pltpu.SMEM)]
            + [slab_spec(n) for n in range(n_slab)] * 2,
            out_specs=small((1, 1, HEAD_DIM, 8)),
        ),
        out_shape=jax.ShapeDtypeStruct((db, H_ATTN, HEAD_DIM, 8), F32),
        compiler_params=_cparams("parallel", "parallel"),
        name="sample_attn",
    )(slabs, idx_flat, qt, knt, vnt, bias_own, bias_last, far, *([ck_slabs] * n_slab), *([cv_slabs] * n_slab))


def _moba_sample(zqkv_s, cache_k, cache_v, page_table, rel_bias, db, dec_seq, past_len):
    n_past_blk = past_len // BLOCK
    assert past_len % BLOCK == 0 and n_past_blk >= TOP_BLOCKS and n_past_blk <= LANES
    ppb = BLOCK // PAGE_SIZE
    pool = cache_k.shape[1]
    ckt = cache_k[0].transpose(0, 2, 3, 1)
    cvt = cache_v[0].transpose(0, 2, 3, 1)
    z3 = zqkv_s.reshape(db, dec_seq, 3 * D_ATTN)
    q = z3[..., :D_ATTN]
    q8 = jnp.pad(q, ((0, 0), (0, 8 - dec_seq), (0, 0)))
    idx = _sample_gate(page_table, q8, ckt.reshape(pool, H_ATTN * HEAD_DIM, PAGE_SIZE), n_past_blk, dec_seq)
    idx = idx[:, :, :TOP_BLOCKS, :H_ATTN].transpose(0, 3, 1, 2)
    pages = jnp.take_along_axis(page_table[:, None, None, None, :],
                                (idx[..., None] * ppb + jnp.arange(ppb)).reshape(db, H_ATTN, dec_seq, 1, -1),
                                axis=-1)
    slabs = (pages.reshape(db, H_ATTN, -1) * H_ATTN + jnp.arange(H_ATTN)[None, :, None]).reshape(-1).astype(jnp.int32)

    def heads_t(t):
        t = t.reshape(db, dec_seq, H_ATTN, HEAD_DIM).transpose(0, 2, 3, 1)
        return jnp.pad(t, ((0, 0), (0, 0), (0, 0), (0, 8 - dec_seq)))

    bt = _bucket_table(BLOCK + dec_seq)
    rb = rel_bias.astype(F32)
    qi = np.arange(dec_seq)[:, None]
    t8 = np.arange(8)[None, :]
    own_ok = (t8 <= qi) & (t8 < dec_seq)
    bias_own = jnp.where(jnp.asarray(own_ok)[..., None], rb[bt[np.maximum(qi - t8, 0)]], NEG_INF).transpose(2, 0, 1)
    off = np.arange(BLOCK).reshape(ppb, PAGE_SIZE)[None]
    bias_last = rb[bt[BLOCK + qi[:, :, None] - off]].transpose(3, 0, 1, 2)
    o = _sample_attn(slabs, idx.reshape(-1).astype(jnp.int32), heads_t(q), heads_t(z3[..., D_ATTN:2 * D_ATTN]),
                     heads_t(z3[..., 2 * D_ATTN:]), bias_own, bias_last, rb[N_BUCKETS - 1],
                     ckt.reshape(pool * H_ATTN, HEAD_DIM, PAGE_SIZE), cvt.reshape(pool * H_ATTN, HEAD_DIM, PAGE_SIZE),
                     dec_seq, n_past_blk)
    return o[..., :dec_seq].transpose(0, 3, 1, 2).reshape(db * dec_seq, D_ATTN)


def _pair_ones():
    r = np.arange(LANES)
    return jnp.asarray((r[:, None] // HEAD_DIM) == (r[None, :] // HEAD_DIM), BF16)


def _head_sum(x, ones_pair):
    hi = x.astype(BF16)
    lo = (x - hi.astype(F32)).astype(BF16)
    outs = []
    for g in range(N_PAIRS):
        sl = slice(g * LANES, (g + 1) * LANES)
        outs.append(jnp.dot(hi[:, sl], ones_pair, preferred_element_type=F32)
                    + jnp.dot(lo[:, sl], ones_pair, preferred_element_type=F32))
    return jnp.concatenate(outs, axis=-1)


def _rwkv_prep_kernel(z_ref, prev_ref, mu_ref, w0_ref, w2_ref, a0_ref, a2_ref, g2_ref, kk_ref, ka_ref, ones_ref,
                      r_ref, ld_ref, k_ref, v_ref, kap_ref, b_ref, g_ref):
    z = z_ref[...]
    zs = z + (prev_ref[...] - z) * mu_ref[...]
    o = D_RWKV
    r = zs[:, :o]
    k = zs[:, o:2 * o]
    v = zs[:, 2 * o:3 * o]
    lora = zs[:, 3 * o:3 * o + LANES]
    gl = zs[:, 3 * o + LANES:]
    x = w0_ref[...] + _dot(jnp.tanh(lora), w2_ref[...])
    nx = -x
    w = -(jnp.maximum(nx, 0.0) + jnp.log(1.0 + jnp.exp(-jnp.abs(nx)))) - 0.5
    logd = -jnp.exp(w)
    a = _sigmoid(a0_ref[...] + _dot(lora, a2_ref[...]))
    g = _dot(_sigmoid(gl), g2_ref[...])
    kk = k * kk_ref[...]
    nrm = jnp.sqrt(_head_sum(kk * kk, ones_ref[...]))
    kap = kk / jnp.maximum(nrm, 1e-12)
    k2 = k * (1.0 + (a - 1.0) * ka_ref[...])
    for p in range(N_PAIRS):
        sl = slice(p * LANES, (p + 1) * LANES)
        r_ref[p] = r[:, sl]
        ld_ref[p] = logd[:, sl]
        k_ref[p] = k2[:, sl]
        v_ref[p] = v[:, sl]
        kap_ref[p] = kap[:, sl]
        b_ref[p] = (kap * a)[:, sl]
        g_ref[p] = g[:, sl]


def _rwkv_prep(zr, prev, lw, tm):
    n = zr.shape[0]
    row = lambda a: a.reshape(1, -1).astype(F32)
    zpad = jnp.zeros((DECAY_LORA, D_RWKV), F32)
    w2p = jnp.concatenate([lw['w2'], zpad], axis=0).astype(BF16)
    a2p = jnp.concatenate([zpad, lw['a2']], axis=0).astype(BF16)
    full = lambda shape: pl.BlockSpec(shape, lambda i: (0,) * len(shape))
    out_sd = jax.ShapeDtypeStruct((N_PAIRS, n, LANES), F32)
    out_spec = pl.BlockSpec((N_PAIRS, tm, LANES), lambda i: (0, i, 0))
    return pl.pallas_call(
        _rwkv_prep_kernel,
        grid=(n // tm,),
        in_specs=[pl.BlockSpec((tm, C_SHIFT), lambda i: (i, 0)), pl.BlockSpec((tm, C_SHIFT), lambda i: (i, 0)),
                  full((1, C_SHIFT)), full((1, D_RWKV)), full((LANES, D_RWKV)), full((1, D_RWKV)),
                  full((LANES, D_RWKV)), full((GATE_LORA, D_RWKV)), full((1, D_RWKV)), full((1, D_RWKV)),
                  full((LANES, LANES))],
        out_specs=[out_spec] * 7,
        out_shape=[out_sd] * 7,
        compiler_params=_cparams("parallel"),
        name="rwkv_prep",
    )(zr, prev, row(lw['mu_shift']), row(lw['w0']), w2p, row(lw['a0']), a2p, lw['g2'].astype(BF16),
      row(lw['k_k']), row(lw['k_a']), _pair_ones())


def _rwkv_lockstep_kernel(r_ref, ld_ref, k_ref, v_ref, kap_ref, b_ref, g_ref, s0_ref, lg_ref, lb_ref, rk_ref,
                          y_ref, s_ref, *, chunk):
    c = pl.program_id(1)

    @pl.when(c == 0)
    def _():
        s_ref[...] = s0_ref[...]

    heads = range(H_RWKV)
    ti = lax.broadcasted_iota(jnp.int32, (chunk, chunk), 0)
    si = lax.broadcasted_iota(jnp.int32, (chunk, chunk), 1)
    tril_incl = (si <= ti)
    tril_strict = (si < ti)
    ltri = tril_incl.astype(F32)
    eye = (si == ti).astype(F32)
    hsl = [slice((hh % 2) * HEAD_DIM, (hh % 2 + 1) * HEAD_DIM) for hh in heads]

    ld = [ld_ref[p] for p in range(N_PAIRS)]
    cum = [_dot_hi(ltri, x) for x in ld]
    kt, rt, bi, ki, bh, kh, gall = [], [], [], [], [], [], []
    for p in range(N_PAIRS):
        last = cum[p][chunk - 1:chunk, :]
        gam_inv = jnp.exp(-cum[p])
        gam_tail = jnp.exp(last - cum[p])
        kap2, b2, k2 = kap_ref[p], b_ref[p], k_ref[p]
        kt2 = (kap2 * jnp.exp(cum[p] - ld[p])).astype(BF16)
        rt2 = r_ref[p] * jnp.exp(cum[p])
        bi2 = (b2 * gam_inv).astype(BF16)
        ki2 = (k2 * gam_inv).astype(BF16)
        bh2 = (b2 * gam_tail).astype(BF16)
        kh2 = (k2 * gam_tail).astype(BF16)
        g2 = jnp.exp(last)
        for h in range(2):
            sl = hsl[h]
            kt.append(kt2[:, sl]); rt.append(rt2[:, sl]); bi.append(bi2[:, sl]); ki.append(ki2[:, sl])
            bh.append(bh2[:, sl]); kh.append(kh2[:, sl]); gall.append(g2[:, sl])
    vv = [v_ref[hh // 2][:, hsl[hh]].astype(BF16) for hh in heads]
    kr = [jnp.concatenate([kt[hh], rt[hh].astype(BF16)], axis=0) for hh in heads]
    xb = [_dot_nt(kr[hh], bi[hh]) for hh in heads]
    xk = [_dot_nt(kr[hh], ki[hh]) for hh in heads]
    a_ab = [jnp.where(tril_strict, xb[hh][:chunk], 0.0) for hh in heads]
    a_rb = [jnp.where(tril_incl, xb[hh][chunk:], 0.0).astype(BF16) for hh in heads]
    a_ak = [jnp.where(tril_strict, xk[hh][:chunk], 0.0).astype(BF16) for hh in heads]
    a_rk = [jnp.where(tril_incl, xk[hh][chunk:], 0.0).astype(BF16) for hh in heads]
    akv = [_dot(a_ak[hh], vv[hh]) for hh in heads]
    ab = [x.astype(BF16) for x in a_ab]
    tm = [eye - a_ab[hh] for hh in heads]
    pw = [_dot(ab[hh], ab[hh]) for hh in heads]
    n_sq = int(math.log2(chunk)) - 1
    for it in range(n_sq):
        pwb = [x.astype(BF16) for x in pw]
        tm = [tm[hh] + _dot(tm[hh], pwb[hh]) for hh in heads]
        if it < n_sq - 1:
            pw = [_dot(pwb[hh], pwb[hh]) for hh in heads]
    tmb = [x.astype(BF16) for x in tm]
    w = [-_dot(tmb[hh], kt[hh]) for hh in heads]
    u0 = [-_dot(tmb[hh], akv[hh]) for hh in heads]
    wb = [x.astype(BF16) for x in w]
    u0b = [x.astype(BF16) for x in u0]
    pm = [rt[hh] + _dot(a_rb[hh], wb[hh]) for hh in heads]
    y0 = [_dot(a_rb[hh], u0b[hh]) + _dot(a_rk[hh], vv[hh]) for hh in heads]
    s_old = [s_ref[0, hh] for hh in heads]
    sb = [x.astype(BF16) for x in s_old]
    y = [_dot_nt(pm[hh], sb[hh]) + y0[hh] for hh in heads]
    mt = [_dot_tn(bh[hh], wb[hh]) for hh in heads]
    nt = [_dot_tn(jnp.concatenate([u0b[hh], vv[hh]], axis=0), jnp.concatenate([bh[hh], kh[hh]], axis=0))
          for hh in heads]
    for hh in heads:
        s_ref[0, hh] = s_old[hh] * gall[hh] + _dot_nt(sb[hh], mt[hh]) + nt[hh]
    for p in range(N_PAIRS):
        outs = []
        for h in range(2):
            hh = 2 * p + h
            sl = hsl[h]
            mu = jnp.mean(y[hh], axis=-1, keepdims=True)
            var = jnp.mean(jnp.square(y[hh] - mu), axis=-1, keepdims=True)
            yn = (y[hh] - mu) * lax.rsqrt(var + LNX_EPS)
            rk = r_ref[p][:, sl] * k_ref[p][:, sl] * rk_ref[p][:, sl]
            bonus = jnp.sum(rk, axis=-1, keepdims=True) * v_ref[p][:, sl]
            outs.append(yn * lg_ref[p][:, sl] + lb_ref[p][:, sl] + bonus)
        y_ref[p] = jnp.concatenate(outs, axis=-1) * g_ref[p]


def _rwkv_chunks(prep, s0, lw, n_batch, seq, chunk):
    n_chunks = seq // chunk
    pairs = lambda a: a.reshape(N_PAIRS, 1, LANES).astype(F32)
    tile = pl.BlockSpec((N_PAIRS, chunk, LANES), lambda b, c: (0, b * n_chunks + c, 0))
    state = pl.BlockSpec((1, H_RWKV, HEAD_DIM, HEAD_DIM), lambda b, c: (b, 0, 0, 0))
    par = pl.BlockSpec((N_PAIRS, 1, LANES), lambda b, c: (0, 0, 0))
    return pl.pallas_call(
        functools.partial(_rwkv_lockstep_kernel, chunk=chunk),
        grid=(n_batch, n_chunks),
        in_specs=[tile] * 7 + [state, par, par, par],
        out_specs=[tile, state],
        out_shape=[jax.ShapeDtypeStruct((N_PAIRS, n_batch * seq, LANES), F32),
                   jax.ShapeDtypeStruct((n_batch, H_RWKV, HEAD_DIM, HEAD_DIM), F32)],
        compiler_params=_cparams("parallel", "arbitrary"),
        name="rwkv_chunks",
    )(*prep, s0.astype(F32), pairs(lw['lnx_g']), pairs(lw['lnx_b']), pairs(lw['r_k']))


def _rwkv(zr, shift0, s0, lw, n_batch, seq):
    z3 = zr.reshape(n_batch, seq, C_SHIFT)
    prev = jnp.concatenate([shift0[:, None].astype(F32), z3[:, :-1]], axis=1).reshape(n_batch * seq, C_SHIFT)
    prep = _rwkv_prep(zr, prev, lw, min(256, n_batch * seq))
    chunk = CHUNK if seq >= CHUNK else SHORT_CHUNK
    pad = (-seq) % chunk
    if pad:
        prep = [jnp.pad(a.reshape(N_PAIRS, n_batch, seq, LANES), ((0, 0), (0, 0), (0, pad), (0, 0)))
                .reshape(N_PAIRS, n_batch * (seq + pad), LANES) for a in prep]
    y, s_fin = _rwkv_chunks(prep, s0, lw, n_batch, seq + pad, chunk)
    if pad:
        y = y.reshape(N_PAIRS, n_batch, seq + pad, LANES)[:, :, :seq].reshape(N_PAIRS, n_batch * seq, LANES)
    return y, s_fin, z3[:, -1]


def _layer_norm(h, g, b):
    mu = jnp.mean(h, axis=-1, keepdims=True)
    d = h - mu
    var = jnp.mean(d * d, axis=-1, keepdims=True)
    return d * lax.rsqrt(var + LN_EPS) * g + b


def _out_proj_kernel(oa_ref, or_ref, x_ref, wa_ref, wr_ref, g_ref, b_ref, rw_ref, x1_ref, x1b_ref, sc_ref):
    orw = jnp.concatenate([or_ref[p] for p in range(N_PAIRS)], axis=-1)
    mix = _dot(oa_ref[...], wa_ref[...]) + _dot(orw, wr_ref[...])
    x1 = _layer_norm(ALPHA * x_ref[...] + mix, g_ref[...], b_ref[...])
    x1_ref[...] = x1
    x1b_ref[...] = x1.astype(BF16)
    sc_ref[...] = _sigmoid(lax.dot_general(rw_ref[...], x1, (((1,), (1,)), ((), ())),
                                           precision=lax.Precision.HIGHEST, preferred_element_type=F32))


def _out_proj(o_attn, o_rwkv, x, lw, tm):
    n = x.shape[0]
    wo = lw['w_out'].astype(BF16)
    full = lambda shape: pl.BlockSpec(shape, lambda i: (0,) * len(shape))
    return pl.pallas_call(
        _out_proj_kernel,
        grid=(n // tm,),
        in_specs=[pl.BlockSpec((tm, D_ATTN), lambda i: (i, 0)),
                  pl.BlockSpec((N_PAIRS, tm, LANES), lambda i: (0, i, 0)),
                  pl.BlockSpec((tm, D_MODEL), lambda i: (i, 0)),
                  full((D_ATTN, D_MODEL)), full((D_RWKV, D_MODEL)), full((1, D_MODEL)), full((1, D_MODEL)),
                  full((N_EXPERTS, D_MODEL))],
        out_specs=[pl.BlockSpec((tm, D_MODEL), lambda i: (i, 0)), pl.BlockSpec((tm, D_MODEL), lambda i: (i, 0)),
                   pl.BlockSpec((N_EXPERTS, tm), lambda i: (0, i))],
        out_shape=[jax.ShapeDtypeStruct((n, D_MODEL), F32), jax.ShapeDtypeStruct((n, D_MODEL), BF16),
                   jax.ShapeDtypeStruct((N_EXPERTS, n), F32)],
        compiler_params=_cparams("parallel"),
        name="out_proj_ln_router",
    )(o_attn, o_rwkv, x, wo[:D_ATTN], wo[D_ATTN:], lw['ln1_g'].reshape(1, -1), lw['ln1_b'].reshape(1, -1),
      lw['router_w'].astype(F32).T)


def _expert_kernel(be_ref, na_ref, x_ref, wg_ref, wu_ref, wd_ref, y_ref, wg_s, wu_s, wd_s):
    g = pl.program_id(0)
    active = g < na_ref[0]

    @pl.when(active & ((g == 0) | (be_ref[g] != be_ref[jnp.maximum(g - 1, 0)])))
    def _():
        wg_s[...] = wg_ref[0].astype(BF16)
        wu_s[...] = wu_ref[0].astype(BF16)
        wd_s[...] = wd_ref[0].astype(BF16)

    @pl.when(active)
    def _():
        x = x_ref[...]
        hg = jnp.dot(x, wg_s[...], preferred_element_type=F32)
        hu = jnp.dot(x, wu_s[...], preferred_element_type=F32)
        h = hg * _sigmoid(hg) * hu
        y_ref[...] = jnp.dot(h.astype(BF16), wd_s[...], preferred_element_type=F32).astype(y_ref.dtype)

    @pl.when(g >= na_ref[0])
    def _():
        y_ref[...] = jnp.zeros_like(y_ref)


def _expert_blocks(xs, blk_e, n_active, wg, wu, wd, tm, out_dtype):
    rows = xs.shape[0]
    n_blocks = rows // tm
    d_e = wg.shape[-1]
    return pl.pallas_call(
        _expert_kernel,
        grid_spec=pltpu.PrefetchScalarGridSpec(
            num_scalar_prefetch=2,
            grid=(n_blocks,),
            in_specs=[pl.BlockSpec((tm, D_MODEL), lambda g, be, na: (g, 0)),
                      pl.BlockSpec((1, D_MODEL, d_e), lambda g, be, na: (be[g], 0, 0)),
                      pl.BlockSpec((1, D_MODEL, d_e), lambda g, be, na: (be[g], 0, 0)),
                      pl.BlockSpec((1, d_e, D_MODEL), lambda g, be, na: (be[g], 0, 0))],
            out_specs=pl.BlockSpec((tm, D_MODEL), lambda g, be, na: (g, 0)),
            scratch_shapes=[pltpu.VMEM((D_MODEL, d_e), BF16), pltpu.VMEM((D_MODEL, d_e), BF16),
                            pltpu.VMEM((d_e, D_MODEL), BF16)],
        ),
        out_shape=jax.ShapeDtypeStruct((rows, D_MODEL), out_dtype),
        compiler_params=_cparams("arbitrary"),
        name="expert_blocks",
    )(blk_e, n_active, xs, wg, wu, wd)


def _first_max(x, idx, n):
    m = jnp.max(x, axis=0, keepdims=True)
    return m, jnp.min(jnp.where(x == m, idx, n), axis=0, keepdims=True)


def _route_kernel(sc_ref, bias_ref, tri_ref, e_ref, gate_ref, rank_ref, cnt_ref, carry_s):
    i = pl.program_id(0)
    tr = sc_ref.shape[1]
    per_group = N_EXPERTS // N_GROUPS

    @pl.when(i == 0)
    def _():
        carry_s[...] = jnp.zeros_like(carry_s)

    sc = sc_ref[...]
    choice = sc + bias_ref[...]
    in_g = lax.broadcasted_iota(jnp.int32, (per_group, tr), 0)
    g_iota = lax.broadcasted_iota(jnp.int32, (N_GROUPS, tr), 0)
    grp = jnp.zeros((N_GROUPS, tr), F32)
    for gi in range(N_GROUPS):
        cg = choice[gi * per_group:(gi + 1) * per_group]
        m1, i1 = _first_max(cg, in_g, per_group)
        m2 = jnp.max(jnp.where(in_g == i1, NEG_INF, cg), axis=0, keepdims=True)
        grp = jnp.where(g_iota == gi, m1 + m2, grp)
    keep = jnp.zeros((N_GROUPS, tr), F32)
    for _ in range(TOPK_GROUPS):
        _, f = _first_max(grp, g_iota, N_GROUPS)
        hit = g_iota == f
        keep = jnp.where(hit, 1.0, keep)
        grp = jnp.where(hit, NEG_INF, grp)
    masked = jnp.concatenate(
        [jnp.where(keep[gi:gi + 1] > 0.0, choice[gi * per_group:(gi + 1) * per_group], NEG_INF)
         for gi in range(N_GROUPS)], axis=0)
    e_iota = lax.broadcasted_iota(jnp.int32, (N_EXPERTS, tr), 0)
    hits, es, ss = [], [], []
    onehot = jnp.zeros((N_EXPERTS, tr), F32)
    for _ in range(TOP_K):
        _, f = _first_max(masked, e_iota, N_EXPERTS)
        hit = e_iota == f
        hits.append(hit)
        es.append(f)
        ss.append(jnp.sum(jnp.where(hit, sc, 0.0), axis=0, keepdims=True))
        onehot = jnp.where(hit, 1.0, onehot)
        masked = jnp.where(hit, NEG_INF, masked)
    total = ss[0]
    for s in ss[1:]:
        total = total + s
    e_ref[...] = jnp.concatenate(es, axis=0)
    gate_ref[...] = jnp.concatenate([s / total * ROUTED_SCALE for s in ss], axis=0)
    before = carry_s[...] + jnp.dot(onehot.astype(BF16), tri_ref[...], preferred_element_type=F32)
    rank_ref[...] = jnp.concatenate(
        [jnp.sum(jnp.where(h, before, 0.0), axis=0, keepdims=True) for h in hits], axis=0).astype(jnp.int32)
    carry = carry_s[...] + jnp.sum(onehot, axis=1, keepdims=True)
    carry_s[...] = carry
    cnt_ref[...] = jnp.broadcast_to(carry, cnt_ref.shape).astype(jnp.int32)


def _route(scores_t, router_bias):
    n = scores_t.shape[1]
    tr = min(256, n)
    tri = jnp.asarray(np.arange(tr)[:, None] < np.arange(tr)[None, :], BF16)
    col = pl.BlockSpec((TOP_K, tr), lambda i: (0, i))
    top_e, gate, rank, counts = pl.pallas_call(
        _route_kernel,
        grid=(n // tr,),
        in_specs=[pl.BlockSpec((N_EXPERTS, tr), lambda i: (0, i)),
                  pl.BlockSpec((N_EXPERTS, 1), lambda i: (0, 0)),
                  pl.BlockSpec((tr, tr), lambda i: (0, 0))],
        out_specs=[col, col, col, pl.BlockSpec((N_EXPERTS, LANES), lambda i: (0, 0))],
        out_shape=[jax.ShapeDtypeStruct((TOP_K, n), jnp.int32), jax.ShapeDtypeStruct((TOP_K, n), F32),
                   jax.ShapeDtypeStruct((TOP_K, n), jnp.int32), jax.ShapeDtypeStruct((N_EXPERTS, LANES), jnp.int32)],
        scratch_shapes=[pltpu.VMEM((N_EXPERTS, 1), F32)],
        compiler_params=_cparams("arbitrary"),
        name="route",
    )(scores_t, router_bias.astype(F32).reshape(N_EXPERTS, 1), tri)
    counts = counts[:, 0]
    pcounts = (counts + MOE_TM - 1) // MOE_TM * MOE_TM
    pends = jnp.cumsum(pcounts)
    starts = pends - pcounts
    dest = rank
    for e in range(N_EXPERTS):
        dest = dest + jnp.where(top_e == e, starts[e], 0)
    n_blocks = -(-(n * TOP_K) // MOE_TM) + N_EXPERTS
    slot_tok = jnp.zeros((n_blocks * MOE_TM,), jnp.int32).at[dest.reshape(-1)].set(
        jnp.tile(jnp.arange(n, dtype=jnp.int32), TOP_K))
    blk_start = jnp.arange(n_blocks, dtype=jnp.int32) * MOE_TM
    blk_e = jnp.minimum(jnp.sum((pends[None, :] <= blk_start[:, None]).astype(jnp.int32), axis=1), N_EXPERTS - 1)
    n_active = (pends[-1] // MOE_TM).astype(jnp.int32).reshape(1)
    return gate.T, dest.astype(jnp.int32), slot_tok, blk_e, n_active


def _combine_kernel(x1_ref, sh_ref, ys_ref, gate_ref, g_ref, b_ref, o_ref):
    gate = gate_ref[...]
    f = sh_ref[...]
    for j in range(TOP_K):
        f = f + ys_ref[j].astype(F32) * gate[:, j:j + 1]
    o_ref[...] = _layer_norm(ALPHA * x1_ref[...] + f, g_ref[...], b_ref[...])


def _combine(x1, shared, ysg, gate, lw, tm):
    n = x1.shape[0]
    row = pl.BlockSpec((tm, D_MODEL), lambda i: (i, 0))
    par = pl.BlockSpec((1, D_MODEL), lambda i: (0, 0))
    return pl.pallas_call(
        _combine_kernel,
        grid=(n // tm,),
        in_specs=[row, row, pl.BlockSpec((TOP_K, tm, D_MODEL), lambda i: (0, i, 0)),
                  pl.BlockSpec((tm, TOP_K), lambda i: (i, 0)), par, par],
        out_specs=row,
        out_shape=jax.ShapeDtypeStruct((n, D_MODEL), F32),
        compiler_params=_cparams("parallel"),
        name="moe_combine_ln",
    )(x1, shared, ysg, gate, lw['ln2_g'].reshape(1, -1), lw['ln2_b'].reshape(1, -1))


def _moe(x1, x1b, scores_t, lw, ew):
    n = x1.shape[0]
    gate, dest, slot_tok, blk_e, n_active = _route(scores_t, lw['router_bias'])
    xs = x1b[slot_tok]
    ys = _expert_blocks(xs, blk_e, n_active, ew['e_gate'], ew['e_up'], ew['e_down'], MOE_TM, BF16)
    tm_s = min(MOE_TM, n)
    shared = _expert_blocks(x1b, jnp.zeros((n // tm_s,), jnp.int32), jnp.full((1,), n // tm_s, jnp.int32),
                            ew['s_gate'], ew['s_up'], ew['s_down'], tm_s, F32)
    ysg = ys[dest.reshape(-1)].reshape(TOP_K, n, D_MODEL)
    return _combine(x1, shared, ysg, gate, lw, min(128, n))


def _layer(x, attend, shift0, s0, lw, ew, w_in_b, n_batch, seq):
    n = n_batch * seq
    x2 = x.reshape(n, D_MODEL)
    xb = x2.astype(BF16)
    tm = min(512, n)
    zqkv = _matmul(xb, w_in_b[:, :3 * D_ATTN], tm, 768)
    zr = _matmul(xb, w_in_b[:, 3 * D_ATTN:], tm, C_SHIFT // 2)
    o_attn = attend(zqkv)
    o_rwkv, s_new, shift_new = _rwkv(zr, shift0, s0, lw, n_batch, seq)
    x1, x1b, scores = _out_proj(o_attn, o_rwkv, x2, lw, min(256, n))
    y = _moe(x1, x1b, scores, lw, ew)
    k = zqkv[:, D_ATTN:2 * D_ATTN].reshape(n_batch, seq, H_ATTN, HEAD_DIM)
    v = zqkv[:, 2 * D_ATTN:].reshape(n_batch, seq, H_ATTN, HEAD_DIM)
    return y.reshape(n_batch, seq, D_MODEL), k, v, s_new, shift_new


def kernel(x_prompt, x_sample, cache_k, cache_v, state_wkv, state_shift, page_table, rel_bias, w_in, mu_shift, w0, w2,
           a0, a2, g2, k_k, k_a, r_k, lnx_g, lnx_b, w_out, ln1_g, ln1_b, router_w, router_bias, e_gate, e_up, e_down,
           s_gate, s_up, s_down, ln2_g, ln2_b):
    assert w_in.shape[0] == DEPTH == 1
    n_b, seq, _ = x_prompt.shape
    db, dec_seq, _ = x_sample.shape
    past_len = page_table.shape[1] * PAGE_SIZE
    lw = {'mu_shift': mu_shift[0], 'w0': w0[0], 'w2': w2[0], 'a0': a0[0], 'a2': a2[0], 'g2': g2[0], 'k_k': k_k[0],
          'k_a': k_a[0], 'r_k': r_k[0], 'lnx_g': lnx_g[0], 'lnx_b': lnx_b[0], 'w_out': w_out[0], 'ln1_g': ln1_g[0],
          'ln1_b': ln1_b[0], 'router_w': router_w[0], 'router_bias': router_bias[0], 'ln2_g': ln2_g[0],
          'ln2_b': ln2_b[0]}
    ew = {'e_gate': e_gate[0], 'e_up': e_up[0], 'e_down': e_down[0],
          's_gate': s_gate, 's_up': s_up, 's_down': s_down}
    w_in_b = w_in[0].astype(BF16)

    y_p, k_p, v_p, w_p, s_p = _layer(
        x_prompt, lambda z: _moba_prompt(z, rel_bias, n_b, seq),
        jnp.zeros((n_b, C_SHIFT), F32), jnp.zeros((n_b, H_RWKV, HEAD_DIM, HEAD_DIM), F32), lw, ew, w_in_b, n_b, seq)
    y_s, k_s, v_s, w_s, s_s = _layer(
        x_sample, lambda z: _moba_sample(z, cache_k, cache_v, page_table, rel_bias, db, dec_seq, past_len),
        state_shift[0], state_wkv[0], lw, ew, w_in_b, db, dec_seq)
    return (y_p, y_s, k_p[None], v_p[None], w_p[None], s_p[None], k_s[None], v_s[None], w_s[None], s_s[None])
```

```python
import functools
import math

import numpy as np
import jax
import jax.numpy as jnp
from jax import lax
from jax.experimental import pallas as pl
from jax.experimental.pallas import tpu as pltpu

F32 = jnp.float32
BF16 = jnp.bfloat16

D_MODEL = 2048
HEAD_DIM = 64
D_ATTN = 1024
D_RWKV = 1024
H_ATTN = 16
H_RWKV = 16
N_PAIRS = 8
LANES = 128
BLOCK = 256
TOP_BLOCKS = 3
PAGE_SIZE = 128
N_BUCKETS = 32
MAX_DISTANCE = 128
DECAY_LORA = 64
AAA_LORA = 64
GATE_LORA = 128
C_SHIFT = 3 * D_RWKV + DECAY_LORA + AAA_LORA + GATE_LORA
LNX_EPS = 64e-5
LN_EPS = 1e-5
N_EXPERTS = 64
TOP_K = 8
N_GROUPS = 8
TOPK_GROUPS = 4
D_EXPERT = 512
ROUTED_SCALE = 2.5
DEPTH = 1
ALPHA = (2.0 * DEPTH) ** 0.25
CHUNK = 64
MOE_TM = 512
SHORT_CHUNK = 16
FAR_BLOCKS_PER_STEP = 2
VMEM_LIMIT = 56 * 1024 * 1024
NEG_INF = float("-inf")
LOG2E = math.log2(math.e)


def _cparams(*sem):
    return pltpu.CompilerParams(dimension_semantics=sem, vmem_limit_bytes=VMEM_LIMIT)


def _dot(a, b):
    return jnp.dot(a.astype(BF16), b.astype(BF16), preferred_element_type=F32)


def _dot_nt(a, b):
    return lax.dot_general(a.astype(BF16), b.astype(BF16), (((1,), (1,)), ((), ())), preferred_element_type=F32)


def _dot_tn(a, b):
    return lax.dot_general(a.astype(BF16), b.astype(BF16), (((0,), (0,)), ((), ())), preferred_element_type=F32)


def _dot_hi(a, b):
    return jnp.dot(a, b, precision=lax.Precision.HIGHEST, preferred_element_type=F32)


def _sigmoid(x):
    return 1.0 / (1.0 + jnp.exp(-x))


def _mm_kernel(x_ref, w_ref, o_ref):
    o_ref[...] = jnp.dot(x_ref[...], w_ref[...], preferred_element_type=F32)


def _matmul(x, w, tm, tn):
    m, k = x.shape
    n = w.shape[1]
    return pl.pallas_call(
        _mm_kernel,
        grid=(m // tm, n // tn),
        in_specs=[pl.BlockSpec((tm, k), lambda i, j: (i, 0)),
                  pl.BlockSpec((k, tn), lambda i, j: (0, j))],
        out_specs=pl.BlockSpec((tm, tn), lambda i, j: (i, j)),
        out_shape=jax.ShapeDtypeStruct((m, n), F32),
        compiler_params=_cparams("parallel", "parallel"),
        name="in_proj",
    )(x, w)


def _bucket_table(max_dist):
    n = np.arange(max_dist + 1)
    max_exact = N_BUCKETS // 2
    nf = np.maximum(n, 1).astype(np.float64)
    large = max_exact + (np.log(nf / max_exact) / math.log(MAX_DISTANCE / max_exact)
                         * (N_BUCKETS - max_exact)).astype(np.int64)
    large = np.minimum(large, N_BUCKETS - 1)
    return np.where(n < max_exact, n, large).astype(np.int32)


def _relative_bias_log2(rel_bias):
    rb = rel_bias.astype(F32)
    return (rb - rb[N_BUCKETS - 1:]) * LOG2E


def _prompt_bucket_tiles():
    kpos = np.arange(BLOCK)[:, None]
    qpos = np.arange(BLOCK)[None, :]
    bt = _bucket_table(2 * BLOCK)
    d_own = qpos - kpos
    own = np.where(d_own >= 0, bt[np.maximum(d_own, 0)], -1)
    return jnp.asarray(np.stack([own, bt[BLOCK + qpos - kpos]]).astype(np.int32))


def _moba_prompt_kernel(rb_ref, q_ref, k_ref, v_ref, bkt_ref, o_ref, kb_s, vt_s, km_s, sel_s, bias_s, sa_s, sb_s,
                        *, n_blk):
    p = pl.program_id(1)
    i = pl.program_id(2)

    @pl.when(i == 0)
    def _():
        for kind in range(2):
            bk = bkt_ref[kind]
            for h in range(2):
                t = jnp.zeros((BLOCK, BLOCK), F32)
                for b in range(N_BUCKETS):
                    t = jnp.where(bk == b, rb_ref[b * H_ATTN + 2 * p + h], t)
                bias_s[h, kind] = jnp.where(bk < 0, NEG_INF, t)

        def fill(j, c):
            r0 = pl.multiple_of(j * BLOCK, BLOCK)
            kblk = k_ref[pl.ds(r0, BLOCK), :]
            vblk = v_ref[pl.ds(r0, BLOCK), :]
            kb_s[pl.ds(r0, BLOCK), :] = kblk.astype(BF16)
            vt_s[j] = vblk.T.astype(BF16)
            km_s[pl.ds(j, 1), :] = jnp.sum(kblk, axis=0, keepdims=True) * (1.0 / BLOCK)
            return c
        lax.fori_loop(0, n_blk, fill, 0)

    qt = q_ref[...].T
    row = lax.broadcasted_iota(jnp.int32, (LANES, BLOCK), 0)
    head0 = row < HEAD_DIM
    qts = [jnp.where(head0, qt, 0.0), jnp.where(head0, 0.0, qt)]
    qtb = [(x * (HEAD_DIM ** -0.5 * LOG2E)).astype(BF16) for x in qts]

    blk = lax.broadcasted_iota(jnp.int32, (n_blk, BLOCK), 0)
    valid = blk < i
    km = km_s[...]
    for h in range(2):
        g = jnp.where(valid, _dot_hi(km, qts[h]), NEG_INF)
        sel = jnp.zeros((n_blk, BLOCK), F32)
        for _ in range(TOP_BLOCKS):
            m = jnp.max(g, axis=0, keepdims=True)
            first = jnp.min(jnp.where(g == m, blk, n_blk), axis=0, keepdims=True)
            hit = blk == first
            sel = jnp.where(hit & valid, 1.0, sel)
            g = jnp.where(hit, NEG_INF, g)
        sel_s[h] = sel

    def scores(j, h):
        r0 = pl.multiple_of(j * BLOCK, BLOCK)
        return jnp.dot(kb_s[pl.ds(r0, BLOCK), :], qtb[h], preferred_element_type=F32)

    def pv(j, pb0, pb1):
        vt = vt_s[j]
        return jnp.concatenate(
            [jnp.dot(vt[:HEAD_DIM], pb0, preferred_element_type=F32),
             jnp.dot(vt[HEAD_DIM:], pb1, preferred_element_type=F32)], axis=0)

    def update(blocks, carry):
        m0, m1, l0, l1, acc = carry
        new = []
        for h, (m, l) in enumerate(((m0, l0), (m1, l1))):
            mn = m
            for blk_ in blocks:
                mn = jnp.where(blk_[3 + h], jnp.maximum(mn, jnp.max(blk_[1 + h], axis=0, keepdims=True)), mn)
            a = jnp.exp2(m - mn)
            ln = a * l
            pbs = []
            for blk_ in blocks:
                pr = jnp.exp2(blk_[1 + h] - jnp.where(blk_[3 + h], mn, jnp.inf))
                ln = ln + jnp.sum(pr, axis=0, keepdims=True)
                pbs.append(pr.astype(BF16))
            new.append((mn, a, ln, pbs))
        acc = jnp.where(head0, new[0][1], new[1][1]) * acc
        for n, blk_ in enumerate(blocks):
            acc = acc + pv(blk_[0], new[0][3][n], new[1][3][n])
        return new[0][0], new[1][0], new[0][2], new[1][2], acc

    jp = jnp.maximum(i - 1, 0)
    has_prev = i > 0
    keep0 = (sel_s[0, pl.ds(jp, 1), :] > 0.0) & has_prev
    keep1 = (sel_s[1, pl.ds(jp, 1), :] > 0.0) & has_prev
    always = jnp.full((1, BLOCK), True)
    row_init = jnp.full((1, BLOCK), NEG_INF, F32)
    carry = (row_init, row_init, jnp.zeros((1, BLOCK), F32), jnp.zeros((1, BLOCK), F32),
             jnp.zeros((LANES, BLOCK), F32))
    n_far = jnp.maximum(i - 1, 0)

    def issue_scores(t, dst):
        for u in range(FAR_BLOCKS_PER_STEP):
            jc = jnp.minimum(t * FAR_BLOCKS_PER_STEP + u, n_blk - 1)
            for h in range(2):
                dst[u, h] = scores(jc, h)

    def consume(t, src, carry):
        blocks = []
        for u in range(FAR_BLOCKS_PER_STEP):
            j = t * FAR_BLOCKS_PER_STEP + u
            in_range = j < n_far
            jc = jnp.minimum(j, n_blk - 1)
            blocks.append((jc, src[u, 0], src[u, 1],
                           (sel_s[0, pl.ds(jc, 1), :] > 0.0) & in_range,
                           (sel_s[1, pl.ds(jc, 1), :] > 0.0) & in_range))
        return update(blocks, carry)

    issue_scores(0, sa_s)
    carry = update([(i, scores(i, 0) + bias_s[0, 0], scores(i, 1) + bias_s[1, 0], always, always),
                    (jp, scores(jp, 0) + bias_s[0, 1], scores(jp, 1) + bias_s[1, 1], keep0, keep1)], carry)

    def far_body(it, carry):
        issue_scores(2 * it + 1, sb_s)
        carry = consume(2 * it, sa_s, carry)
        issue_scores(2 * it + 2, sa_s)
        return consume(2 * it + 1, sb_s, carry)

    n_iter = (n_far + 2 * FAR_BLOCKS_PER_STEP - 1) // (2 * FAR_BLOCKS_PER_STEP)
    m0, m1, l0, l1, acc = lax.fori_loop(0, n_iter, far_body, carry)
    ot = acc / jnp.where(head0, l0, l1)
    o_ref[...] = ot.T


def _moba_prompt(zqkv, rel_bias, n_batch, seq):
    n_blk = seq // BLOCK
    kern = functools.partial(_moba_prompt_kernel, n_blk=n_blk)
    stage = pltpu.VMEM((FAR_BLOCKS_PER_STEP, 2, BLOCK, BLOCK), F32)
    return pl.pallas_call(
        kern,
        grid=(n_batch, N_PAIRS, n_blk),
        in_specs=[
            pl.BlockSpec(memory_space=pltpu.SMEM),
            pl.BlockSpec((BLOCK, LANES), lambda b, p, i: (b * n_blk + i, p)),
            pl.BlockSpec((seq, LANES), lambda b, p, i: (b, N_PAIRS + p)),
            pl.BlockSpec((seq, LANES), lambda b, p, i: (b, 2 * N_PAIRS + p)),
            pl.BlockSpec((2, BLOCK, BLOCK), lambda b, p, i: (0, 0, 0)),
        ],
        out_specs=pl.BlockSpec((BLOCK, LANES), lambda b, p, i: (b * n_blk + i, p)),
        out_shape=jax.ShapeDtypeStruct((n_batch * seq, D_ATTN), F32),
        scratch_shapes=[pltpu.VMEM((seq, LANES), BF16),
                        pltpu.VMEM((n_blk, LANES, BLOCK), BF16),
                        pltpu.VMEM((n_blk, LANES), F32),
                        pltpu.VMEM((2, n_blk, BLOCK), F32),
                        pltpu.VMEM((2, 2, BLOCK, BLOCK), F32),
                        stage, stage],
        compiler_params=_cparams("parallel", "parallel", "arbitrary"),
        name="moba_prompt",
    )(_relative_bias_log2(rel_bias).reshape(-1), zqkv, zqkv, zqkv, _prompt_bucket_tiles())


PAGES_PER_STEP = 16


def _split_bf16(x):
    hi = x.astype(BF16)
    return hi, (x - hi.astype(F32)).astype(BF16)


def _sample_gate_kernel(pt_ref, q_ref, seg_ref, *refs, n_past_blk, n_steps, dec_seq):
    page_refs = refs[:PAGES_PER_STEP]
    idx_ref = refs[PAGES_PER_STEP]
    sum_s = refs[PAGES_PER_STEP + 1]
    s = pl.program_id(1)
    ppb = BLOCK // PAGE_SIZE
    ones = jnp.ones((16, PAGE_SIZE), BF16)
    for bl in range(PAGES_PER_STEP // ppb):
        x = page_refs[ppb * bl][0]
        for e in range(1, ppb):
            x = x + page_refs[ppb * bl + e][0]
        hi, lo = _split_bf16(x)
        tot = _dot_nt(ones, hi) + _dot_nt(ones, lo)
        sum_s[pl.ds(s * (PAGES_PER_STEP // ppb) + bl, 1), :] = tot[:1]

    @pl.when(s == n_steps - 1)
    def _():
        blk = lax.broadcasted_iota(jnp.int32, (n_past_blk, LANES), 0)
        ksum = sum_s[...]
        for qi in range(dec_seq):
            hi, lo = _split_bf16(ksum * q_ref[0, qi:qi + 1, :])
            g = _dot(hi, seg_ref[...]) + _dot(lo, seg_ref[...])
            rows = []
            for _ in range(TOP_BLOCKS):
                m = jnp.max(g, axis=0, keepdims=True)
                first = jnp.min(jnp.where(g == m, blk, n_past_blk), axis=0, keepdims=True)
                rows.append(first)
                g = jnp.where(blk == first, NEG_INF, g)
            rows.append(jnp.zeros((8 - TOP_BLOCKS, LANES), jnp.int32))
            idx_ref[0, qi] = jnp.concatenate(rows, axis=0)


def _sample_gate(page_table, q8, cache_k_pages, n_past_blk, dec_seq):
    db = q8.shape[0]
    n_pages = page_table.shape[1]
    n_steps = n_pages // PAGES_PER_STEP
    kern = functools.partial(_sample_gate_kernel, n_past_blk=n_past_blk, n_steps=n_steps, dec_seq=dec_seq)
    col = np.arange(H_ATTN * HEAD_DIM)[:, None] // HEAD_DIM
    seg = jnp.asarray(col == np.arange(LANES)[None, :], BF16)

    def page_spec(r):
        return pl.BlockSpec((1, H_ATTN * HEAD_DIM, PAGE_SIZE),
                            lambda b, s, pt: (pt[b, s * PAGES_PER_STEP + r], 0, 0))

    return pl.pallas_call(
        kern,
        grid_spec=pltpu.PrefetchScalarGridSpec(
            num_scalar_prefetch=1,
            grid=(db, n_steps),
            in_specs=[pl.BlockSpec((1, 8, H_ATTN * HEAD_DIM), lambda b, s, pt: (b, 0, 0)),
                      pl.BlockSpec((H_ATTN * HEAD_DIM, LANES), lambda b, s, pt: (0, 0))]
            + [page_spec(r) for r in range(PAGES_PER_STEP)],
            out_specs=pl.BlockSpec((1, dec_seq, 8, LANES), lambda b, s, pt: (b, 0, 0, 0)),
            scratch_shapes=[pltpu.VMEM((n_past_blk, H_ATTN * HEAD_DIM), F32)],
        ),
        out_shape=jax.ShapeDtypeStruct((db, dec_seq, 8, LANES), jnp.int32),
        compiler_params=_cparams("parallel", "arbitrary"),
        name="sample_gate",
    )(page_table, q8, seg, *([cache_k_pages] * PAGES_PER_STEP))


def _sample_attn_kernel(slab_ref, idx_ref, qt_ref, kn_ref, vn_ref, bown_ref, blast_ref, far_ref, *refs,
                        dec_seq, n_past_blk):
    n_slab = dec_seq * TOP_BLOCKS * (BLOCK // PAGE_SIZE)
    k_refs = refs[:n_slab]
    v_refs = refs[n_slab:2 * n_slab]
    o_ref = refs[2 * n_slab]
    b = pl.program_id(0)
    h = pl.program_id(1)
    ppb = BLOCK // PAGE_SIZE
    lane8 = lax.broadcasted_iota(jnp.int32, (HEAD_DIM, 8), 1)
    out = jnp.zeros((HEAD_DIM, 8), F32)
    kn = kn_ref[0, 0]
    vn = vn_ref[0, 0]
    far = far_ref[h]
    per_q = TOP_BLOCKS * ppb
    queries = range(dec_seq)
    qcs = [qt_ref[0, 0][:, qi:qi + 1] * (HEAD_DIM ** -0.5) for qi in queries]
    rows = []
    for qi in queries:
        for t in range(TOP_BLOCKS):
            sel_blk = idx_ref[((b * H_ATTN + h) * dec_seq + qi) * TOP_BLOCKS + t]
            is_last = sel_blk == n_past_blk - 1
            for e in range(ppb):
                slab = k_refs[qi * per_q + t * ppb + e][0]
                srow = jnp.sum(slab * qcs[qi], axis=0, keepdims=True)
                rows.append(srow + jnp.where(is_last, blast_ref[0, qi, e:e + 1, :], far))
    s_own = [jnp.sum(kn * qcs[qi], axis=0, keepdims=True) + bown_ref[0, qi:qi + 1, :] for qi in queries]
    row_max = [functools.reduce(jnp.maximum, rows[qi * per_q:(qi + 1) * per_q]) for qi in queries]
    m = [jnp.maximum(jnp.max(row_max[qi], axis=-1, keepdims=True), jnp.max(s_own[qi], axis=-1, keepdims=True))
         for qi in queries]
    p_own = [jnp.exp(s_own[qi] - m[qi]) for qi in queries]
    prs = [jnp.exp(rows[n] - m[n // per_q]) for n in range(len(rows))]
    l = [jnp.sum(functools.reduce(jnp.add, prs[qi * per_q:(qi + 1) * per_q]), axis=-1, keepdims=True)
         + jnp.sum(p_own[qi], axis=-1, keepdims=True) for qi in queries]
    acc = [functools.reduce(jnp.add, [v_refs[n][0] * prs[n] for n in range(qi * per_q, (qi + 1) * per_q)])
           for qi in queries]
    o = [jnp.sum(acc[qi], axis=-1, keepdims=True) + jnp.sum(vn * p_own[qi], axis=-1, keepdims=True)
         for qi in queries]
    for qi in queries:
        out = jnp.where(lane8 == qi, o[qi] / l[qi], out)
    o_ref[0, 0] = out


def _sample_attn(slabs, idx_flat, qt, knt, vnt, bias_own, bias_last, far, ck_slabs, cv_slabs, dec_seq, n_past_blk):
    db = qt.shape[0]
    ppb = BLOCK // PAGE_SIZE
    n_slab = dec_seq * TOP_BLOCKS * ppb
    kern = functools.partial(_sample_attn_kernel, dec_seq=dec_seq, n_past_blk=n_past_blk)

    def slab_spec(n):
        return pl.BlockSpec((1, HEAD_DIM, PAGE_SIZE),
                            lambda b, h, sl, ix: (sl[(b * H_ATTN + h) * n_slab + n], 0, 0))

    small = lambda shape: pl.BlockSpec(shape, lambda b, h, sl, ix: (b, h, 0, 0))
    return pl.pallas_call(
        kern,
        grid_spec=pltpu.PrefetchScalarGridSpec(
            num_scalar_prefetch=2,
            grid=(db, H_ATTN),
            in_specs=[small((1, 1, HEAD_DIM, 8)), small((1, 1, HEAD_DIM, 8)), small((1, 1, HEAD_DIM, 8)),
                      pl.BlockSpec((1, dec_seq, 8), lambda b, h, sl, ix: (h, 0, 0)),
                      pl.BlockSpec((1, dec_seq, ppb, PAGE_SIZE), lambda b, h, sl, ix: (h, 0, 0, 0)),
                      pl.BlockSpec(memory_space=pltpu.SMEM)]
            + [slab_spec(n) for n in range(n_slab)] * 2,
            out_specs=small((1, 1, HEAD_DIM, 8)),
        ),
        out_shape=jax.ShapeDtypeStruct((db, H_ATTN, HEAD_DIM, 8), F32),
        compiler_params=_cparams("parallel", "parallel"),
        name="sample_attn",
    )(slabs, idx_flat, qt, knt, vnt, bias_own, bias_last, far, *([ck_slabs] * n_slab), *([cv_slabs] * n_slab))


def _moba_sample(zqkv_s, cache_k, cache_v, page_table, rel_bias, db, dec_seq, past_len):
    n_past_blk = past_len // BLOCK
    assert past_len % BLOCK == 0 and n_past_blk >= TOP_BLOCKS and n_past_blk <= LANES
    ppb = BLOCK // PAGE_SIZE
    pool = cache_k.shape[1]
    ckt = cache_k[0].transpose(0, 2, 3, 1)
    cvt = cache_v[0].transpose(0, 2, 3, 1)
    z3 = zqkv_s.reshape(db, dec_seq, 3 * D_ATTN)
    q = z3[..., :D_ATTN]
    q8 = jnp.pad(q, ((0, 0), (0, 8 - dec_seq), (0, 0)))
    idx = _sample_gate(page_table, q8, ckt.reshape(pool, H_ATTN * HEAD_DIM, PAGE_SIZE), n_past_blk, dec_seq)
    idx = idx[:, :, :TOP_BLOCKS, :H_ATTN].transpose(0, 3, 1, 2)
    pages = jnp.take_along_axis(page_table[:, None, None, None, :],
                                (idx[..., None] * ppb + jnp.arange(ppb)).reshape(db, H_ATTN, dec_seq, 1, -1),
                                axis=-1)
    slabs = (pages.reshape(db, H_ATTN, -1) * H_ATTN + jnp.arange(H_ATTN)[None, :, None]).reshape(-1).astype(jnp.int32)

    def heads_t(t):
        t = t.reshape(db, dec_seq, H_ATTN, HEAD_DIM).transpose(0, 2, 3, 1)
        return jnp.pad(t, ((0, 0), (0, 0), (0, 0), (0, 8 - dec_seq)))

    bt = _bucket_table(BLOCK + dec_seq)
    rb = rel_bias.astype(F32)
    qi = np.arange(dec_seq)[:, None]
    t8 = np.arange(8)[None, :]
    own_ok = (t8 <= qi) & (t8 < dec_seq)
    bias_own = jnp.where(jnp.asarray(own_ok)[..., None], rb[bt[np.maximum(qi - t8, 0)]], NEG_INF).transpose(2, 0, 1)
    off = np.arange(BLOCK).reshape(ppb, PAGE_SIZE)[None]
    bias_last = rb[bt[BLOCK + qi[:, :, None] - off]].transpose(3, 0, 1, 2)
    o = _sample_attn(slabs, idx.reshape(-1).astype(jnp.int32), heads_t(q), heads_t(z3[..., D_ATTN:2 * D_ATTN]),
                     heads_t(z3[..., 2 * D_ATTN:]), bias_own, bias_last, rb[N_BUCKETS - 1],
                     ckt.reshape(pool * H_ATTN, HEAD_DIM, PAGE_SIZE), cvt.reshape(pool * H_ATTN, HEAD_DIM, PAGE_SIZE),
                     dec_seq, n_past_blk)
    return o[..., :dec_seq].transpose(0, 3, 1, 2).reshape(db * dec_seq, D_ATTN)


def _pair_ones():
    r = np.arange(LANES)
    return jnp.asarray((r[:, None] // HEAD_DIM) == (r[None, :] // HEAD_DIM), BF16)


def _head_sum(x, ones_pair):
    hi = x.astype(BF16)
    lo = (x - hi.astype(F32)).astype(BF16)
    outs = []
    for g in range(N_PAIRS):
        sl = slice(g * LANES, (g + 1) * LANES)
        outs.append(jnp.dot(hi[:, sl], ones_pair, preferred_element_type=F32)
                    + jnp.dot(lo[:, sl], ones_pair, preferred_element_type=F32))
    return jnp.concatenate(outs, axis=-1)


def _rwkv_prep_kernel(z_ref, prev_ref, mu_ref, w0_ref, w2_ref, a0_ref, a2_ref, g2_ref, kk_ref, ka_ref, ones_ref,
                      r_ref, ld_ref, k_ref, v_ref, kap_ref, b_ref, g_ref):
    z = z_ref[...]
    zs = z + (prev_ref[...] - z) * mu_ref[...]
    o = D_RWKV
    r = zs[:, :o]
    k = zs[:, o:2 * o]
    v = zs[:, 2 * o:3 * o]
    lora = zs[:, 3 * o:3 * o + LANES]
    gl = zs[:, 3 * o + LANES:]
    x = w0_ref[...] + _dot(jnp.tanh(lora), w2_ref[...])
    nx = -x
    w = -(jnp.maximum(nx, 0.0) + jnp.log(1.0 + jnp.exp(-jnp.abs(nx)))) - 0.5
    logd = -jnp.exp(w)
    a = _sigmoid(a0_ref[...] + _dot(lora, a2_ref[...]))
    g = _dot(_sigmoid(gl), g2_ref[...])
    kk = k * kk_ref[...]
    nrm = jnp.sqrt(_head_sum(kk * kk, ones_ref[...]))
    kap = kk / jnp.maximum(nrm, 1e-12)
    k2 = k * (1.0 + (a - 1.0) * ka_ref[...])
    for p in range(N_PAIRS):
        sl = slice(p * LANES, (p + 1) * LANES)
        r_ref[p] = r[:, sl]
        ld_ref[p] = logd[:, sl]
        k_ref[p] = k2[:, sl]
        v_ref[p] = v[:, sl]
        kap_ref[p] = kap[:, sl]
        b_ref[p] = (kap * a)[:, sl]
        g_ref[p] = g[:, sl]


def _rwkv_prep(zr, prev, lw, tm):
    n = zr.shape[0]
    row = lambda a: a.reshape(1, -1).astype(F32)
    zpad = jnp.zeros((DECAY_LORA, D_RWKV), F32)
    w2p = jnp.concatenate([lw['w2'], zpad], axis=0).astype(BF16)
    a2p = jnp.concatenate([zpad, lw['a2']], axis=0).astype(BF16)
    full = lambda shape: pl.BlockSpec(shape, lambda i: (0,) * len(shape))
    out_sd = jax.ShapeDtypeStruct((N_PAIRS, n, LANES), F32)
    out_spec = pl.BlockSpec((N_PAIRS, tm, LANES), lambda i: (0, i, 0))
    return pl.pallas_call(
        _rwkv_prep_kernel,
        grid=(n // tm,),
        in_specs=[pl.BlockSpec((tm, C_SHIFT), lambda i: (i, 0)), pl.BlockSpec((tm, C_SHIFT), lambda i: (i, 0)),
                  full((1, C_SHIFT)), full((1, D_RWKV)), full((LANES, D_RWKV)), full((1, D_RWKV)),
                  full((LANES, D_RWKV)), full((GATE_LORA, D_RWKV)), full((1, D_RWKV)), full((1, D_RWKV)),
                  full((LANES, LANES))],
        out_specs=[out_spec] * 7,
        out_shape=[out_sd] * 7,
        compiler_params=_cparams("parallel"),
        name="rwkv_prep",
    )(zr, prev, row(lw['mu_shift']), row(lw['w0']), w2p, row(lw['a0']), a2p, lw['g2'].astype(BF16),
      row(lw['k_k']), row(lw['k_a']), _pair_ones())


def _rwkv_lockstep_kernel(r_ref, ld_ref, k_ref, v_ref, kap_ref, b_ref, g_ref, s0_ref, lg_ref, lb_ref, rk_ref,
                          y_ref, s_ref, *, chunk):
    c = pl.program_id(1)

    @pl.when(c == 0)
    def _():
        s_ref[...] = s0_ref[...]

    heads = range(H_RWKV)
    ti = lax.broadcasted_iota(jnp.int32, (chunk, chunk), 0)
    si = lax.broadcasted_iota(jnp.int32, (chunk, chunk), 1)
    tril_incl = (si <= ti)
    tril_strict = (si < ti)
    ltri = tril_incl.astype(F32)
    eye = (si == ti).astype(F32)
    hsl = [slice((hh % 2) * HEAD_DIM, (hh % 2 + 1) * HEAD_DIM) for hh in heads]

    ld = [ld_ref[p] for p in range(N_PAIRS)]
    cum = [_dot_hi(ltri, x) for x in ld]
    kt, rt, bi, ki, bh, kh, gall = [], [], [], [], [], [], []
    for p in range(N_PAIRS):
        last = cum[p][chunk - 1:chunk, :]
        gam_inv = jnp.exp(-cum[p])
        gam_tail = jnp.exp(last - cum[p])
        kap2, b2, k2 = kap_ref[p], b_ref[p], k_ref[p]
        kt2 = (kap2 * jnp.exp(cum[p] - ld[p])).astype(BF16)
        rt2 = r_ref[p] * jnp.exp(cum[p])
        bi2 = (b2 * gam_inv).astype(BF16)
        ki2 = (k2 * gam_inv).astype(BF16)
        bh2 = (b2 * gam_tail).astype(BF16)
        kh2 = (k2 * gam_tail).astype(BF16)
        g2 = jnp.exp(last)
        for h in range(2):
            sl = hsl[h]
            kt.append(kt2[:, sl]); rt.append(rt2[:, sl]); bi.append(bi2[:, sl]); ki.append(ki2[:, sl])
            bh.append(bh2[:, sl]); kh.append(kh2[:, sl]); gall.append(g2[:, sl])
    vv = [v_ref[hh // 2][:, hsl[hh]].astype(BF16) for hh in heads]
    kr = [jnp.concatenate([kt[hh], rt[hh].astype(BF16)], axis=0) for hh in heads]
    xb = [_dot_nt(kr[hh], bi[hh]) for hh in heads]
    xk = [_dot_nt(kr[hh], ki[hh]) for hh in heads]
    a_ab = [jnp.where(tril_strict, xb[hh][:chunk], 0.0) for hh in heads]
    a_rb = [jnp.where(tril_incl, xb[hh][chunk:], 0.0).astype(BF16) for hh in heads]
    a_ak = [jnp.where(tril_strict, xk[hh][:chunk], 0.0).astype(BF16) for hh in heads]
    a_rk = [jnp.where(tril_incl, xk[hh][chunk:], 0.0).astype(BF16) for hh in heads]
    akv = [_dot(a_ak[hh], vv[hh]) for hh in heads]
    ab = [x.astype(BF16) for x in a_ab]
    tm = [eye - a_ab[hh] for hh in heads]
    pw = [_dot(ab[hh], ab[hh]) for hh in heads]
    n_sq = int(math.log2(chunk)) - 1
    for it in range(n_sq):
        pwb = [x.astype(BF16) for x in pw]
        tm = [tm[hh] + _dot(tm[hh], pwb[hh]) for hh in heads]
        if it < n_sq - 1:
            pw = [_dot(pwb[hh], pwb[hh]) for hh in heads]
    tmb = [x.astype(BF16) for x in tm]
    w = [-_dot(tmb[hh], kt[hh]) for hh in heads]
    u0 = [-_dot(tmb[hh], akv[hh]) for hh in heads]
    wb = [x.astype(BF16) for x in w]
    u0b = [x.astype(BF16) for x in u0]
    pm = [rt[hh] + _dot(a_rb[hh], wb[hh]) for hh in heads]
    y0 = [_dot(a_rb[hh], u0b[hh]) + _dot(a_rk[hh], vv[hh]) for hh in heads]
    s_old = [s_ref[0, hh] for hh in heads]
    sb = [x.astype(BF16) for x in s_old]
    y = [_dot_nt(pm[hh], sb[hh]) + y0[hh] for hh in heads]
    mt = [_dot_tn(bh[hh], wb[hh]) for hh in heads]
    nt = [_dot_tn(jnp.concatenate([u0b[hh], vv[hh]], axis=0), jnp.concatenate([bh[hh], kh[hh]], axis=0))
          for hh in heads]
    for hh in heads:
        s_ref[0, hh] = s_old[hh] * gall[hh] + _dot_nt(sb[hh], mt[hh]) + nt[hh]
    for p in range(N_PAIRS):
        outs = []
        for h in range(2):
            hh = 2 * p + h
            sl = hsl[h]
            mu = jnp.mean(y[hh], axis=-1, keepdims=True)
            var = jnp.mean(jnp.square(y[hh] - mu), axis=-1, keepdims=True)
            yn = (y[hh] - mu) * lax.rsqrt(var + LNX_EPS)
            rk = r_ref[p][:, sl] * k_ref[p][:, sl] * rk_ref[p][:, sl]
            bonus = jnp.sum(rk, axis=-1, keepdims=True) * v_ref[p][:, sl]
            outs.append(yn * lg_ref[p][:, sl] + lb_ref[p][:, sl] + bonus)
        y_ref[p] = jnp.concatenate(outs, axis=-1) * g_ref[p]


def _rwkv_chunks(prep, s0, lw, n_batch, seq, chunk):
    n_chunks = seq // chunk
    pairs = lambda a: a.reshape(N_PAIRS, 1, LANES).astype(F32)
    tile = pl.BlockSpec((N_PAIRS, chunk, LANES), lambda b, c: (0, b * n_chunks + c, 0))
    state = pl.BlockSpec((1, H_RWKV, HEAD_DIM, HEAD_DIM), lambda b, c: (b, 0, 0, 0))
    par = pl.BlockSpec((N_PAIRS, 1, LANES), lambda b, c: (0, 0, 0))
    return pl.pallas_call(
        functools.partial(_rwkv_lockstep_kernel, chunk=chunk),
        grid=(n_batch, n_chunks),
        in_specs=[tile] * 7 + [state, par, par, par],
        out_specs=[tile, state],
        out_shape=[jax.ShapeDtypeStruct((N_PAIRS, n_batch * seq, LANES), F32),
                   jax.ShapeDtypeStruct((n_batch, H_RWKV, HEAD_DIM, HEAD_DIM), F32)],
        compiler_params=_cparams("parallel", "arbitrary"),
        name="rwkv_chunks",
    )(*prep, s0.astype(F32), pairs(lw['lnx_g']), pairs(lw['lnx_b']), pairs(lw['r_k']))


def _rwkv(zr, shift0, s0, lw, n_batch, seq):
    z3 = zr.reshape(n_batch, seq, C_SHIFT)
    prev = jnp.concatenate([shift0[:, None].astype(F32), z3[:, :-1]], axis=1).reshape(n_batch * seq, C_SHIFT)
    prep = _rwkv_prep(zr, prev, lw, min(256, n_batch * seq))
    chunk = CHUNK if seq >= CHUNK else SHORT_CHUNK
    pad = (-seq) % chunk
    if pad:
        prep = [jnp.pad(a.reshape(N_PAIRS, n_batch, seq, LANES), ((0, 0), (0, 0), (0, pad), (0, 0)))
                .reshape(N_PAIRS, n_batch * (seq + pad), LANES) for a in prep]
    y, s_fin = _rwkv_chunks(prep, s0, lw, n_batch, seq + pad, chunk)
    if pad:
        y = y.reshape(N_PAIRS, n_batch, seq + pad, LANES)[:, :, :seq].reshape(N_PAIRS, n_batch * seq, LANES)
    return y, s_fin, z3[:, -1]


def _layer_norm(h, g, b):
    mu = jnp.mean(h, axis=-1, keepdims=True)
    d = h - mu
    var = jnp.mean(d * d, axis=-1, keepdims=True)
    return d * lax.rsqrt(var + LN_EPS) * g + b


def _out_proj_kernel(oa_ref, or_ref, x_ref, wa_ref, wr_ref, g_ref, b_ref, rw_ref, x1_ref, x1b_ref, sc_ref):
    orw = jnp.concatenate([or_ref[p] for p in range(N_PAIRS)], axis=-1)
    mix = _dot(oa_ref[...], wa_ref[...]) + _dot(orw, wr_ref[...])
    x1 = _layer_norm(ALPHA * x_ref[...] + mix, g_ref[...], b_ref[...])
    x1_ref[...] = x1
    x1b_ref[...] = x1.astype(BF16)
    sc_ref[...] = _sigmoid(lax.dot_general(rw_ref[...], x1, (((1,), (1,)), ((), ())),
                                           precision=lax.Precision.HIGHEST, preferred_element_type=F32))


def _out_proj(o_attn, o_rwkv, x, lw, tm):
    n = x.shape[0]
    wo = lw['w_out'].astype(BF16)
    full = lambda shape: pl.BlockSpec(shape, lambda i: (0,) * len(shape))
    return pl.pallas_call(
        _out_proj_kernel,
        grid=(n // tm,),
        in_specs=[pl.BlockSpec((tm, D_ATTN), lambda i: (i, 0)),
                  pl.BlockSpec((N_PAIRS, tm, LANES), lambda i: (0, i, 0)),
                  pl.BlockSpec((tm, D_MODEL), lambda i: (i, 0)),
                  full((D_ATTN, D_MODEL)), full((D_RWKV, D_MODEL)), full((1, D_MODEL)), full((1, D_MODEL)),
                  full((N_EXPERTS, D_MODEL))],
        out_specs=[pl.BlockSpec((tm, D_MODEL), lambda i: (i, 0)), pl.BlockSpec((tm, D_MODEL), lambda i: (i, 0)),
                   pl.BlockSpec((N_EXPERTS, tm), lambda i: (0, i))],
        out_shape=[jax.ShapeDtypeStruct((n, D_MODEL), F32), jax.ShapeDtypeStruct((n, D_MODEL), BF16),
                   jax.ShapeDtypeStruct((N_EXPERTS, n), F32)],
        compiler_params=_cparams("parallel"),
        name="out_proj_ln_router",
    )(o_attn, o_rwkv, x, wo[:D_ATTN], wo[D_ATTN:], lw['ln1_g'].reshape(1, -1), lw['ln1_b'].reshape(1, -1),
      lw['router_w'].astype(F32).T)


def _expert_kernel(be_ref, na_ref, x_ref, wg_ref, wu_ref, wd_ref, after_ref, y_ref, wg_s, wu_s, wd_s):
    del after_ref
    g = pl.program_id(0)
    active = g < na_ref[0]

    @pl.when(active & ((g == 0) | (be_ref[g] != be_ref[jnp.maximum(g - 1, 0)])))
    def _():
        wg_s[...] = wg_ref[0].astype(BF16)
        wu_s[...] = wu_ref[0].astype(BF16)
        wd_s[...] = wd_ref[0].astype(BF16)

    @pl.when(active)
    def _():
        x = x_ref[...]
        hg = jnp.dot(x, wg_s[...], preferred_element_type=F32)
        hu = jnp.dot(x, wu_s[...], preferred_element_type=F32)
        h = hg * _sigmoid(hg) * hu
        y_ref[...] = jnp.dot(h.astype(BF16), wd_s[...], preferred_element_type=F32).astype(y_ref.dtype)

    @pl.when(g >= na_ref[0])
    def _():
        y_ref[...] = jnp.zeros_like(y_ref)


def _expert_blocks(xs, blk_e, n_active, wg, wu, wd, tm, out_dtype, after):
    rows = xs.shape[0]
    n_blocks = rows // tm
    d_e = wg.shape[-1]
    return pl.pallas_call(
        _expert_kernel,
        grid_spec=pltpu.PrefetchScalarGridSpec(
            num_scalar_prefetch=2,
            grid=(n_blocks,),
            in_specs=[pl.BlockSpec((tm, D_MODEL), lambda g, be, na: (g, 0)),
                      pl.BlockSpec((1, D_MODEL, d_e), lambda g, be, na: (be[g], 0, 0)),
                      pl.BlockSpec((1, D_MODEL, d_e), lambda g, be, na: (be[g], 0, 0)),
                      pl.BlockSpec((1, d_e, D_MODEL), lambda g, be, na: (be[g], 0, 0)),
                      pl.BlockSpec(memory_space=pl.ANY)],
            out_specs=pl.BlockSpec((tm, D_MODEL), lambda g, be, na: (g, 0)),
            scratch_shapes=[pltpu.VMEM((D_MODEL, d_e), BF16), pltpu.VMEM((D_MODEL, d_e), BF16),
                            pltpu.VMEM((d_e, D_MODEL), BF16)],
        ),
        out_shape=jax.ShapeDtypeStruct((rows, D_MODEL), out_dtype),
        compiler_params=_cparams("arbitrary"),
        name="expert_blocks",
    )(blk_e, n_active, xs, wg, wu, wd, after)


def _first_max(x, idx, n):
    m = jnp.max(x, axis=0, keepdims=True)
    return m, jnp.min(jnp.where(x == m, idx, n), axis=0, keepdims=True)


def _route_kernel(sc_ref, bias_ref, tri_ref, e_ref, gate_ref, rank_ref, cnt_ref, carry_s):
    i = pl.program_id(0)
    tr = sc_ref.shape[1]
    per_group = N_EXPERTS // N_GROUPS

    @pl.when(i == 0)
    def _():
        carry_s[...] = jnp.zeros_like(carry_s)

    sc = sc_ref[...]
    choice = sc + bias_ref[...]
    in_g = lax.broadcasted_iota(jnp.int32, (per_group, tr), 0)
    g_iota = lax.broadcasted_iota(jnp.int32, (N_GROUPS, tr), 0)
    grp = jnp.zeros((N_GROUPS, tr), F32)
    for gi in range(N_GROUPS):
        cg = choice[gi * per_group:(gi + 1) * per_group]
        m1, i1 = _first_max(cg, in_g, per_group)
        m2 = jnp.max(jnp.where(in_g == i1, NEG_INF, cg), axis=0, keepdims=True)
        grp = jnp.where(g_iota == gi, m1 + m2, grp)
    keep = jnp.zeros((N_GROUPS, tr), F32)
    for _ in range(TOPK_GROUPS):
        _, f = _first_max(grp, g_iota, N_GROUPS)
        hit = g_iota == f
        keep = jnp.where(hit, 1.0, keep)
        grp = jnp.where(hit, NEG_INF, grp)
    masked = jnp.concatenate(
        [jnp.where(keep[gi:gi + 1] > 0.0, choice[gi * per_group:(gi + 1) * per_group], NEG_INF)
         for gi in range(N_GROUPS)], axis=0)
    e_iota = lax.broadcasted_iota(jnp.int32, (N_EXPERTS, tr), 0)
    hits, es, ss = [], [], []
    onehot = jnp.zeros((N_EXPERTS, tr), F32)
    for _ in range(TOP_K):
        _, f = _first_max(masked, e_iota, N_EXPERTS)
        hit = e_iota == f
        hits.append(hit)
        es.append(f)
        ss.append(jnp.sum(jnp.where(hit, sc, 0.0), axis=0, keepdims=True))
        onehot = jnp.where(hit, 1.0, onehot)
        masked = jnp.where(hit, NEG_INF, masked)
    total = ss[0]
    for s in ss[1:]:
        total = total + s
    e_ref[...] = jnp.concatenate(es, axis=0)
    gate_ref[...] = jnp.concatenate([s / total * ROUTED_SCALE for s in ss], axis=0)
    before = carry_s[...] + jnp.dot(onehot.astype(BF16), tri_ref[...], preferred_element_type=F32)
    rank_ref[...] = jnp.concatenate(
        [jnp.sum(jnp.where(h, before, 0.0), axis=0, keepdims=True) for h in hits], axis=0).astype(jnp.int32)
    carry = carry_s[...] + jnp.sum(onehot, axis=1, keepdims=True)
    carry_s[...] = carry
    cnt_ref[...] = jnp.broadcast_to(carry, cnt_ref.shape).astype(jnp.int32)


def _route(scores_t, router_bias):
    n = scores_t.shape[1]
    tr = min(256, n)
    tri = jnp.asarray(np.arange(tr)[:, None] < np.arange(tr)[None, :], BF16)
    col = pl.BlockSpec((TOP_K, tr), lambda i: (0, i))
    top_e, gate, rank, counts = pl.pallas_call(
        _route_kernel,
        grid=(n // tr,),
        in_specs=[pl.BlockSpec((N_EXPERTS, tr), lambda i: (0, i)),
                  pl.BlockSpec((N_EXPERTS, 1), lambda i: (0, 0)),
                  pl.BlockSpec((tr, tr), lambda i: (0, 0))],
        out_specs=[col, col, col, pl.BlockSpec((N_EXPERTS, LANES), lambda i: (0, 0))],
        out_shape=[jax.ShapeDtypeStruct((TOP_K, n), jnp.int32), jax.ShapeDtypeStruct((TOP_K, n), F32),
                   jax.ShapeDtypeStruct((TOP_K, n), jnp.int32), jax.ShapeDtypeStruct((N_EXPERTS, LANES), jnp.int32)],
        scratch_shapes=[pltpu.VMEM((N_EXPERTS, 1), F32)],
        compiler_params=_cparams("arbitrary"),
        name="route",
    )(scores_t, router_bias.astype(F32).reshape(N_EXPERTS, 1), tri)
    counts = counts[:, 0]
    pcounts = (counts + MOE_TM - 1) // MOE_TM * MOE_TM
    pends = jnp.cumsum(pcounts)
    starts = pends - pcounts
    dest = rank
    for e in range(N_EXPERTS):
        dest = dest + jnp.where(top_e == e, starts[e], 0)
    n_blocks = -(-(n * TOP_K) // MOE_TM) + N_EXPERTS
    slot_tok = jnp.zeros((n_blocks * MOE_TM,), jnp.int32).at[dest.reshape(-1)].set(
        jnp.tile(jnp.arange(n, dtype=jnp.int32), TOP_K))
    blk_start = jnp.arange(n_blocks, dtype=jnp.int32) * MOE_TM
    blk_e = jnp.minimum(jnp.sum((pends[None, :] <= blk_start[:, None]).astype(jnp.int32), axis=1), N_EXPERTS - 1)
    n_active = (pends[-1] // MOE_TM).astype(jnp.int32).reshape(1)
    return gate.T, dest.astype(jnp.int32), slot_tok, blk_e, n_active


def _combine_kernel(x1_ref, sh_ref, ys_ref, gate_ref, g_ref, b_ref, o_ref):
    gate = gate_ref[...]
    f = sh_ref[...]
    for j in range(TOP_K):
        f = f + ys_ref[j].astype(F32) * gate[:, j:j + 1]
    o_ref[...] = _layer_norm(ALPHA * x1_ref[...] + f, g_ref[...], b_ref[...])


def _combine(x1, shared, ysg, gate, lw, tm):
    n = x1.shape[0]
    row = pl.BlockSpec((tm, D_MODEL), lambda i: (i, 0))
    par = pl.BlockSpec((1, D_MODEL), lambda i: (0, 0))
    return pl.pallas_call(
        _combine_kernel,
        grid=(n // tm,),
        in_specs=[row, row, pl.BlockSpec((TOP_K, tm, D_MODEL), lambda i: (0, i, 0)),
                  pl.BlockSpec((tm, TOP_K), lambda i: (i, 0)), par, par],
        out_specs=row,
        out_shape=jax.ShapeDtypeStruct((n, D_MODEL), F32),
        compiler_params=_cparams("parallel"),
        name="moe_combine_ln",
    )(x1, shared, ysg, gate, lw['ln2_g'].reshape(1, -1), lw['ln2_b'].reshape(1, -1))


def _moe(x1, x1b, scores_t, lw, ew, after):
    n = x1.shape[0]
    gate, dest, slot_tok, blk_e, n_active = _route(scores_t, lw['router_bias'])
    xs = x1b[slot_tok]
    ys = _expert_blocks(xs, blk_e, n_active, ew['e_gate'], ew['e_up'], ew['e_down'], MOE_TM, BF16, after)
    tm_s = min(MOE_TM, n)
    shared = _expert_blocks(x1b, jnp.zeros((n // tm_s,), jnp.int32), jnp.full((1,), n // tm_s, jnp.int32),
                            ew['s_gate'], ew['s_up'], ew['s_down'], tm_s, F32, x1)
    ysg = ys[dest.reshape(-1)].reshape(TOP_K, n, D_MODEL)
    return _combine(x1, shared, ysg, gate, lw, min(128, n))


def _layer(x, attend, shift0, s0, lw, ew, w_in_b, n_batch, seq, after=None):
    n = n_batch * seq
    x2 = x.reshape(n, D_MODEL)
    xb = x2.astype(BF16)
    tm = min(512, n)
    zqkv = _matmul(xb, w_in_b[:, :3 * D_ATTN], tm, 768)
    zr = _matmul(xb, w_in_b[:, 3 * D_ATTN:], tm, C_SHIFT // 2)
    o_attn = attend(zqkv)
    o_rwkv, s_new, shift_new = _rwkv(zr, shift0, s0, lw, n_batch, seq)
    x1, x1b, scores = _out_proj(o_attn, o_rwkv, x2, lw, min(256, n))
    y = _moe(x1, x1b, scores, lw, ew, x1 if after is None else after)
    k = zqkv[:, D_ATTN:2 * D_ATTN].reshape(n_batch, seq, H_ATTN, HEAD_DIM)
    v = zqkv[:, 2 * D_ATTN:].reshape(n_batch, seq, H_ATTN, HEAD_DIM)
    return y.reshape(n_batch, seq, D_MODEL), k, v, s_new, shift_new


def kernel(x_prompt, x_sample, cache_k, cache_v, state_wkv, state_shift, page_table, rel_bias, w_in, mu_shift, w0, w2,
           a0, a2, g2, k_k, k_a, r_k, lnx_g, lnx_b, w_out, ln1_g, ln1_b, router_w, router_bias, e_gate, e_up, e_down,
           s_gate, s_up, s_down, ln2_g, ln2_b):
    assert w_in.shape[0] == DEPTH == 1
    n_b, seq, _ = x_prompt.shape
    db, dec_seq, _ = x_sample.shape
    past_len = page_table.shape[1] * PAGE_SIZE
    lw = {'mu_shift': mu_shift[0], 'w0': w0[0], 'w2': w2[0], 'a0': a0[0], 'a2': a2[0], 'g2': g2[0], 'k_k': k_k[0],
          'k_a': k_a[0], 'r_k': r_k[0], 'lnx_g': lnx_g[0], 'lnx_b': lnx_b[0], 'w_out': w_out[0], 'ln1_g': ln1_g[0],
          'ln1_b': ln1_b[0], 'router_w': router_w[0], 'router_bias': router_bias[0], 'ln2_g': ln2_g[0],
          'ln2_b': ln2_b[0]}
    ew = {'e_gate': e_gate[0], 'e_up': e_up[0], 'e_down': e_down[0],
          's_gate': s_gate, 's_up': s_up, 's_down': s_down}
    w_in_b = w_in[0].astype(BF16)

    y_s, k_s, v_s, w_s, s_s = _layer(
        x_sample, lambda z: _moba_sample(z, cache_k, cache_v, page_table, rel_bias, db, dec_seq, past_len),
        state_shift[0], state_wkv[0], lw, ew, w_in_b, db, dec_seq)
    y_p, k_p, v_p, w_p, s_p = _layer(
        x_prompt, lambda z: _moba_prompt(z, rel_bias, n_b, seq),
        jnp.zeros((n_b, C_SHIFT), F32), jnp.zeros((n_b, H_RWKV, HEAD_DIM, HEAD_DIM), F32), lw, ew, w_in_b, n_b, seq,
        after=y_s)
    return (y_p, y_s, k_p[None], v_p[None], w_p[None], s_p[None], k_s[None], v_s[None], w_s[None], s_s[None])
```

```python
import functools
import math

import numpy as np
import jax
import jax.numpy as jnp
from jax import lax
from jax.experimental import pallas as pl
from jax.experimental.pallas import tpu as pltpu

F32 = jnp.float32
BF16 = jnp.bfloat16

D_MODEL = 2048
HEAD_DIM = 64
D_ATTN = 1024
D_RWKV = 1024
H_ATTN = 16
H_RWKV = 16
N_PAIRS = 8
LANES = 128
BLOCK = 256
TOP_BLOCKS = 3
PAGE_SIZE = 128
N_BUCKETS = 32
MAX_DISTANCE = 128
DECAY_LORA = 64
AAA_LORA = 64
GATE_LORA = 128
C_SHIFT = 3 * D_RWKV + DECAY_LORA + AAA_LORA + GATE_LORA
LNX_EPS = 64e-5
LN_EPS = 1e-5
N_EXPERTS = 64
TOP_K = 8
N_GROUPS = 8
TOPK_GROUPS = 4
D_EXPERT = 512
ROUTED_SCALE = 2.5
DEPTH = 1
ALPHA = (2.0 * DEPTH) ** 0.25
CHUNK = 64
MOE_TM = 256
MOE_TOKEN_PARTS = 2
SHORT_CHUNK = 16
FAR_BLOCKS_PER_STEP = 2
VMEM_LIMIT = 56 * 1024 * 1024
NEG_INF = float("-inf")
LOG2E = math.log2(math.e)


def _cparams(*sem):
    return pltpu.CompilerParams(dimension_semantics=sem, vmem_limit_bytes=VMEM_LIMIT)


def _dot(a, b):
    return jnp.dot(a.astype(BF16), b.astype(BF16), preferred_element_type=F32)


def _dot_nt(a, b):
    return lax.dot_general(a.astype(BF16), b.astype(BF16), (((1,), (1,)), ((), ())), preferred_element_type=F32)


def _dot_tn(a, b):
    return lax.dot_general(a.astype(BF16), b.astype(BF16), (((0,), (0,)), ((), ())), preferred_element_type=F32)


def _dot_hi(a, b):
    return jnp.dot(a, b, precision=lax.Precision.HIGHEST, preferred_element_type=F32)


def _sigmoid(x):
    return 1.0 / (1.0 + jnp.exp(-x))


def _mm_kernel(x_ref, w_ref, o_ref):
    o_ref[...] = jnp.dot(x_ref[...], w_ref[...], preferred_element_type=F32)


def _matmul(x, w, tm, tn):
    m, k = x.shape
    n = w.shape[1]
    return pl.pallas_call(
        _mm_kernel,
        grid=(m // tm, n // tn),
        in_specs=[pl.BlockSpec((tm, k), lambda i, j: (i, 0)),
                  pl.BlockSpec((k, tn), lambda i, j: (0, j))],
        out_specs=pl.BlockSpec((tm, tn), lambda i, j: (i, j)),
        out_shape=jax.ShapeDtypeStruct((m, n), F32),
        compiler_params=_cparams("parallel", "parallel"),
        name="in_proj",
    )(x, w)


def _bucket_table(max_dist):
    n = np.arange(max_dist + 1)
    max_exact = N_BUCKETS // 2
    nf = np.maximum(n, 1).astype(np.float64)
    large = max_exact + (np.log(nf / max_exact) / math.log(MAX_DISTANCE / max_exact)
                         * (N_BUCKETS - max_exact)).astype(np.int64)
    large = np.minimum(large, N_BUCKETS - 1)
    return np.where(n < max_exact, n, large).astype(np.int32)


def _relative_bias_log2(rel_bias):
    rb = rel_bias.astype(F32)
    return (rb - rb[N_BUCKETS - 1:]) * LOG2E


def _prompt_bucket_tiles():
    kpos = np.arange(BLOCK)[:, None]
    qpos = np.arange(BLOCK)[None, :]
    bt = _bucket_table(2 * BLOCK)
    d_own = qpos - kpos
    own = np.where(d_own >= 0, bt[np.maximum(d_own, 0)], -1)
    return jnp.asarray(np.stack([own, bt[BLOCK + qpos - kpos]]).astype(np.int32))


def _moba_prompt_kernel(rb_ref, q_ref, k_ref, v_ref, bkt_ref, o_ref, kb_s, vt_s, km_s, sel_s, bias_s, sa_s, sb_s,
                        *, n_blk):
    p = pl.program_id(1)
    i = pl.program_id(2)

    @pl.when(i == 0)
    def _():
        for kind in range(2):
            bk = bkt_ref[kind]
            for h in range(2):
                t = jnp.zeros((BLOCK, BLOCK), F32)
                for b in range(N_BUCKETS):
                    t = jnp.where(bk == b, rb_ref[b * H_ATTN + 2 * p + h], t)
                bias_s[h, kind] = jnp.where(bk < 0, NEG_INF, t)

        def fill(j, c):
            r0 = pl.multiple_of(j * BLOCK, BLOCK)
            kblk = k_ref[pl.ds(r0, BLOCK), :]
            vblk = v_ref[pl.ds(r0, BLOCK), :]
            kb_s[pl.ds(r0, BLOCK), :] = kblk.astype(BF16)
            vt_s[j] = vblk.T.astype(BF16)
            km_s[pl.ds(j, 1), :] = jnp.sum(kblk, axis=0, keepdims=True) * (1.0 / BLOCK)
            return c
        lax.fori_loop(0, n_blk, fill, 0)

    qt = q_ref[...].T
    row = lax.broadcasted_iota(jnp.int32, (LANES, BLOCK), 0)
    head0 = row < HEAD_DIM
    qts = [jnp.where(head0, qt, 0.0), jnp.where(head0, 0.0, qt)]
    qtb = [(x * (HEAD_DIM ** -0.5 * LOG2E)).astype(BF16) for x in qts]

    blk = lax.broadcasted_iota(jnp.int32, (n_blk, BLOCK), 0)
    valid = blk < i
    km = km_s[...]
    for h in range(2):
        g = jnp.where(valid, _dot_hi(km, qts[h]), NEG_INF)
        sel = jnp.zeros((n_blk, BLOCK), F32)
        for _ in range(TOP_BLOCKS):
            m = jnp.max(g, axis=0, keepdims=True)
            first = jnp.min(jnp.where(g == m, blk, n_blk), axis=0, keepdims=True)
            hit = blk == first
            sel = jnp.where(hit & valid, 1.0, sel)
            g = jnp.where(hit, NEG_INF, g)
        sel_s[h] = sel

    def scores(j, h):
        r0 = pl.multiple_of(j * BLOCK, BLOCK)
        return jnp.dot(kb_s[pl.ds(r0, BLOCK), :], qtb[h], preferred_element_type=F32)

    def pv(j, pb0, pb1):
        vt = vt_s[j]
        return jnp.concatenate(
            [jnp.dot(vt[:HEAD_DIM], pb0, preferred_element_type=F32),
             jnp.dot(vt[HEAD_DIM:], pb1, preferred_element_type=F32)], axis=0)

    def update(blocks, carry):
        m0, m1, l0, l1, acc = carry
        new = []
        for h, (m, l) in enumerate(((m0, l0), (m1, l1))):
            mn = m
            for blk_ in blocks:
                mn = jnp.where(blk_[3 + h], jnp.maximum(mn, jnp.max(blk_[1 + h], axis=0, keepdims=True)), mn)
            a = jnp.exp2(m - mn)
            ln = a * l
            pbs = []
            for blk_ in blocks:
                pr = jnp.exp2(blk_[1 + h] - jnp.where(blk_[3 + h], mn, jnp.inf))
                ln = ln + jnp.sum(pr, axis=0, keepdims=True)
                pbs.append(pr.astype(BF16))
            new.append((mn, a, ln, pbs))
        acc = jnp.where(head0, new[0][1], new[1][1]) * acc
        for n, blk_ in enumerate(blocks):
            acc = acc + pv(blk_[0], new[0][3][n], new[1][3][n])
        return new[0][0], new[1][0], new[0][2], new[1][2], acc

    jp = jnp.maximum(i - 1, 0)
    has_prev = i > 0
    keep0 = (sel_s[0, pl.ds(jp, 1), :] > 0.0) & has_prev
    keep1 = (sel_s[1, pl.ds(jp, 1), :] > 0.0) & has_prev
    always = jnp.full((1, BLOCK), True)
    row_init = jnp.full((1, BLOCK), NEG_INF, F32)
    carry = (row_init, row_init, jnp.zeros((1, BLOCK), F32), jnp.zeros((1, BLOCK), F32),
             jnp.zeros((LANES, BLOCK), F32))
    n_far = jnp.maximum(i - 1, 0)

    def issue_scores(t, dst):
        for u in range(FAR_BLOCKS_PER_STEP):
            jc = jnp.minimum(t * FAR_BLOCKS_PER_STEP + u, n_blk - 1)
            for h in range(2):
                dst[u, h] = scores(jc, h)

    def consume(t, src, carry):
        blocks = []
        for u in range(FAR_BLOCKS_PER_STEP):
            j = t * FAR_BLOCKS_PER_STEP + u
            in_range = j < n_far
            jc = jnp.minimum(j, n_blk - 1)
            blocks.append((jc, src[u, 0], src[u, 1],
                           (sel_s[0, pl.ds(jc, 1), :] > 0.0) & in_range,
                           (sel_s[1, pl.ds(jc, 1), :] > 0.0) & in_range))
        return update(blocks, carry)

    issue_scores(0, sa_s)
    carry = update([(i, scores(i, 0) + bias_s[0, 0], scores(i, 1) + bias_s[1, 0], always, always),
                    (jp, scores(jp, 0) + bias_s[0, 1], scores(jp, 1) + bias_s[1, 1], keep0, keep1)], carry)

    def far_body(it, carry):
        issue_scores(2 * it + 1, sb_s)
        carry = consume(2 * it, sa_s, carry)
        issue_scores(2 * it + 2, sa_s)
        return consume(2 * it + 1, sb_s, carry)

    n_iter = (n_far + 2 * FAR_BLOCKS_PER_STEP - 1) // (2 * FAR_BLOCKS_PER_STEP)
    m0, m1, l0, l1, acc = lax.fori_loop(0, n_iter, far_body, carry)
    ot = acc / jnp.where(head0, l0, l1)
    o_ref[...] = ot.T


def _moba_prompt(zqkv, rel_bias, n_batch, seq):
    n_blk = seq // BLOCK
    kern = functools.partial(_moba_prompt_kernel, n_blk=n_blk)
    stage = pltpu.VMEM((FAR_BLOCKS_PER_STEP, 2, BLOCK, BLOCK), F32)
    return pl.pallas_call(
        kern,
        grid=(n_batch, N_PAIRS, n_blk),
        in_specs=[
            pl.BlockSpec(memory_space=pltpu.SMEM),
            pl.BlockSpec((BLOCK, LANES), lambda b, p, i: (b * n_blk + i, p)),
            pl.BlockSpec((seq, LANES), lambda b, p, i: (b, N_PAIRS + p)),
            pl.BlockSpec((seq, LANES), lambda b, p, i: (b, 2 * N_PAIRS + p)),
            pl.BlockSpec((2, BLOCK, BLOCK), lambda b, p, i: (0, 0, 0)),
        ],
        out_specs=pl.BlockSpec((BLOCK, LANES), lambda b, p, i: (b * n_blk + i, p)),
        out_shape=jax.ShapeDtypeStruct((n_batch * seq, D_ATTN), F32),
        scratch_shapes=[pltpu.VMEM((seq, LANES), BF16),
                        pltpu.VMEM((n_blk, LANES, BLOCK), BF16),
                        pltpu.VMEM((n_blk, LANES), F32),
                        pltpu.VMEM((2, n_blk, BLOCK), F32),
                        pltpu.VMEM((2, 2, BLOCK, BLOCK), F32),
                        stage, stage],
        compiler_params=_cparams("parallel", "parallel", "arbitrary"),
        name="moba_prompt",
    )(_relative_bias_log2(rel_bias).reshape(-1), zqkv, zqkv, zqkv, _prompt_bucket_tiles())


PAGES_PER_STEP = 8


def _split_bf16(x):
    hi = x.astype(BF16)
    return hi, (x - hi.astype(F32)).astype(BF16)


def _sample_gate_kernel(pt_ref, q_ref, seg_ref, *refs, n_past_blk, n_steps, dec_seq):
    page_refs = refs[:PAGES_PER_STEP]
    idx_ref = refs[PAGES_PER_STEP]
    sum_s = refs[PAGES_PER_STEP + 1]
    s = pl.program_id(1)
    ppb = BLOCK // PAGE_SIZE
    ones = jnp.ones((16, PAGE_SIZE), BF16)
    for bl in range(PAGES_PER_STEP // ppb):
        x = page_refs[ppb * bl][0]
        for e in range(1, ppb):
            x = x + page_refs[ppb * bl + e][0]
        hi, lo = _split_bf16(x)
        tot = _dot_nt(ones, hi) + _dot_nt(ones, lo)
        sum_s[pl.ds(s * (PAGES_PER_STEP // ppb) + bl, 1), :] = tot[:1]

    @pl.when(s == n_steps - 1)
    def _():
        blk = lax.broadcasted_iota(jnp.int32, (n_past_blk, LANES), 0)
        ksum = sum_s[...]
        for qi in range(dec_seq):
            hi, lo = _split_bf16(ksum * q_ref[0, qi:qi + 1, :])
            g = _dot(hi, seg_ref[...]) + _dot(lo, seg_ref[...])
            rows = []
            for _ in range(TOP_BLOCKS):
                m = jnp.max(g, axis=0, keepdims=True)
                first = jnp.min(jnp.where(g == m, blk, n_past_blk), axis=0, keepdims=True)
                rows.append(first)
                g = jnp.where(blk == first, NEG_INF, g)
            rows.append(jnp.zeros((8 - TOP_BLOCKS, LANES), jnp.int32))
            idx_ref[0, qi] = jnp.concatenate(rows, axis=0)


def _sample_gate(page_table, q8, cache_k_pages, n_past_blk, dec_seq):
    db = q8.shape[0]
    n_pages = page_table.shape[1]
    n_steps = n_pages // PAGES_PER_STEP
    kern = functools.partial(_sample_gate_kernel, n_past_blk=n_past_blk, n_steps=n_steps, dec_seq=dec_seq)
    col = np.arange(H_ATTN * HEAD_DIM)[:, None] // HEAD_DIM
    seg = jnp.asarray(col == np.arange(LANES)[None, :], BF16)

    def page_spec(r):
        return pl.BlockSpec((1, H_ATTN * HEAD_DIM, PAGE_SIZE),
                            lambda b, s, pt: (pt[b, s * PAGES_PER_STEP + r], 0, 0))

    return pl.pallas_call(
        kern,
        grid_spec=pltpu.PrefetchScalarGridSpec(
            num_scalar_prefetch=1,
            grid=(db, n_steps),
            in_specs=[pl.BlockSpec((1, 8, H_ATTN * HEAD_DIM), lambda b, s, pt: (b, 0, 0)),
                      pl.BlockSpec((H_ATTN * HEAD_DIM, LANES), lambda b, s, pt: (0, 0))]
            + [page_spec(r) for r in range(PAGES_PER_STEP)],
            out_specs=pl.BlockSpec((1, dec_seq, 8, LANES), lambda b, s, pt: (b, 0, 0, 0)),
            scratch_shapes=[pltpu.VMEM((n_past_blk, H_ATTN * HEAD_DIM), F32)],
        ),
        out_shape=jax.ShapeDtypeStruct((db, dec_seq, 8, LANES), jnp.int32),
        compiler_params=_cparams("parallel", "arbitrary"),
        name="sample_gate",
    )(page_table, q8, seg, *([cache_k_pages] * PAGES_PER_STEP))


def _sample_attn_kernel(slab_ref, idx_ref, qt_ref, kn_ref, vn_ref, bown_ref, blast_ref, far_ref, *refs,
                        dec_seq, n_past_blk):
    n_slab = dec_seq * TOP_BLOCKS * (BLOCK // PAGE_SIZE)
    k_refs = refs[:n_slab]
    v_refs = refs[n_slab:2 * n_slab]
    o_ref = refs[2 * n_slab]
    b = pl.program_id(0)
    h = pl.program_id(1)
    ppb = BLOCK // PAGE_SIZE
    lane8 = lax.broadcasted_iota(jnp.int32, (HEAD_DIM, 8), 1)
    out = jnp.zeros((HEAD_DIM, 8), F32)
    kn = kn_ref[0, 0]
    vn = vn_ref[0, 0]
    far = far_ref[h]
    per_q = TOP_BLOCKS * ppb
    queries = range(dec_seq)
    qcs = [qt_ref[0, 0][:, qi:qi + 1] * (HEAD_DIM ** -0.5) for qi in queries]
    rows = []
    for qi in queries:
        for t in range(TOP_BLOCKS):
            sel_blk = idx_ref[((b * H_ATTN + h) * dec_seq + qi) * TOP_BLOCKS + t]
            is_last = sel_blk == n_past_blk - 1
            for e in range(ppb):
                slab = k_refs[qi * per_q + t * ppb + e][0]
                srow = jnp.sum(slab * qcs[qi], axis=0, keepdims=True)
                rows.append(srow + jnp.where(is_last, blast_ref[0, qi, e:e + 1, :], far))
    s_own = [jnp.sum(kn * qcs[qi], axis=0, keepdims=True) + bown_ref[0, qi:qi + 1, :] for qi in queries]
    row_max = [functools.reduce(jnp.maximum, rows[qi * per_q:(qi + 1) * per_q]) for qi in queries]
    m = [jnp.maximum(jnp.max(row_max[qi], axis=-1, keepdims=True), jnp.max(s_own[qi], axis=-1, keepdims=True))
         for qi in queries]
    p_own = [jnp.exp(s_own[qi] - m[qi]) for qi in queries]
    prs = [jnp.exp(rows[n] - m[n // per_q]) for n in range(len(rows))]
    l = [jnp.sum(functools.reduce(jnp.add, prs[qi * per_q:(qi + 1) * per_q]), axis=-1, keepdims=True)
         + jnp.sum(p_own[qi], axis=-1, keepdims=True) for qi in queries]
    acc = [functools.reduce(jnp.add, [v_refs[n][0] * prs[n] for n in range(qi * per_q, (qi + 1) * per_q)])
           for qi in queries]
    o = [jnp.sum(acc[qi], axis=-1, keepdims=True) + jnp.sum(vn * p_own[qi], axis=-1, keepdims=True)
         for qi in queries]
    for qi in queries:
        out = jnp.where(lane8 == qi, o[qi] / l[qi], out)
    o_ref[0, 0] = out


def _sample_attn(slabs, idx_flat, qt, knt, vnt, bias_own, bias_last, far, ck_slabs, cv_slabs, dec_seq, n_past_blk):
    db = qt.shape[0]
    ppb = BLOCK // PAGE_SIZE
    n_slab = dec_seq * TOP_BLOCKS * ppb
    kern = functools.partial(_sample_attn_kernel, dec_seq=dec_seq, n_past_blk=n_past_blk)

    def slab_spec(n):
        return pl.BlockSpec((1, HEAD_DIM, PAGE_SIZE),
                            lambda b, h, sl, ix: (sl[(b * H_ATTN + h) * n_slab + n], 0, 0))

    small = lambda shape: pl.BlockSpec(shape, lambda b, h, sl, ix: (b, h, 0, 0))
    return pl.pallas_call(
        kern,
        grid_spec=pltpu.PrefetchScalarGridSpec(
            num_scalar_prefetch=2,
            grid=(db, H_ATTN),
            in_specs=[small((1, 1, HEAD_DIM, 8)), small((1, 1, HEAD_DIM, 8)), small((1, 1, HEAD_DIM, 8)),
                      pl.BlockSpec((1, dec_seq, 8), lambda b, h, sl, ix: (h, 0, 0)),
                      pl.BlockSpec((1, dec_seq, ppb, PAGE_SIZE), lambda b, h, sl, ix: (h, 0, 0, 0)),
                      pl.BlockSpec(memory_space=pltpu.SMEM)]
            + [slab_spec(n) for n in range(n_slab)] * 2,
            out_specs=small((1, 1, HEAD_DIM, 8)),
        ),
        out_shape=jax.ShapeDtypeStruct((db, H_ATTN, HEAD_DIM, 8), F32),
        compiler_params=_cparams("parallel", "parallel"),
        name="sample_attn",
    )(slabs, idx_flat, qt, knt, vnt, bias_own, bias_last, far, *([ck_slabs] * n_slab), *([cv_slabs] * n_slab))


def _sample_select(zqkv_s, cache_k, page_table, db, dec_seq, past_len):
    n_past_blk = past_len // BLOCK
    assert past_len % BLOCK == 0 and n_past_blk >= TOP_BLOCKS
    pool = cache_k.shape[1]
    ckt = cache_k[0].transpose(0, 2, 3, 1)
    q = zqkv_s.reshape(db, dec_seq, 3 * D_ATTN)[..., :D_ATTN]
    q8 = jnp.pad(q, ((0, 0), (0, 8 - dec_seq), (0, 0)))
    idx = _sample_gate(page_table, q8, ckt.reshape(pool, H_ATTN * HEAD_DIM, PAGE_SIZE), n_past_blk, dec_seq)
    return idx[:, :, :TOP_BLOCKS, :H_ATTN].transpose(0, 3, 1, 2)


def _sample_attend(zqkv_s, idx, cache_k, cache_v, page_table, rel_bias, db, dec_seq, past_len, after):
    n_past_blk = past_len // BLOCK
    ppb = BLOCK // PAGE_SIZE
    pool = cache_k.shape[1]
    ckt = cache_k[0].transpose(0, 2, 3, 1)
    cvt = cache_v[0].transpose(0, 2, 3, 1)
    z3 = zqkv_s.reshape(db, dec_seq, 3 * D_ATTN)
    q = z3[..., :D_ATTN]
    pages = jnp.take_along_axis(page_table[:, None, None, None, :],
                                (idx[..., None] * ppb + jnp.arange(ppb)).reshape(db, H_ATTN, dec_seq, 1, -1),
                                axis=-1)
    slabs = (pages.reshape(db, H_ATTN, -1) * H_ATTN + jnp.arange(H_ATTN)[None, :, None]).reshape(-1).astype(jnp.int32)
    slabs = _order_after(slabs.reshape(-1, LANES), after).reshape(-1)

    def heads_t(t):
        t = t.reshape(db, dec_seq, H_ATTN, HEAD_DIM).transpose(0, 2, 3, 1)
        return jnp.pad(t, ((0, 0), (0, 0), (0, 0), (0, 8 - dec_seq)))

    bt = _bucket_table(BLOCK + dec_seq)
    rb = rel_bias.astype(F32)
    qi = np.arange(dec_seq)[:, None]
    t8 = np.arange(8)[None, :]
    own_ok = (t8 <= qi) & (t8 < dec_seq)
    bias_own = jnp.where(jnp.asarray(own_ok)[..., None], rb[bt[np.maximum(qi - t8, 0)]], NEG_INF).transpose(2, 0, 1)
    off = np.arange(BLOCK).reshape(ppb, PAGE_SIZE)[None]
    bias_last = rb[bt[BLOCK + qi[:, :, None] - off]].transpose(3, 0, 1, 2)
    o = _sample_attn(slabs, idx.reshape(-1).astype(jnp.int32), heads_t(q), heads_t(z3[..., D_ATTN:2 * D_ATTN]),
                     heads_t(z3[..., 2 * D_ATTN:]), bias_own, bias_last, rb[N_BUCKETS - 1],
                     ckt.reshape(pool * H_ATTN, HEAD_DIM, PAGE_SIZE), cvt.reshape(pool * H_ATTN, HEAD_DIM, PAGE_SIZE),
                     dec_seq, n_past_blk)
    return o[..., :dec_seq].transpose(0, 3, 1, 2).reshape(db * dec_seq, D_ATTN)


def _pair_ones():
    r = np.arange(LANES)
    return jnp.asarray((r[:, None] // HEAD_DIM) == (r[None, :] // HEAD_DIM), BF16)


def _head_sum(x, ones_pair):
    hi = x.astype(BF16)
    lo = (x - hi.astype(F32)).astype(BF16)
    outs = []
    for g in range(N_PAIRS):
        sl = slice(g * LANES, (g + 1) * LANES)
        outs.append(jnp.dot(hi[:, sl], ones_pair, preferred_element_type=F32)
                    + jnp.dot(lo[:, sl], ones_pair, preferred_element_type=F32))
    return jnp.concatenate(outs, axis=-1)


def _rwkv_prep_kernel(z_ref, prev_ref, mu_ref, w0_ref, w2_ref, a0_ref, a2_ref, g2_ref, kk_ref, ka_ref, ones_ref,
                      r_ref, ld_ref, k_ref, v_ref, kap_ref, b_ref, g_ref):
    z = z_ref[...]
    zs = z + (prev_ref[...] - z) * mu_ref[...]
    o = D_RWKV
    r = zs[:, :o]
    k = zs[:, o:2 * o]
    v = zs[:, 2 * o:3 * o]
    lora = zs[:, 3 * o:3 * o + LANES]
    gl = zs[:, 3 * o + LANES:]
    x = w0_ref[...] + _dot(jnp.tanh(lora), w2_ref[...])
    nx = -x
    w = -(jnp.maximum(nx, 0.0) + jnp.log(1.0 + jnp.exp(-jnp.abs(nx)))) - 0.5
    logd = -jnp.exp(w)
    a = _sigmoid(a0_ref[...] + _dot(lora, a2_ref[...]))
    g = _dot(_sigmoid(gl), g2_ref[...])
    kk = k * kk_ref[...]
    nrm = jnp.sqrt(_head_sum(kk * kk, ones_ref[...]))
    kap = kk / jnp.maximum(nrm, 1e-12)
    k2 = k * (1.0 + (a - 1.0) * ka_ref[...])
    for p in range(N_PAIRS):
        sl = slice(p * LANES, (p + 1) * LANES)
        r_ref[p] = r[:, sl]
        ld_ref[p] = logd[:, sl]
        k_ref[p] = k2[:, sl]
        v_ref[p] = v[:, sl]
        kap_ref[p] = kap[:, sl]
        b_ref[p] = (kap * a)[:, sl]
        g_ref[p] = g[:, sl]


def _rwkv_prep(zr, prev, lw, tm):
    n = zr.shape[0]
    row = lambda a: a.reshape(1, -1).astype(F32)
    zpad = jnp.zeros((DECAY_LORA, D_RWKV), F32)
    w2p = jnp.concatenate([lw['w2'], zpad], axis=0).astype(BF16)
    a2p = jnp.concatenate([zpad, lw['a2']], axis=0).astype(BF16)
    full = lambda shape: pl.BlockSpec(shape, lambda i: (0,) * len(shape))
    out_sd = jax.ShapeDtypeStruct((N_PAIRS, n, LANES), F32)
    out_spec = pl.BlockSpec((N_PAIRS, tm, LANES), lambda i: (0, i, 0))
    return pl.pallas_call(
        _rwkv_prep_kernel,
        grid=(n // tm,),
        in_specs=[pl.BlockSpec((tm, C_SHIFT), lambda i: (i, 0)), pl.BlockSpec((tm, C_SHIFT), lambda i: (i, 0)),
                  full((1, C_SHIFT)), full((1, D_RWKV)), full((LANES, D_RWKV)), full((1, D_RWKV)),
                  full((LANES, D_RWKV)), full((GATE_LORA, D_RWKV)), full((1, D_RWKV)), full((1, D_RWKV)),
                  full((LANES, LANES))],
        out_specs=[out_spec] * 7,
        out_shape=[out_sd] * 7,
        compiler_params=_cparams("parallel"),
        name="rwkv_prep",
    )(zr, prev, row(lw['mu_shift']), row(lw['w0']), w2p, row(lw['a0']), a2p, lw['g2'].astype(BF16),
      row(lw['k_k']), row(lw['k_a']), _pair_ones())


def _rwkv_lockstep_kernel(r_ref, ld_ref, k_ref, v_ref, kap_ref, b_ref, g_ref, s0_ref, lg_ref, lb_ref, rk_ref,
                          y_ref, s_ref, *, chunk):
    c = pl.program_id(1)

    @pl.when(c == 0)
    def _():
        s_ref[...] = s0_ref[...]

    heads = range(H_RWKV)
    ti = lax.broadcasted_iota(jnp.int32, (chunk, chunk), 0)
    si = lax.broadcasted_iota(jnp.int32, (chunk, chunk), 1)
    tril_incl = (si <= ti)
    tril_strict = (si < ti)
    ltri = tril_incl.astype(F32)
    eye = (si == ti).astype(F32)
    hsl = [slice((hh % 2) * HEAD_DIM, (hh % 2 + 1) * HEAD_DIM) for hh in heads]

    ld = [ld_ref[p] for p in range(N_PAIRS)]
    cum = [_dot_hi(ltri, x) for x in ld]
    kt, rt, bi, ki, bh, kh, gall = [], [], [], [], [], [], []
    for p in range(N_PAIRS):
        last = cum[p][chunk - 1:chunk, :]
        gam_inv = jnp.exp(-cum[p])
        gam_tail = jnp.exp(last - cum[p])
        kap2, b2, k2 = kap_ref[p], b_ref[p], k_ref[p]
        kt2 = (kap2 * jnp.exp(cum[p] - ld[p])).astype(BF16)
        rt2 = r_ref[p] * jnp.exp(cum[p])
        bi2 = (b2 * gam_inv).astype(BF16)
        ki2 = (k2 * gam_inv).astype(BF16)
        bh2 = (b2 * gam_tail).astype(BF16)
        kh2 = (k2 * gam_tail).astype(BF16)
        g2 = jnp.exp(last)
        for h in range(2):
            sl = hsl[h]
            kt.append(kt2[:, sl]); rt.append(rt2[:, sl]); bi.append(bi2[:, sl]); ki.append(ki2[:, sl])
            bh.append(bh2[:, sl]); kh.append(kh2[:, sl]); gall.append(g2[:, sl])
    vv = [v_ref[hh // 2][:, hsl[hh]].astype(BF16) for hh in heads]
    kr = [jnp.concatenate([kt[hh], rt[hh].astype(BF16)], axis=0) for hh in heads]
    xb = [_dot_nt(kr[hh], bi[hh]) for hh in heads]
    xk = [_dot_nt(kr[hh], ki[hh]) for hh in heads]
    a_ab = [jnp.where(tril_strict, xb[hh][:chunk], 0.0) for hh in heads]
    a_rb = [jnp.where(tril_incl, xb[hh][chunk:], 0.0).astype(BF16) for hh in heads]
    a_ak = [jnp.where(tril_strict, xk[hh][:chunk], 0.0).astype(BF16) for hh in heads]
    a_rk = [jnp.where(tril_incl, xk[hh][chunk:], 0.0).astype(BF16) for hh in heads]
    akv = [_dot(a_ak[hh], vv[hh]) for hh in heads]
    ab = [x.astype(BF16) for x in a_ab]
    tm = [eye - a_ab[hh] for hh in heads]
    pw = [_dot(ab[hh], ab[hh]) for hh in heads]
    n_sq = int(math.log2(chunk)) - 1
    for it in range(n_sq):
        pwb = [x.astype(BF16) for x in pw]
        tm = [tm[hh] + _dot(tm[hh], pwb[hh]) for hh in heads]
        if it < n_sq - 1:
            pw = [_dot(pwb[hh], pwb[hh]) for hh in heads]
    tmb = [x.astype(BF16) for x in tm]
    w = [-_dot(tmb[hh], kt[hh]) for hh in heads]
    u0 = [-_dot(tmb[hh], akv[hh]) for hh in heads]
    wb = [x.astype(BF16) for x in w]
    u0b = [x.astype(BF16) for x in u0]
    pm = [rt[hh] + _dot(a_rb[hh], wb[hh]) for hh in heads]
    y0 = [_dot(a_rb[hh], u0b[hh]) + _dot(a_rk[hh], vv[hh]) for hh in heads]
    s_old = [s_ref[0, hh] for hh in heads]
    sb = [x.astype(BF16) for x in s_old]
    y = [_dot_nt(pm[hh], sb[hh]) + y0[hh] for hh in heads]
    mt = [_dot_tn(bh[hh], wb[hh]) for hh in heads]
    nt = [_dot_tn(jnp.concatenate([u0b[hh], vv[hh]], axis=0), jnp.concatenate([bh[hh], kh[hh]], axis=0))
          for hh in heads]
    for hh in heads:
        s_ref[0, hh] = s_old[hh] * gall[hh] + _dot_nt(sb[hh], mt[hh]) + nt[hh]
    for p in range(N_PAIRS):
        outs = []
        for h in range(2):
            hh = 2 * p + h
            sl = hsl[h]
            mu = jnp.mean(y[hh], axis=-1, keepdims=True)
            var = jnp.mean(jnp.square(y[hh] - mu), axis=-1, keepdims=True)
            yn = (y[hh] - mu) * lax.rsqrt(var + LNX_EPS)
            rk = r_ref[p][:, sl] * k_ref[p][:, sl] * rk_ref[p][:, sl]
            bonus = jnp.sum(rk, axis=-1, keepdims=True) * v_ref[p][:, sl]
            outs.append(yn * lg_ref[p][:, sl] + lb_ref[p][:, sl] + bonus)
        y_ref[p] = jnp.concatenate(outs, axis=-1) * g_ref[p]


def _rwkv_chunks(prep, s0, lw, n_batch, seq, chunk):
    n_chunks = seq // chunk
    pairs = lambda a: a.reshape(N_PAIRS, 1, LANES).astype(F32)
    tile = pl.BlockSpec((N_PAIRS, chunk, LANES), lambda b, c: (0, b * n_chunks + c, 0))
    state = pl.BlockSpec((1, H_RWKV, HEAD_DIM, HEAD_DIM), lambda b, c: (b, 0, 0, 0))
    par = pl.BlockSpec((N_PAIRS, 1, LANES), lambda b, c: (0, 0, 0))
    return pl.pallas_call(
        functools.partial(_rwkv_lockstep_kernel, chunk=chunk),
        grid=(n_batch, n_chunks),
        in_specs=[tile] * 7 + [state, par, par, par],
        out_specs=[tile, state],
        out_shape=[jax.ShapeDtypeStruct((N_PAIRS, n_batch * seq, LANES), F32),
                   jax.ShapeDtypeStruct((n_batch, H_RWKV, HEAD_DIM, HEAD_DIM), F32)],
        compiler_params=_cparams("parallel", "arbitrary"),
        name="rwkv_chunks",
    )(*prep, s0.astype(F32), pairs(lw['lnx_g']), pairs(lw['lnx_b']), pairs(lw['r_k']))


def _rwkv(zr, shift0, s0, lw, n_batch, seq):
    z3 = zr.reshape(n_batch, seq, C_SHIFT)
    prev = jnp.concatenate([shift0[:, None].astype(F32), z3[:, :-1]], axis=1).reshape(n_batch * seq, C_SHIFT)
    prep = _rwkv_prep(zr, prev, lw, min(256, n_batch * seq))
    chunk = CHUNK if seq >= CHUNK else SHORT_CHUNK
    pad = (-seq) % chunk
    if pad:
        prep = [jnp.pad(a.reshape(N_PAIRS, n_batch, seq, LANES), ((0, 0), (0, 0), (0, pad), (0, 0)))
                .reshape(N_PAIRS, n_batch * (seq + pad), LANES) for a in prep]
    y, s_fin = _rwkv_chunks(prep, s0, lw, n_batch, seq + pad, chunk)
    if pad:
        y = y.reshape(N_PAIRS, n_batch, seq + pad, LANES)[:, :, :seq].reshape(N_PAIRS, n_batch * seq, LANES)
    return y, s_fin, z3[:, -1]


def _layer_norm(h, g, b):
    mu = jnp.mean(h, axis=-1, keepdims=True)
    d = h - mu
    var = jnp.mean(d * d, axis=-1, keepdims=True)
    return d * lax.rsqrt(var + LN_EPS) * g + b


def _out_proj_kernel(oa_ref, or_ref, x_ref, wa_ref, wr_ref, g_ref, b_ref, rw_ref, x1_ref, x1b_ref, sc_ref):
    orw = jnp.concatenate([or_ref[p] for p in range(N_PAIRS)], axis=-1)
    mix = _dot(oa_ref[...], wa_ref[...]) + _dot(orw, wr_ref[...])
    x1 = _layer_norm(ALPHA * x_ref[...] + mix, g_ref[...], b_ref[...])
    x1_ref[...] = x1
    x1b_ref[...] = x1.astype(BF16)
    sc_ref[...] = _sigmoid(lax.dot_general(rw_ref[...], x1, (((1,), (1,)), ((), ())),
                                           precision=lax.Precision.HIGHEST, preferred_element_type=F32))


def _out_proj(o_attn, o_rwkv, x, lw, tm):
    n = x.shape[0]
    wo = lw['w_out'].astype(BF16)
    full = lambda shape: pl.BlockSpec(shape, lambda i: (0,) * len(shape))
    return pl.pallas_call(
        _out_proj_kernel,
        grid=(n // tm,),
        in_specs=[pl.BlockSpec((tm, D_ATTN), lambda i: (i, 0)),
                  pl.BlockSpec((N_PAIRS, tm, LANES), lambda i: (0, i, 0)),
                  pl.BlockSpec((tm, D_MODEL), lambda i: (i, 0)),
                  full((D_ATTN, D_MODEL)), full((D_RWKV, D_MODEL)), full((1, D_MODEL)), full((1, D_MODEL)),
                  full((N_EXPERTS, D_MODEL))],
        out_specs=[pl.BlockSpec((tm, D_MODEL), lambda i: (i, 0)), pl.BlockSpec((tm, D_MODEL), lambda i: (i, 0)),
                   pl.BlockSpec((N_EXPERTS, tm), lambda i: (0, i))],
        out_shape=[jax.ShapeDtypeStruct((n, D_MODEL), F32), jax.ShapeDtypeStruct((n, D_MODEL), BF16),
                   jax.ShapeDtypeStruct((N_EXPERTS, n), F32)],
        compiler_params=_cparams("parallel"),
        name="out_proj_ln_router",
    )(o_attn, o_rwkv, x, wo[:D_ATTN], wo[D_ATTN:], lw['ln1_g'].reshape(1, -1), lw['ln1_b'].reshape(1, -1),
      lw['router_w'].astype(F32).T)


def _expert_kernel(be_ref, na_ref, x_ref, wg_ref, wu_ref, wd_ref, *rest):
    y_ref, wg_s, wu_s, wd_s = rest[-4:]
    g = pl.program_id(0)
    active = g < na_ref[0]

    @pl.when(active & ((g == 0) | (be_ref[g] != be_ref[jnp.maximum(g - 1, 0)])))
    def _():
        wg_s[...] = wg_ref[0].astype(BF16)
        wu_s[...] = wu_ref[0].astype(BF16)
        wd_s[...] = wd_ref[0].astype(BF16)

    @pl.when(active)
    def _():
        x = x_ref[...]
        hg = jnp.dot(x, wg_s[...], preferred_element_type=F32)
        hu = jnp.dot(x, wu_s[...], preferred_element_type=F32)
        h = hg * _sigmoid(hg) * hu
        y_ref[...] = jnp.dot(h.astype(BF16), wd_s[...], preferred_element_type=F32).astype(y_ref.dtype)

    @pl.when(g >= na_ref[0])
    def _():
        y_ref[...] = jnp.zeros_like(y_ref)


def _expert_blocks(xs, blk_e, n_active, wg, wu, wd, tm, out_dtype, after):
    rows = xs.shape[0]
    n_blocks = rows // tm
    d_e = wg.shape[-1]
    return pl.pallas_call(
        _expert_kernel,
        grid_spec=pltpu.PrefetchScalarGridSpec(
            num_scalar_prefetch=2,
            grid=(n_blocks,),
            in_specs=[pl.BlockSpec((tm, D_MODEL), lambda g, be, na: (g, 0)),
                      pl.BlockSpec((1, D_MODEL, d_e), lambda g, be, na: (be[g], 0, 0)),
                      pl.BlockSpec((1, D_MODEL, d_e), lambda g, be, na: (be[g], 0, 0)),
                      pl.BlockSpec((1, d_e, D_MODEL), lambda g, be, na: (be[g], 0, 0))]
            + [pl.BlockSpec(memory_space=pl.ANY)] * len(after),
            out_specs=pl.BlockSpec((tm, D_MODEL), lambda g, be, na: (g, 0)),
            scratch_shapes=[pltpu.VMEM((D_MODEL, d_e), BF16), pltpu.VMEM((D_MODEL, d_e), BF16),
                            pltpu.VMEM((d_e, D_MODEL), BF16)],
        ),
        out_shape=jax.ShapeDtypeStruct((rows, D_MODEL), out_dtype),
        compiler_params=_cparams("arbitrary"),
        name="expert_blocks",
    )(blk_e, n_active, xs, wg, wu, wd, *after)


def _first_max(x, idx, n):
    m = jnp.max(x, axis=0, keepdims=True)
    return m, jnp.min(jnp.where(x == m, idx, n), axis=0, keepdims=True)


def _route_kernel(sc_ref, bias_ref, tri_ref, e_ref, gate_ref, rank_ref, cnt_ref, carry_s):
    i = pl.program_id(0)
    tr = sc_ref.shape[1]
    per_group = N_EXPERTS // N_GROUPS

    @pl.when(i == 0)
    def _():
        carry_s[...] = jnp.zeros_like(carry_s)

    sc = sc_ref[...]
    choice = sc + bias_ref[...]
    in_g = lax.broadcasted_iota(jnp.int32, (per_group, tr), 0)
    g_iota = lax.broadcasted_iota(jnp.int32, (N_GROUPS, tr), 0)
    grp = jnp.zeros((N_GROUPS, tr), F32)
    for gi in range(N_GROUPS):
        cg = choice[gi * per_group:(gi + 1) * per_group]
        m1, i1 = _first_max(cg, in_g, per_group)
        m2 = jnp.max(jnp.where(in_g == i1, NEG_INF, cg), axis=0, keepdims=True)
        grp = jnp.where(g_iota == gi, m1 + m2, grp)
    keep = jnp.zeros((N_GROUPS, tr), F32)
    for _ in range(TOPK_GROUPS):
        _, f = _first_max(grp, g_iota, N_GROUPS)
        hit = g_iota == f
        keep = jnp.where(hit, 1.0, keep)
        grp = jnp.where(hit, NEG_INF, grp)
    masked = jnp.concatenate(
        [jnp.where(keep[gi:gi + 1] > 0.0, choice[gi * per_group:(gi + 1) * per_group], NEG_INF)
         for gi in range(N_GROUPS)], axis=0)
    e_iota = lax.broadcasted_iota(jnp.int32, (N_EXPERTS, tr), 0)
    hits, es, ss = [], [], []
    onehot = jnp.zeros((N_EXPERTS, tr), F32)
    for _ in range(TOP_K):
        _, f = _first_max(masked, e_iota, N_EXPERTS)
        hit = e_iota == f
        hits.append(hit)
        es.append(f)
        ss.append(jnp.sum(jnp.where(hit, sc, 0.0), axis=0, keepdims=True))
        onehot = jnp.where(hit, 1.0, onehot)
        masked = jnp.where(hit, NEG_INF, masked)
    total = ss[0]
    for s in ss[1:]:
        total = total + s
    e_ref[...] = jnp.concatenate(es, axis=0)
    gate_ref[...] = jnp.concatenate([s / total * ROUTED_SCALE for s in ss], axis=0)
    before = carry_s[...] + jnp.dot(onehot.astype(BF16), tri_ref[...], preferred_element_type=F32)
    rank_ref[...] = jnp.concatenate(
        [jnp.sum(jnp.where(h, before, 0.0), axis=0, keepdims=True) for h in hits], axis=0).astype(jnp.int32)
    carry = carry_s[...] + jnp.sum(onehot, axis=1, keepdims=True)
    carry_s[...] = carry
    cnt_ref[...] = jnp.broadcast_to(carry, cnt_ref.shape).astype(jnp.int32)


def _route(scores_t, router_bias, tok0, n):
    tr = min(256, n)
    assert tok0 % tr == 0 and n % tr == 0
    tri = jnp.asarray(np.arange(tr)[:, None] < np.arange(tr)[None, :], BF16)
    col = pl.BlockSpec((TOP_K, tr), lambda i: (0, i))
    top_e, gate, rank, counts = pl.pallas_call(
        _route_kernel,
        grid=(n // tr,),
        in_specs=[pl.BlockSpec((N_EXPERTS, tr), lambda i: (0, i + tok0 // tr)),
                  pl.BlockSpec((N_EXPERTS, 1), lambda i: (0, 0)),
                  pl.BlockSpec((tr, tr), lambda i: (0, 0))],
        out_specs=[col, col, col, pl.BlockSpec((N_EXPERTS, LANES), lambda i: (0, 0))],
        out_shape=[jax.ShapeDtypeStruct((TOP_K, n), jnp.int32), jax.ShapeDtypeStruct((TOP_K, n), F32),
                   jax.ShapeDtypeStruct((TOP_K, n), jnp.int32), jax.ShapeDtypeStruct((N_EXPERTS, LANES), jnp.int32)],
        scratch_shapes=[pltpu.VMEM((N_EXPERTS, 1), F32)],
        compiler_params=_cparams("arbitrary"),
        name="route",
    )(scores_t, router_bias.astype(F32).reshape(N_EXPERTS, 1), tri)
    counts = counts[:, 0]
    pcounts = (counts + MOE_TM - 1) // MOE_TM * MOE_TM
    pends = jnp.cumsum(pcounts)
    starts = pends - pcounts
    dest = rank
    for e in range(N_EXPERTS):
        dest = dest + jnp.where(top_e == e, starts[e], 0)
    n_blocks = -(-(n * TOP_K) // MOE_TM) + N_EXPERTS
    slot_tok = jnp.zeros((n_blocks * MOE_TM,), jnp.int32).at[dest.reshape(-1)].set(
        jnp.tile(jnp.arange(tok0, tok0 + n, dtype=jnp.int32), TOP_K))
    blk_start = jnp.arange(n_blocks, dtype=jnp.int32) * MOE_TM
    blk_e = jnp.minimum(jnp.sum((pends[None, :] <= blk_start[:, None]).astype(jnp.int32), axis=1), N_EXPERTS - 1)
    n_active = (pends[-1] // MOE_TM).astype(jnp.int32).reshape(1)
    return gate.T, dest.astype(jnp.int32), slot_tok, blk_e, n_active


def _combine_kernel(x1_ref, sh_ref, gate_ref, g_ref, b_ref, *rest, tiles_per_part):
    ys_refs, o_ref = rest[:-1], rest[-1]
    i = pl.program_id(0)
    gate = gate_ref[...]
    for part, ys_ref in enumerate(ys_refs):
        @pl.when(i // tiles_per_part == part)
        def _():
            f = sh_ref[...]
            for j in range(TOP_K):
                f = f + ys_ref[j].astype(F32) * gate[:, j:j + 1]
            o_ref[...] = _layer_norm(ALPHA * x1_ref[...] + f, g_ref[...], b_ref[...])


def _combine(x1, shared, ysgs, gate, lw, tm):
    n = x1.shape[0]
    tiles_per_part = n // len(ysgs) // tm
    row = pl.BlockSpec((tm, D_MODEL), lambda i: (i, 0))
    par = pl.BlockSpec((1, D_MODEL), lambda i: (0, 0))

    def part_spec(part):
        return pl.BlockSpec((TOP_K, tm, D_MODEL),
                            lambda i: (0, jnp.clip(i - part * tiles_per_part, 0, tiles_per_part - 1), 0))

    return pl.pallas_call(
        functools.partial(_combine_kernel, tiles_per_part=tiles_per_part),
        grid=(n // tm,),
        in_specs=[row, row, pl.BlockSpec((tm, TOP_K), lambda i: (i, 0)), par, par]
        + [part_spec(part) for part in range(len(ysgs))],
        out_specs=row,
        out_shape=jax.ShapeDtypeStruct((n, D_MODEL), F32),
        compiler_params=_cparams("arbitrary"),
        name="moe_combine_ln",
    )(x1, shared, gate, lw['ln2_g'].reshape(1, -1), lw['ln2_b'].reshape(1, -1), *ysgs)


def _moe(x1, x1b, scores_t, lw, ew, n_parts):
    n = x1.shape[0]
    n_part = n // n_parts
    routes = [_route(scores_t, lw['router_bias'], part * n_part, n_part) for part in range(n_parts)]
    ysgs, ys = [], None
    for part, (_, dest, slot_tok, blk_e, n_active) in enumerate(routes):
        xs = x1b[slot_tok]
        after = [routes[part + 1][2]] if part + 1 < n_parts else []
        after = after + ([ys] if ys is not None else [])
        ys = _expert_blocks(xs, blk_e, n_active, ew['e_gate'], ew['e_up'], ew['e_down'], MOE_TM, BF16, after)
        ysgs.append(ys[dest.reshape(-1)].reshape(TOP_K, n_part, D_MODEL))
    tm_s = min(MOE_TM, n)
    shared = _expert_blocks(x1b, jnp.zeros((n // tm_s,), jnp.int32), jnp.full((1,), n // tm_s, jnp.int32),
                            ew['s_gate'], ew['s_up'], ew['s_down'], tm_s, F32, [ys])
    gate = jnp.concatenate([r[0] for r in routes], axis=0)
    return _combine(x1, shared, ysgs, gate, lw, min(128, n_part)), ys


def _copy_kernel(x_ref, after_ref, o_ref):
    del after_ref
    o_ref[...] = x_ref[...]


def _order_after(x, after):
    return pl.pallas_call(
        _copy_kernel,
        in_specs=[pl.BlockSpec(memory_space=pltpu.VMEM), pl.BlockSpec(memory_space=pl.ANY)],
        out_specs=pl.BlockSpec(memory_space=pltpu.VMEM),
        out_shape=jax.ShapeDtypeStruct(x.shape, x.dtype),
        name="order_after",
    )(x, after)


def _project(x, w_in_b):
    n = x.shape[0] * x.shape[1]
    xb = x.reshape(n, D_MODEL).astype(BF16)
    tm = min(512, n)
    return _matmul(xb, w_in_b[:, :3 * D_ATTN], tm, 768), _matmul(xb, w_in_b[:, 3 * D_ATTN:], tm, C_SHIFT // 2)


def _layer(x, zqkv, zr, o_attn, shift0, s0, lw, ew):
    n_batch, seq, _ = x.shape
    n = n_batch * seq
    o_rwkv, s_new, shift_new = _rwkv(zr, shift0, s0, lw, n_batch, seq)
    x1, x1b, scores = _out_proj(o_attn, o_rwkv, x.reshape(n, D_MODEL), lw, min(256, n))
    y, ys_last = _moe(x1, x1b, scores, lw, ew, MOE_TOKEN_PARTS if n >= MOE_TOKEN_PARTS * 2048 else 1)
    k = zqkv[:, D_ATTN:2 * D_ATTN].reshape(n_batch, seq, H_ATTN, HEAD_DIM)
    v = zqkv[:, 2 * D_ATTN:].reshape(n_batch, seq, H_ATTN, HEAD_DIM)
    return (y.reshape(n_batch, seq, D_MODEL), k, v, s_new, shift_new), ys_last


def kernel(x_prompt, x_sample, cache_k, cache_v, state_wkv, state_shift, page_table, rel_bias, w_in, mu_shift, w0, w2,
           a0, a2, g2, k_k, k_a, r_k, lnx_g, lnx_b, w_out, ln1_g, ln1_b, router_w, router_bias, e_gate, e_up, e_down,
           s_gate, s_up, s_down, ln2_g, ln2_b):
    assert w_in.shape[0] == DEPTH == 1
    n_b, seq, _ = x_prompt.shape
    db, dec_seq, _ = x_sample.shape
    past_len = page_table.shape[1] * PAGE_SIZE
    lw = {'mu_shift': mu_shift[0], 'w0': w0[0], 'w2': w2[0], 'a0': a0[0], 'a2': a2[0], 'g2': g2[0], 'k_k': k_k[0],
          'k_a': k_a[0], 'r_k': r_k[0], 'lnx_g': lnx_g[0], 'lnx_b': lnx_b[0], 'w_out': w_out[0], 'ln1_g': ln1_g[0],
          'ln1_b': ln1_b[0], 'router_w': router_w[0], 'router_bias': router_bias[0], 'ln2_g': ln2_g[0],
          'ln2_b': ln2_b[0]}
    ew = {'e_gate': e_gate[0], 'e_up': e_up[0], 'e_down': e_down[0],
          's_gate': s_gate, 's_up': s_up, 's_down': s_down}
    w_in_b = w_in[0].astype(BF16)

    zqkv_s, zr_s = _project(x_sample, w_in_b)
    idx = _sample_select(zqkv_s, cache_k, page_table, db, dec_seq, past_len)
    zqkv_p, zr_p = _project(x_prompt, w_in_b)
    o_attn_p = _moba_prompt(zqkv_p, _order_after(rel_bias, idx), n_b, seq)
    (y_p, k_p, v_p, w_p, s_p), ys_last = _layer(
        x_prompt, zqkv_p, zr_p, o_attn_p, jnp.zeros((n_b, C_SHIFT), F32),
        jnp.zeros((n_b, H_RWKV, HEAD_DIM, HEAD_DIM), F32), lw, ew)
    o_attn_s = _sample_attend(zqkv_s, idx, cache_k, cache_v, page_table, rel_bias, db, dec_seq, past_len, ys_last)
    (y_s, k_s, v_s, w_s, s_s), _ = _layer(x_sample, zqkv_s, zr_s, o_attn_s, state_shift[0], state_wkv[0], lw, ew)
    return (y_p, y_s, k_p[None], v_p[None], w_p[None], s_p[None], k_s[None], v_s[None], w_s[None], s_s[None])
```

```python
import functools
import math

import numpy as np
import jax
import jax.numpy as jnp
from jax import lax
from jax.experimental import pallas as pl
from jax.experimental.pallas import tpu as pltpu

F32 = jnp.float32
BF16 = jnp.bfloat16

D_MODEL = 2048
HEAD_DIM = 64
D_ATTN = 1024
D_RWKV = 1024
H_ATTN = 16
H_RWKV = 16
N_PAIRS = 8
LANES = 128
BLOCK = 256
TOP_BLOCKS = 3
PAGE_SIZE = 128
N_BUCKETS = 32
MAX_DISTANCE = 128
DECAY_LORA = 64
AAA_LORA = 64
GATE_LORA = 128
C_SHIFT = 3 * D_RWKV + DECAY_LORA + AAA_LORA + GATE_LORA
LNX_EPS = 64e-5
LN_EPS = 1e-5
N_EXPERTS = 64
TOP_K = 8
N_GROUPS = 8
TOPK_GROUPS = 4
D_EXPERT = 512
ROUTED_SCALE = 2.5
DEPTH = 1
ALPHA = (2.0 * DEPTH) ** 0.25
CHUNK = 64
MOE_TM = 256
MOE_TOKEN_PARTS = 1
SHORT_CHUNK = 16
FAR_BLOCKS_PER_STEP = 2
VMEM_LIMIT = 56 * 1024 * 1024
NEG_INF = float("-inf")
LOG2E = math.log2(math.e)


def _cparams(*sem):
    return pltpu.CompilerParams(dimension_semantics=sem, vmem_limit_bytes=VMEM_LIMIT)


def _dot(a, b):
    return jnp.dot(a.astype(BF16), b.astype(BF16), preferred_element_type=F32)


def _dot_nt(a, b):
    return lax.dot_general(a.astype(BF16), b.astype(BF16), (((1,), (1,)), ((), ())), preferred_element_type=F32)


def _dot_tn(a, b):
    return lax.dot_general(a.astype(BF16), b.astype(BF16), (((0,), (0,)), ((), ())), preferred_element_type=F32)


def _dot_hi(a, b):
    return jnp.dot(a, b, precision=lax.Precision.HIGHEST, preferred_element_type=F32)


def _sigmoid(x):
    return 1.0 / (1.0 + jnp.exp(-x))


def _mm_kernel(x_ref, w_ref, o_ref):
    o_ref[...] = jnp.dot(x_ref[...], w_ref[...], preferred_element_type=F32)


def _matmul(x, w, tm, tn):
    m, k = x.shape
    n = w.shape[1]
    return pl.pallas_call(
        _mm_kernel,
        grid=(m // tm, n // tn),
        in_specs=[pl.BlockSpec((tm, k), lambda i, j: (i, 0)),
                  pl.BlockSpec((k, tn), lambda i, j: (0, j))],
        out_specs=pl.BlockSpec((tm, tn), lambda i, j: (i, j)),
        out_shape=jax.ShapeDtypeStruct((m, n), F32),
        compiler_params=_cparams("parallel", "parallel"),
        name="in_proj",
    )(x, w)


def _bucket_table(max_dist):
    n = np.arange(max_dist + 1)
    max_exact = N_BUCKETS // 2
    nf = np.maximum(n, 1).astype(np.float64)
    large = max_exact + (np.log(nf / max_exact) / math.log(MAX_DISTANCE / max_exact)
                         * (N_BUCKETS - max_exact)).astype(np.int64)
    large = np.minimum(large, N_BUCKETS - 1)
    return np.where(n < max_exact, n, large).astype(np.int32)


def _relative_bias_log2(rel_bias):
    rb = rel_bias.astype(F32)
    return (rb - rb[N_BUCKETS - 1:]) * LOG2E


def _prompt_bucket_tiles():
    kpos = np.arange(BLOCK)[:, None]
    qpos = np.arange(BLOCK)[None, :]
    bt = _bucket_table(2 * BLOCK)
    d_own = qpos - kpos
    own = np.where(d_own >= 0, bt[np.maximum(d_own, 0)], -1)
    return jnp.asarray(np.stack([own, bt[BLOCK + qpos - kpos]]).astype(np.int32))


def _moba_prompt_kernel(rb_ref, q_ref, k_ref, v_ref, bkt_ref, o_ref, kb_s, vt_s, km_s, sel_s, bias_s, sa_s, sb_s,
                        *, n_blk):
    p = pl.program_id(1)
    i = pl.program_id(2)

    @pl.when(i == 0)
    def _():
        for kind in range(2):
            bk = bkt_ref[kind]
            for h in range(2):
                t = jnp.zeros((BLOCK, BLOCK), F32)
                for b in range(N_BUCKETS):
                    t = jnp.where(bk == b, rb_ref[b * H_ATTN + 2 * p + h], t)
                bias_s[h, kind] = jnp.where(bk < 0, NEG_INF, t)

        def fill(j, c):
            r0 = pl.multiple_of(j * BLOCK, BLOCK)
            kblk = k_ref[pl.ds(r0, BLOCK), :]
            vblk = v_ref[pl.ds(r0, BLOCK), :]
            kb_s[pl.ds(r0, BLOCK), :] = kblk.astype(BF16)
            vt_s[j] = vblk.T.astype(BF16)
            km_s[pl.ds(j, 1), :] = jnp.sum(kblk, axis=0, keepdims=True) * (1.0 / BLOCK)
            return c
        lax.fori_loop(0, n_blk, fill, 0)

    qt = q_ref[...].T
    row = lax.broadcasted_iota(jnp.int32, (LANES, BLOCK), 0)
    head0 = row < HEAD_DIM
    qts = [jnp.where(head0, qt, 0.0), jnp.where(head0, 0.0, qt)]
    qtb = [(x * (HEAD_DIM ** -0.5 * LOG2E)).astype(BF16) for x in qts]

    blk = lax.broadcasted_iota(jnp.int32, (n_blk, BLOCK), 0)
    valid = blk < i
    km = km_s[...]
    for h in range(2):
        g = jnp.where(valid, _dot_hi(km, qts[h]), NEG_INF)
        sel = jnp.zeros((n_blk, BLOCK), F32)
        for _ in range(TOP_BLOCKS):
            m = jnp.max(g, axis=0, keepdims=True)
            first = jnp.min(jnp.where(g == m, blk, n_blk), axis=0, keepdims=True)
            hit = blk == first
            sel = jnp.where(hit & valid, 1.0, sel)
            g = jnp.where(hit, NEG_INF, g)
        sel_s[h] = sel

    def scores(j, h):
        r0 = pl.multiple_of(j * BLOCK, BLOCK)
        return jnp.dot(kb_s[pl.ds(r0, BLOCK), :], qtb[h], preferred_element_type=F32)

    def pv(j, pb0, pb1):
        vt = vt_s[j]
        return jnp.concatenate(
            [jnp.dot(vt[:HEAD_DIM], pb0, preferred_element_type=F32),
             jnp.dot(vt[HEAD_DIM:], pb1, preferred_element_type=F32)], axis=0)

    def update(blocks, carry):
        m0, m1, l0, l1, acc = carry
        new = []
        for h, (m, l) in enumerate(((m0, l0), (m1, l1))):
            mn = m
            for blk_ in blocks:
                mn = jnp.where(blk_[3 + h], jnp.maximum(mn, jnp.max(blk_[1 + h], axis=0, keepdims=True)), mn)
            a = jnp.exp2(m - mn)
            ln = a * l
            pbs = []
            for blk_ in blocks:
                pr = jnp.exp2(blk_[1 + h] - jnp.where(blk_[3 + h], mn, jnp.inf))
                ln = ln + jnp.sum(pr, axis=0, keepdims=True)
                pbs.append(pr.astype(BF16))
            new.append((mn, a, ln, pbs))
        acc = jnp.where(head0, new[0][1], new[1][1]) * acc
        for n, blk_ in enumerate(blocks):
            acc = acc + pv(blk_[0], new[0][3][n], new[1][3][n])
        return new[0][0], new[1][0], new[0][2], new[1][2], acc

    jp = jnp.maximum(i - 1, 0)
    has_prev = i > 0
    keep0 = (sel_s[0, pl.ds(jp, 1), :] > 0.0) & has_prev
    keep1 = (sel_s[1, pl.ds(jp, 1), :] > 0.0) & has_prev
    always = jnp.full((1, BLOCK), True)
    row_init = jnp.full((1, BLOCK), NEG_INF, F32)
    carry = (row_init, row_init, jnp.zeros((1, BLOCK), F32), jnp.zeros((1, BLOCK), F32),
             jnp.zeros((LANES, BLOCK), F32))
    n_far = jnp.maximum(i - 1, 0)

    def issue_scores(t, dst):
        for u in range(FAR_BLOCKS_PER_STEP):
            jc = jnp.minimum(t * FAR_BLOCKS_PER_STEP + u, n_blk - 1)
            for h in range(2):
                dst[u, h] = scores(jc, h)

    def consume(t, src, carry):
        blocks = []
        for u in range(FAR_BLOCKS_PER_STEP):
            j = t * FAR_BLOCKS_PER_STEP + u
            in_range = j < n_far
            jc = jnp.minimum(j, n_blk - 1)
            blocks.append((jc, src[u, 0], src[u, 1],
                           (sel_s[0, pl.ds(jc, 1), :] > 0.0) & in_range,
                           (sel_s[1, pl.ds(jc, 1), :] > 0.0) & in_range))
        return update(blocks, carry)

    issue_scores(0, sa_s)
    carry = update([(i, scores(i, 0) + bias_s[0, 0], scores(i, 1) + bias_s[1, 0], always, always),
                    (jp, scores(jp, 0) + bias_s[0, 1], scores(jp, 1) + bias_s[1, 1], keep0, keep1)], carry)

    def far_body(it, carry):
        issue_scores(2 * it + 1, sb_s)
        carry = consume(2 * it, sa_s, carry)
        issue_scores(2 * it + 2, sa_s)
        return consume(2 * it + 1, sb_s, carry)

    n_iter = (n_far + 2 * FAR_BLOCKS_PER_STEP - 1) // (2 * FAR_BLOCKS_PER_STEP)
    m0, m1, l0, l1, acc = lax.fori_loop(0, n_iter, far_body, carry)
    ot = acc / jnp.where(head0, l0, l1)
    o_ref[...] = ot.T


def _moba_prompt(zqkv, rel_bias, n_batch, seq):
    n_blk = seq // BLOCK
    kern = functools.partial(_moba_prompt_kernel, n_blk=n_blk)
    stage = pltpu.VMEM((FAR_BLOCKS_PER_STEP, 2, BLOCK, BLOCK), F32)
    return pl.pallas_call(
        kern,
        grid=(n_batch, N_PAIRS, n_blk),
        in_specs=[
            pl.BlockSpec(memory_space=pltpu.SMEM),
            pl.BlockSpec((BLOCK, LANES), lambda b, p, i: (b * n_blk + i, p)),
            pl.BlockSpec((seq, LANES), lambda b, p, i: (b, N_PAIRS + p)),
            pl.BlockSpec((seq, LANES), lambda b, p, i: (b, 2 * N_PAIRS + p)),
            pl.BlockSpec((2, BLOCK, BLOCK), lambda b, p, i: (0, 0, 0)),
        ],
        out_specs=pl.BlockSpec((BLOCK, LANES), lambda b, p, i: (b * n_blk + i, p)),
        out_shape=jax.ShapeDtypeStruct((n_batch * seq, D_ATTN), F32),
        scratch_shapes=[pltpu.VMEM((seq, LANES), BF16),
                        pltpu.VMEM((n_blk, LANES, BLOCK), BF16),
                        pltpu.VMEM((n_blk, LANES), F32),
                        pltpu.VMEM((2, n_blk, BLOCK), F32),
                        pltpu.VMEM((2, 2, BLOCK, BLOCK), F32),
                        stage, stage],
        compiler_params=_cparams("parallel", "parallel", "arbitrary"),
        name="moba_prompt",
    )(_relative_bias_log2(rel_bias).reshape(-1), zqkv, zqkv, zqkv, _prompt_bucket_tiles())


PAGES_PER_STEP = 8


def _split_bf16(x):
    hi = x.astype(BF16)
    return hi, (x - hi.astype(F32)).astype(BF16)


def _sample_gate_kernel(pt_ref, q_ref, seg_ref, *refs, n_past_blk, n_steps, dec_seq):
    page_refs = refs[:PAGES_PER_STEP]
    idx_ref = refs[PAGES_PER_STEP]
    sum_s = refs[PAGES_PER_STEP + 1]
    s = pl.program_id(1)
    ppb = BLOCK // PAGE_SIZE
    ones = jnp.ones((16, PAGE_SIZE), BF16)
    for bl in range(PAGES_PER_STEP // ppb):
        x = page_refs[ppb * bl][0]
        for e in range(1, ppb):
            x = x + page_refs[ppb * bl + e][0]
        hi, lo = _split_bf16(x)
        tot = _dot_nt(ones, hi) + _dot_nt(ones, lo)
        sum_s[pl.ds(s * (PAGES_PER_STEP // ppb) + bl, 1), :] = tot[:1]

    @pl.when(s == n_steps - 1)
    def _():
        blk = lax.broadcasted_iota(jnp.int32, (n_past_blk, LANES), 0)
        ksum = sum_s[...]
        for qi in range(dec_seq):
            hi, lo = _split_bf16(ksum * q_ref[0, qi:qi + 1, :])
            g = _dot(hi, seg_ref[...]) + _dot(lo, seg_ref[...])
            rows = []
            for _ in range(TOP_BLOCKS):
                m = jnp.max(g, axis=0, keepdims=True)
                first = jnp.min(jnp.where(g == m, blk, n_past_blk), axis=0, keepdims=True)
                rows.append(first)
                g = jnp.where(blk == first, NEG_INF, g)
            rows.append(jnp.zeros((8 - TOP_BLOCKS, LANES), jnp.int32))
            idx_ref[0, qi] = jnp.concatenate(rows, axis=0)


def _sample_gate(page_table, q8, cache_k_pages, n_past_blk, dec_seq):
    db = q8.shape[0]
    n_pages = page_table.shape[1]
    n_steps = n_pages // PAGES_PER_STEP
    kern = functools.partial(_sample_gate_kernel, n_past_blk=n_past_blk, n_steps=n_steps, dec_seq=dec_seq)
    col = np.arange(H_ATTN * HEAD_DIM)[:, None] // HEAD_DIM
    seg = jnp.asarray(col == np.arange(LANES)[None, :], BF16)

    def page_spec(r):
        return pl.BlockSpec((1, H_ATTN * HEAD_DIM, PAGE_SIZE),
                            lambda b, s, pt: (pt[b, s * PAGES_PER_STEP + r], 0, 0))

    return pl.pallas_call(
        kern,
        grid_spec=pltpu.PrefetchScalarGridSpec(
            num_scalar_prefetch=1,
            grid=(db, n_steps),
            in_specs=[pl.BlockSpec((1, 8, H_ATTN * HEAD_DIM), lambda b, s, pt: (b, 0, 0)),
                      pl.BlockSpec((H_ATTN * HEAD_DIM, LANES), lambda b, s, pt: (0, 0))]
            + [page_spec(r) for r in range(PAGES_PER_STEP)],
            out_specs=pl.BlockSpec((1, dec_seq, 8, LANES), lambda b, s, pt: (b, 0, 0, 0)),
            scratch_shapes=[pltpu.VMEM((n_past_blk, H_ATTN * HEAD_DIM), F32)],
        ),
        out_shape=jax.ShapeDtypeStruct((db, dec_seq, 8, LANES), jnp.int32),
        compiler_params=_cparams("parallel", "arbitrary"),
        name="sample_gate",
    )(page_table, q8, seg, *([cache_k_pages] * PAGES_PER_STEP))


def _sample_attn_kernel(slab_ref, idx_ref, qt_ref, kn_ref, vn_ref, bown_ref, blast_ref, far_ref, *refs,
                        dec_seq, n_past_blk):
    n_slab = dec_seq * TOP_BLOCKS * (BLOCK // PAGE_SIZE)
    k_refs = refs[:n_slab]
    v_refs = refs[n_slab:2 * n_slab]
    o_ref = refs[2 * n_slab]
    b = pl.program_id(0)
    h = pl.program_id(1)
    ppb = BLOCK // PAGE_SIZE
    lane8 = lax.broadcasted_iota(jnp.int32, (HEAD_DIM, 8), 1)
    out = jnp.zeros((HEAD_DIM, 8), F32)
    kn = kn_ref[0, 0]
    vn = vn_ref[0, 0]
    far = far_ref[h]
    per_q = TOP_BLOCKS * ppb
    queries = range(dec_seq)
    qcs = [qt_ref[0, 0][:, qi:qi + 1] * (HEAD_DIM ** -0.5) for qi in queries]
    rows = []
    for qi in queries:
        for t in range(TOP_BLOCKS):
            sel_blk = idx_ref[((b * H_ATTN + h) * dec_seq + qi) * TOP_BLOCKS + t]
            is_last = sel_blk == n_past_blk - 1
            for e in range(ppb):
                slab = k_refs[qi * per_q + t * ppb + e][0]
                srow = jnp.sum(slab * qcs[qi], axis=0, keepdims=True)
                rows.append(srow + jnp.where(is_last, blast_ref[0, qi, e:e + 1, :], far))
    s_own = [jnp.sum(kn * qcs[qi], axis=0, keepdims=True) + bown_ref[0, qi:qi + 1, :] for qi in queries]
    row_max = [functools.reduce(jnp.maximum, rows[qi * per_q:(qi + 1) * per_q]) for qi in queries]
    m = [jnp.maximum(jnp.max(row_max[qi], axis=-1, keepdims=True), jnp.max(s_own[qi], axis=-1, keepdims=True))
         for qi in queries]
    p_own = [jnp.exp(s_own[qi] - m[qi]) for qi in queries]
    prs = [jnp.exp(rows[n] - m[n // per_q]) for n in range(len(rows))]
    l = [jnp.sum(functools.reduce(jnp.add, prs[qi * per_q:(qi + 1) * per_q]), axis=-1, keepdims=True)
         + jnp.sum(p_own[qi], axis=-1, keepdims=True) for qi in queries]
    acc = [functools.reduce(jnp.add, [v_refs[n][0] * prs[n] for n in range(qi * per_q, (qi + 1) * per_q)])
           for qi in queries]
    o = [jnp.sum(acc[qi], axis=-1, keepdims=True) + jnp.sum(vn * p_own[qi], axis=-1, keepdims=True)
         for qi in queries]
    for qi in queries:
        out = jnp.where(lane8 == qi, o[qi] / l[qi], out)
    o_ref[0, 0] = out


def _sample_attn(slabs, idx_flat, qt, knt, vnt, bias_own, bias_last, far, ck_slabs, cv_slabs, dec_seq, n_past_blk):
    db = qt.shape[0]
    ppb = BLOCK // PAGE_SIZE
    n_slab = dec_seq * TOP_BLOCKS * ppb
    kern = functools.partial(_sample_attn_kernel, dec_seq=dec_seq, n_past_blk=n_past_blk)

    def slab_spec(n):
        return pl.BlockSpec((1, HEAD_DIM, PAGE_SIZE),
                            lambda b, h, sl, ix: (sl[(b * H_ATTN + h) * n_slab + n], 0, 0))

    small = lambda shape: pl.BlockSpec(shape, lambda b, h, sl, ix: (b, h, 0, 0))
    return pl.pallas_call(
        kern,
        grid_spec=pltpu.PrefetchScalarGridSpec(
            num_scalar_prefetch=2,
            grid=(db, H_ATTN),
            in_specs=[small((1, 1, HEAD_DIM, 8)), small((1, 1, HEAD_DIM, 8)), small((1, 1, HEAD_DIM, 8)),
                      pl.BlockSpec((1, dec_seq, 8), lambda b, h, sl, ix: (h, 0, 0)),
                      pl.BlockSpec((1, dec_seq, ppb, PAGE_SIZE), lambda b, h, sl, ix: (h, 0, 0, 0)),
                      pl.BlockSpec(memory_space=pltpu.SMEM)]
            + [slab_spec(n) for n in range(n_slab)] * 2,
            out_specs=small((1, 1, HEAD_DIM, 8)),
        ),
        out_shape=jax.ShapeDtypeStruct((db, H_ATTN, HEAD_DIM, 8), F32),
        compiler_params=_cparams("parallel", "parallel"),
        name="sample_attn",
    )(slabs, idx_flat, qt, knt, vnt, bias_own, bias_last, far, *([ck_slabs] * n_slab), *([cv_slabs] * n_slab))


def _sample_select(zqkv_s, cache_k, page_table, db, dec_seq, past_len):
    n_past_blk = past_len // BLOCK
    assert past_len % BLOCK == 0 and n_past_blk >= TOP_BLOCKS
    pool = cache_k.shape[1]
    ckt = cache_k[0].transpose(0, 2, 3, 1)
    q = zqkv_s.reshape(db, dec_seq, 3 * D_ATTN)[..., :D_ATTN]
    q8 = jnp.pad(q, ((0, 0), (0, 8 - dec_seq), (0, 0)))
    idx = _sample_gate(page_table, q8, ckt.reshape(pool, H_ATTN * HEAD_DIM, PAGE_SIZE), n_past_blk, dec_seq)
    return idx[:, :, :TOP_BLOCKS, :H_ATTN].transpose(0, 3, 1, 2)


def _sample_attend(zqkv_s, idx, cache_k, cache_v, page_table, rel_bias, db, dec_seq, past_len, after):
    n_past_blk = past_len // BLOCK
    ppb = BLOCK // PAGE_SIZE
    pool = cache_k.shape[1]
    ckt = cache_k[0].transpose(0, 2, 3, 1)
    cvt = cache_v[0].transpose(0, 2, 3, 1)
    z3 = zqkv_s.reshape(db, dec_seq, 3 * D_ATTN)
    q = z3[..., :D_ATTN]
    pages = jnp.take_along_axis(page_table[:, None, None, None, :],
                                (idx[..., None] * ppb + jnp.arange(ppb)).reshape(db, H_ATTN, dec_seq, 1, -1),
                                axis=-1)
    slabs = (pages.reshape(db, H_ATTN, -1) * H_ATTN + jnp.arange(H_ATTN)[None, :, None]).reshape(-1).astype(jnp.int32)
    slabs = _order_after(slabs.reshape(-1, LANES), after).reshape(-1)

    def heads_t(t):
        t = t.reshape(db, dec_seq, H_ATTN, HEAD_DIM).transpose(0, 2, 3, 1)
        return jnp.pad(t, ((0, 0), (0, 0), (0, 0), (0, 8 - dec_seq)))

    bt = _bucket_table(BLOCK + dec_seq)
    rb = rel_bias.astype(F32)
    qi = np.arange(dec_seq)[:, None]
    t8 = np.arange(8)[None, :]
    own_ok = (t8 <= qi) & (t8 < dec_seq)
    bias_own = jnp.where(jnp.asarray(own_ok)[..., None], rb[bt[np.maximum(qi - t8, 0)]], NEG_INF).transpose(2, 0, 1)
    off = np.arange(BLOCK).reshape(ppb, PAGE_SIZE)[None]
    bias_last = rb[bt[BLOCK + qi[:, :, None] - off]].transpose(3, 0, 1, 2)
    o = _sample_attn(slabs, idx.reshape(-1).astype(jnp.int32), heads_t(q), heads_t(z3[..., D_ATTN:2 * D_ATTN]),
                     heads_t(z3[..., 2 * D_ATTN:]), bias_own, bias_last, rb[N_BUCKETS - 1],
                     ckt.reshape(pool * H_ATTN, HEAD_DIM, PAGE_SIZE), cvt.reshape(pool * H_ATTN, HEAD_DIM, PAGE_SIZE),
                     dec_seq, n_past_blk)
    return o[..., :dec_seq].transpose(0, 3, 1, 2).reshape(db * dec_seq, D_ATTN)


def _pair_ones():
    r = np.arange(LANES)
    return jnp.asarray((r[:, None] // HEAD_DIM) == (r[None, :] // HEAD_DIM), BF16)


def _head_sum(x, ones_pair):
    hi = x.astype(BF16)
    lo = (x - hi.astype(F32)).astype(BF16)
    outs = []
    for g in range(N_PAIRS):
        sl = slice(g * LANES, (g + 1) * LANES)
        outs.append(jnp.dot(hi[:, sl], ones_pair, preferred_element_type=F32)
                    + jnp.dot(lo[:, sl], ones_pair, preferred_element_type=F32))
    return jnp.concatenate(outs, axis=-1)


def _rwkv_prep_kernel(z_ref, prev_ref, mu_ref, w0_ref, w2_ref, a0_ref, a2_ref, g2_ref, kk_ref, ka_ref, ones_ref,
                      r_ref, ld_ref, k_ref, v_ref, kap_ref, b_ref, g_ref):
    z = z_ref[...]
    zs = z + (prev_ref[...] - z) * mu_ref[...]
    o = D_RWKV
    r = zs[:, :o]
    k = zs[:, o:2 * o]
    v = zs[:, 2 * o:3 * o]
    lora = zs[:, 3 * o:3 * o + LANES]
    gl = zs[:, 3 * o + LANES:]
    x = w0_ref[...] + _dot(jnp.tanh(lora), w2_ref[...])
    nx = -x
    w = -(jnp.maximum(nx, 0.0) + jnp.log(1.0 + jnp.exp(-jnp.abs(nx)))) - 0.5
    logd = -jnp.exp(w)
    a = _sigmoid(a0_ref[...] + _dot(lora, a2_ref[...]))
    g = _dot(_sigmoid(gl), g2_ref[...])
    kk = k * kk_ref[...]
    nrm = jnp.sqrt(_head_sum(kk * kk, ones_ref[...]))
    kap = kk / jnp.maximum(nrm, 1e-12)
    k2 = k * (1.0 + (a - 1.0) * ka_ref[...])
    for p in range(N_PAIRS):
        sl = slice(p * LANES, (p + 1) * LANES)
        r_ref[p] = r[:, sl]
        ld_ref[p] = logd[:, sl]
        k_ref[p] = k2[:, sl]
        v_ref[p] = v[:, sl]
        kap_ref[p] = kap[:, sl]
        b_ref[p] = (kap * a)[:, sl]
        g_ref[p] = g[:, sl]


def _rwkv_prep(zr, prev, lw, tm):
    n = zr.shape[0]
    row = lambda a: a.reshape(1, -1).astype(F32)
    zpad = jnp.zeros((DECAY_LORA, D_RWKV), F32)
    w2p = jnp.concatenate([lw['w2'], zpad], axis=0).astype(BF16)
    a2p = jnp.concatenate([zpad, lw['a2']], axis=0).astype(BF16)
    full = lambda shape: pl.BlockSpec(shape, lambda i: (0,) * len(shape))
    out_sd = jax.ShapeDtypeStruct((N_PAIRS, n, LANES), F32)
    out_spec = pl.BlockSpec((N_PAIRS, tm, LANES), lambda i: (0, i, 0))
    return pl.pallas_call(
        _rwkv_prep_kernel,
        grid=(n // tm,),
        in_specs=[pl.BlockSpec((tm, C_SHIFT), lambda i: (i, 0)), pl.BlockSpec((tm, C_SHIFT), lambda i: (i, 0)),
                  full((1, C_SHIFT)), full((1, D_RWKV)), full((LANES, D_RWKV)), full((1, D_RWKV)),
                  full((LANES, D_RWKV)), full((GATE_LORA, D_RWKV)), full((1, D_RWKV)), full((1, D_RWKV)),
                  full((LANES, LANES))],
        out_specs=[out_spec] * 7,
        out_shape=[out_sd] * 7,
        compiler_params=_cparams("parallel"),
        name="rwkv_prep",
    )(zr, prev, row(lw['mu_shift']), row(lw['w0']), w2p, row(lw['a0']), a2p, lw['g2'].astype(BF16),
      row(lw['k_k']), row(lw['k_a']), _pair_ones())


def _rwkv_lockstep_kernel(r_ref, ld_ref, k_ref, v_ref, kap_ref, b_ref, g_ref, s0_ref, lg_ref, lb_ref, rk_ref,
                          y_ref, s_ref, *, chunk):
    c = pl.program_id(1)

    @pl.when(c == 0)
    def _():
        s_ref[...] = s0_ref[...]

    heads = range(H_RWKV)
    ti = lax.broadcasted_iota(jnp.int32, (chunk, chunk), 0)
    si = lax.broadcasted_iota(jnp.int32, (chunk, chunk), 1)
    tril_incl = (si <= ti)
    tril_strict = (si < ti)
    ltri = tril_incl.astype(F32)
    eye = (si == ti).astype(F32)
    hsl = [slice((hh % 2) * HEAD_DIM, (hh % 2 + 1) * HEAD_DIM) for hh in heads]

    ld = [ld_ref[p] for p in range(N_PAIRS)]
    cum = [_dot_hi(ltri, x) for x in ld]
    kt, rt, bi, ki, bh, kh, gall = [], [], [], [], [], [], []
    for p in range(N_PAIRS):
        last = cum[p][chunk - 1:chunk, :]
        gam_inv = jnp.exp(-cum[p])
        gam_tail = jnp.exp(last - cum[p])
        kap2, b2, k2 = kap_ref[p], b_ref[p], k_ref[p]
        kt2 = (kap2 * jnp.exp(cum[p] - ld[p])).astype(BF16)
        rt2 = r_ref[p] * jnp.exp(cum[p])
        bi2 = (b2 * gam_inv).astype(BF16)
        ki2 = (k2 * gam_inv).astype(BF16)
        bh2 = (b2 * gam_tail).astype(BF16)
        kh2 = (k2 * gam_tail).astype(BF16)
        g2 = jnp.exp(last)
        for h in range(2):
            sl = hsl[h]
            kt.append(kt2[:, sl]); rt.append(rt2[:, sl]); bi.append(bi2[:, sl]); ki.append(ki2[:, sl])
            bh.append(bh2[:, sl]); kh.append(kh2[:, sl]); gall.append(g2[:, sl])
    vv = [v_ref[hh // 2][:, hsl[hh]].astype(BF16) for hh in heads]
    kr = [jnp.concatenate([kt[hh], rt[hh].astype(BF16)], axis=0) for hh in heads]
    xb = [_dot_nt(kr[hh], bi[hh]) for hh in heads]
    xk = [_dot_nt(kr[hh], ki[hh]) for hh in heads]
    a_ab = [jnp.where(tril_strict, xb[hh][:chunk], 0.0) for hh in heads]
    a_rb = [jnp.where(tril_incl, xb[hh][chunk:], 0.0).astype(BF16) for hh in heads]
    a_ak = [jnp.where(tril_strict, xk[hh][:chunk], 0.0).astype(BF16) for hh in heads]
    a_rk = [jnp.where(tril_incl, xk[hh][chunk:], 0.0).astype(BF16) for hh in heads]
    akv = [_dot(a_ak[hh], vv[hh]) for hh in heads]
    ab = [x.astype(BF16) for x in a_ab]
    tm = [eye - a_ab[hh] for hh in heads]
    pw = [_dot(ab[hh], ab[hh]) for hh in heads]
    n_sq = int(math.log2(chunk)) - 1
    for it in range(n_sq):
        pwb = [x.astype(BF16) for x in pw]
        tm = [tm[hh] + _dot(tm[hh], pwb[hh]) for hh in heads]
        if it < n_sq - 1:
            pw = [_dot(pwb[hh], pwb[hh]) for hh in heads]
    tmb = [x.astype(BF16) for x in tm]
    w = [-_dot(tmb[hh], kt[hh]) for hh in heads]
    u0 = [-_dot(tmb[hh], akv[hh]) for hh in heads]
    wb = [x.astype(BF16) for x in w]
    u0b = [x.astype(BF16) for x in u0]
    pm = [rt[hh] + _dot(a_rb[hh], wb[hh]) for hh in heads]
    y0 = [_dot(a_rb[hh], u0b[hh]) + _dot(a_rk[hh], vv[hh]) for hh in heads]
    s_old = [s_ref[0, hh] for hh in heads]
    sb = [x.astype(BF16) for x in s_old]
    y = [_dot_nt(pm[hh], sb[hh]) + y0[hh] for hh in heads]
    mt = [_dot_tn(bh[hh], wb[hh]) for hh in heads]
    nt = [_dot_tn(jnp.concatenate([u0b[hh], vv[hh]], axis=0), jnp.concatenate([bh[hh], kh[hh]], axis=0))
          for hh in heads]
    for hh in heads:
        s_ref[0, hh] = s_old[hh] * gall[hh] + _dot_nt(sb[hh], mt[hh]) + nt[hh]
    for p in range(N_PAIRS):
        outs = []
        for h in range(2):
            hh = 2 * p + h
            sl = hsl[h]
            mu = jnp.mean(y[hh], axis=-1, keepdims=True)
            var = jnp.mean(jnp.square(y[hh] - mu), axis=-1, keepdims=True)
            yn = (y[hh] - mu) * lax.rsqrt(var + LNX_EPS)
            rk = r_ref[p][:, sl] * k_ref[p][:, sl] * rk_ref[p][:, sl]
            bonus = jnp.sum(rk, axis=-1, keepdims=True) * v_ref[p][:, sl]
            outs.append(yn * lg_ref[p][:, sl] + lb_ref[p][:, sl] + bonus)
        y_ref[p] = jnp.concatenate(outs, axis=-1) * g_ref[p]


def _rwkv_chunks(prep, s0, lw, n_batch, seq, chunk):
    n_chunks = seq // chunk
    pairs = lambda a: a.reshape(N_PAIRS, 1, LANES).astype(F32)
    tile = pl.BlockSpec((N_PAIRS, chunk, LANES), lambda b, c: (0, b * n_chunks + c, 0))
    state = pl.BlockSpec((1, H_RWKV, HEAD_DIM, HEAD_DIM), lambda b, c: (b, 0, 0, 0))
    par = pl.BlockSpec((N_PAIRS, 1, LANES), lambda b, c: (0, 0, 0))
    return pl.pallas_call(
        functools.partial(_rwkv_lockstep_kernel, chunk=chunk),
        grid=(n_batch, n_chunks),
        in_specs=[tile] * 7 + [state, par, par, par],
        out_specs=[tile, state],
        out_shape=[jax.ShapeDtypeStruct((N_PAIRS, n_batch * seq, LANES), F32),
                   jax.ShapeDtypeStruct((n_batch, H_RWKV, HEAD_DIM, HEAD_DIM), F32)],
        compiler_params=_cparams("parallel", "arbitrary"),
        name="rwkv_chunks",
    )(*prep, s0.astype(F32), pairs(lw['lnx_g']), pairs(lw['lnx_b']), pairs(lw['r_k']))


def _rwkv(zr, shift0, s0, lw, n_batch, seq):
    z3 = zr.reshape(n_batch, seq, C_SHIFT)
    prev = jnp.concatenate([shift0[:, None].astype(F32), z3[:, :-1]], axis=1).reshape(n_batch * seq, C_SHIFT)
    prep = _rwkv_prep(zr, prev, lw, min(256, n_batch * seq))
    chunk = CHUNK if seq >= CHUNK else SHORT_CHUNK
    pad = (-seq) % chunk
    if pad:
        prep = [jnp.pad(a.reshape(N_PAIRS, n_batch, seq, LANES), ((0, 0), (0, 0), (0, pad), (0, 0)))
                .reshape(N_PAIRS, n_batch * (seq + pad), LANES) for a in prep]
    y, s_fin = _rwkv_chunks(prep, s0, lw, n_batch, seq + pad, chunk)
    if pad:
        y = y.reshape(N_PAIRS, n_batch, seq + pad, LANES)[:, :, :seq].reshape(N_PAIRS, n_batch * seq, LANES)
    return y, s_fin, z3[:, -1]


def _layer_norm(h, g, b):
    mu = jnp.mean(h, axis=-1, keepdims=True)
    d = h - mu
    var = jnp.mean(d * d, axis=-1, keepdims=True)
    return d * lax.rsqrt(var + LN_EPS) * g + b


def _out_proj_kernel(oa_ref, or_ref, x_ref, wa_ref, wr_ref, g_ref, b_ref, rw_ref, x1_ref, x1b_ref, sc_ref):
    orw = jnp.concatenate([or_ref[p] for p in range(N_PAIRS)], axis=-1)
    mix = _dot(oa_ref[...], wa_ref[...]) + _dot(orw, wr_ref[...])
    x1 = _layer_norm(ALPHA * x_ref[...] + mix, g_ref[...], b_ref[...])
    x1_ref[...] = x1
    x1b_ref[...] = x1.astype(BF16)
    sc_ref[...] = _sigmoid(lax.dot_general(rw_ref[...], x1, (((1,), (1,)), ((), ())),
                                           precision=lax.Precision.HIGHEST, preferred_element_type=F32))


def _out_proj(o_attn, o_rwkv, x, lw, tm):
    n = x.shape[0]
    wo = lw['w_out'].astype(BF16)
    full = lambda shape: pl.BlockSpec(shape, lambda i: (0,) * len(shape))
    return pl.pallas_call(
        _out_proj_kernel,
        grid=(n // tm,),
        in_specs=[pl.BlockSpec((tm, D_ATTN), lambda i: (i, 0)),
                  pl.BlockSpec((N_PAIRS, tm, LANES), lambda i: (0, i, 0)),
                  pl.BlockSpec((tm, D_MODEL), lambda i: (i, 0)),
                  full((D_ATTN, D_MODEL)), full((D_RWKV, D_MODEL)), full((1, D_MODEL)), full((1, D_MODEL)),
                  full((N_EXPERTS, D_MODEL))],
        out_specs=[pl.BlockSpec((tm, D_MODEL), lambda i: (i, 0)), pl.BlockSpec((tm, D_MODEL), lambda i: (i, 0)),
                   pl.BlockSpec((N_EXPERTS, tm), lambda i: (0, i))],
        out_shape=[jax.ShapeDtypeStruct((n, D_MODEL), F32), jax.ShapeDtypeStruct((n, D_MODEL), BF16),
                   jax.ShapeDtypeStruct((N_EXPERTS, n), F32)],
        compiler_params=_cparams("parallel"),
        name="out_proj_ln_router",
    )(o_attn, o_rwkv, x, wo[:D_ATTN], wo[D_ATTN:], lw['ln1_g'].reshape(1, -1), lw['ln1_b'].reshape(1, -1),
      lw['router_w'].astype(F32).T)


def _expert_kernel(be_ref, na_ref, x_ref, wg_ref, wu_ref, wd_ref, *rest):
    y_ref, wg_s, wu_s, wd_s = rest[-4:]
    g = pl.program_id(0)
    active = g < na_ref[0]

    @pl.when(active & ((g == 0) | (be_ref[g] != be_ref[jnp.maximum(g - 1, 0)])))
    def _():
        wg_s[...] = wg_ref[0].astype(BF16)
        wu_s[...] = wu_ref[0].astype(BF16)
        wd_s[...] = wd_ref[0].astype(BF16)

    @pl.when(active)
    def _():
        x = x_ref[...]
        hg = jnp.dot(x, wg_s[...], preferred_element_type=F32)
        hu = jnp.dot(x, wu_s[...], preferred_element_type=F32)
        h = hg * _sigmoid(hg) * hu
        y_ref[...] = jnp.dot(h.astype(BF16), wd_s[...], preferred_element_type=F32).astype(y_ref.dtype)

    @pl.when(g >= na_ref[0])
    def _():
        y_ref[...] = jnp.zeros_like(y_ref)


def _expert_blocks(xs, blk_e, n_active, wg, wu, wd, tm, out_dtype, after):
    rows = xs.shape[0]
    n_blocks = rows // tm
    d_e = wg.shape[-1]
    return pl.pallas_call(
        _expert_kernel,
        grid_spec=pltpu.PrefetchScalarGridSpec(
            num_scalar_prefetch=2,
            grid=(n_blocks,),
            in_specs=[pl.BlockSpec((tm, D_MODEL), lambda g, be, na: (g, 0)),
                      pl.BlockSpec((1, D_MODEL, d_e), lambda g, be, na: (be[g], 0, 0)),
                      pl.BlockSpec((1, D_MODEL, d_e), lambda g, be, na: (be[g], 0, 0)),
                      pl.BlockSpec((1, d_e, D_MODEL), lambda g, be, na: (be[g], 0, 0))]
            + [pl.BlockSpec(memory_space=pl.ANY)] * len(after),
            out_specs=pl.BlockSpec((tm, D_MODEL), lambda g, be, na: (g, 0)),
            scratch_shapes=[pltpu.VMEM((D_MODEL, d_e), BF16), pltpu.VMEM((D_MODEL, d_e), BF16),
                            pltpu.VMEM((d_e, D_MODEL), BF16)],
        ),
        out_shape=jax.ShapeDtypeStruct((rows, D_MODEL), out_dtype),
        compiler_params=_cparams("arbitrary"),
        name="expert_blocks",
    )(blk_e, n_active, xs, wg, wu, wd, *after)


def _first_max(x, idx, n):
    m = jnp.max(x, axis=0, keepdims=True)
    return m, jnp.min(jnp.where(x == m, idx, n), axis=0, keepdims=True)


def _route_kernel(sc_ref, bias_ref, tri_ref, e_ref, gate_ref, rank_ref, cnt_ref, carry_s):
    i = pl.program_id(0)
    tr = sc_ref.shape[1]
    per_group = N_EXPERTS // N_GROUPS

    @pl.when(i == 0)
    def _():
        carry_s[...] = jnp.zeros_like(carry_s)

    sc = sc_ref[...]
    choice = sc + bias_ref[...]
    in_g = lax.broadcasted_iota(jnp.int32, (per_group, tr), 0)
    g_iota = lax.broadcasted_iota(jnp.int32, (N_GROUPS, tr), 0)
    grp = jnp.zeros((N_GROUPS, tr), F32)
    for gi in range(N_GROUPS):
        cg = choice[gi * per_group:(gi + 1) * per_group]
        m1, i1 = _first_max(cg, in_g, per_group)
        m2 = jnp.max(jnp.where(in_g == i1, NEG_INF, cg), axis=0, keepdims=True)
        grp = jnp.where(g_iota == gi, m1 + m2, grp)
    keep = jnp.zeros((N_GROUPS, tr), F32)
    for _ in range(TOPK_GROUPS):
        _, f = _first_max(grp, g_iota, N_GROUPS)
        hit = g_iota == f
        keep = jnp.where(hit, 1.0, keep)
        grp = jnp.where(hit, NEG_INF, grp)
    masked = jnp.concatenate(
        [jnp.where(keep[gi:gi + 1] > 0.0, choice[gi * per_group:(gi + 1) * per_group], NEG_INF)
         for gi in range(N_GROUPS)], axis=0)
    e_iota = lax.broadcasted_iota(jnp.int32, (N_EXPERTS, tr), 0)
    hits, es, ss = [], [], []
    onehot = jnp.zeros((N_EXPERTS, tr), F32)
    for _ in range(TOP_K):
        _, f = _first_max(masked, e_iota, N_EXPERTS)
        hit = e_iota == f
        hits.append(hit)
        es.append(f)
        ss.append(jnp.sum(jnp.where(hit, sc, 0.0), axis=0, keepdims=True))
        onehot = jnp.where(hit, 1.0, onehot)
        masked = jnp.where(hit, NEG_INF, masked)
    total = ss[0]
    for s in ss[1:]:
        total = total + s
    e_ref[...] = jnp.concatenate(es, axis=0)
    gate_ref[...] = jnp.concatenate([s / total * ROUTED_SCALE for s in ss], axis=0)
    before = carry_s[...] + jnp.dot(onehot.astype(BF16), tri_ref[...], preferred_element_type=F32)
    rank_ref[...] = jnp.concatenate(
        [jnp.sum(jnp.where(h, before, 0.0), axis=0, keepdims=True) for h in hits], axis=0).astype(jnp.int32)
    carry = carry_s[...] + jnp.sum(onehot, axis=1, keepdims=True)
    carry_s[...] = carry
    cnt_ref[...] = jnp.broadcast_to(carry, cnt_ref.shape).astype(jnp.int32)


def _route(scores_t, router_bias, tok0, n):
    tr = min(256, n)
    assert tok0 % tr == 0 and n % tr == 0
    tri = jnp.asarray(np.arange(tr)[:, None] < np.arange(tr)[None, :], BF16)
    col = pl.BlockSpec((TOP_K, tr), lambda i: (0, i))
    top_e, gate, rank, counts = pl.pallas_call(
        _route_kernel,
        grid=(n // tr,),
        in_specs=[pl.BlockSpec((N_EXPERTS, tr), lambda i: (0, i + tok0 // tr)),
                  pl.BlockSpec((N_EXPERTS, 1), lambda i: (0, 0)),
                  pl.BlockSpec((tr, tr), lambda i: (0, 0))],
        out_specs=[col, col, col, pl.BlockSpec((N_EXPERTS, LANES), lambda i: (0, 0))],
        out_shape=[jax.ShapeDtypeStruct((TOP_K, n), jnp.int32), jax.ShapeDtypeStruct((TOP_K, n), F32),
                   jax.ShapeDtypeStruct((TOP_K, n), jnp.int32), jax.ShapeDtypeStruct((N_EXPERTS, LANES), jnp.int32)],
        scratch_shapes=[pltpu.VMEM((N_EXPERTS, 1), F32)],
        compiler_params=_cparams("arbitrary"),
        name="route",
    )(scores_t, router_bias.astype(F32).reshape(N_EXPERTS, 1), tri)
    counts = counts[:, 0]
    pcounts = (counts + MOE_TM - 1) // MOE_TM * MOE_TM
    pends = jnp.cumsum(pcounts)
    starts = pends - pcounts
    dest = rank
    for e in range(N_EXPERTS):
        dest = dest + jnp.where(top_e == e, starts[e], 0)
    n_blocks = -(-(n * TOP_K) // MOE_TM) + N_EXPERTS
    slot_tok = jnp.zeros((n_blocks * MOE_TM,), jnp.int32).at[dest.reshape(-1)].set(
        jnp.tile(jnp.arange(tok0, tok0 + n, dtype=jnp.int32), TOP_K))
    blk_start = jnp.arange(n_blocks, dtype=jnp.int32) * MOE_TM
    blk_e = jnp.minimum(jnp.sum((pends[None, :] <= blk_start[:, None]).astype(jnp.int32), axis=1), N_EXPERTS - 1)
    n_active = (pends[-1] // MOE_TM).astype(jnp.int32).reshape(1)
    return gate.T, dest.astype(jnp.int32), slot_tok, blk_e, n_active


def _combine_kernel(x1_ref, sh_ref, gate_ref, g_ref, b_ref, *rest, tiles_per_part):
    ys_refs, o_ref = rest[:-1], rest[-1]
    i = pl.program_id(0)
    gate = gate_ref[...]
    for part, ys_ref in enumerate(ys_refs):
        @pl.when(i // tiles_per_part == part)
        def _():
            f = sh_ref[...]
            for j in range(TOP_K):
                f = f + ys_ref[j].astype(F32) * gate[:, j:j + 1]
            o_ref[...] = _layer_norm(ALPHA * x1_ref[...] + f, g_ref[...], b_ref[...])


def _combine(x1, shared, ysgs, gate, lw, tm):
    n = x1.shape[0]
    tiles_per_part = n // len(ysgs) // tm
    row = pl.BlockSpec((tm, D_MODEL), lambda i: (i, 0))
    par = pl.BlockSpec((1, D_MODEL), lambda i: (0, 0))

    def part_spec(part):
        return pl.BlockSpec((TOP_K, tm, D_MODEL),
                            lambda i: (0, jnp.clip(i - part * tiles_per_part, 0, tiles_per_part - 1), 0))

    return pl.pallas_call(
        functools.partial(_combine_kernel, tiles_per_part=tiles_per_part),
        grid=(n // tm,),
        in_specs=[row, row, pl.BlockSpec((tm, TOP_K), lambda i: (i, 0)), par, par]
        + [part_spec(part) for part in range(len(ysgs))],
        out_specs=row,
        out_shape=jax.ShapeDtypeStruct((n, D_MODEL), F32),
        compiler_params=_cparams("arbitrary"),
        name="moe_combine_ln",
    )(x1, shared, gate, lw['ln2_g'].reshape(1, -1), lw['ln2_b'].reshape(1, -1), *ysgs)


def _moe(x1, x1b, scores_t, lw, ew, n_parts, while_gathering):
    n = x1.shape[0]
    n_part = n // n_parts
    routes = [_route(scores_t, lw['router_bias'], part * n_part, n_part) for part in range(n_parts)]
    wait_for = while_gathering(routes[0][2])
    ysgs, ys = [], None
    for part, (_, dest, slot_tok, blk_e, n_active) in enumerate(routes):
        xs = x1b[slot_tok]
        after = [routes[part + 1][2]] if part + 1 < n_parts else []
        after = after + ([ys] if ys is not None else wait_for)
        ys = _expert_blocks(xs, blk_e, n_active, ew['e_gate'], ew['e_up'], ew['e_down'], MOE_TM, BF16, after)
        ysgs.append(ys[dest.reshape(-1)].reshape(TOP_K, n_part, D_MODEL))
    tm_s = min(MOE_TM, n)
    shared = _expert_blocks(x1b, jnp.zeros((n // tm_s,), jnp.int32), jnp.full((1,), n // tm_s, jnp.int32),
                            ew['s_gate'], ew['s_up'], ew['s_down'], tm_s, F32, [ys])
    gate = jnp.concatenate([r[0] for r in routes], axis=0)
    return _combine(x1, shared, ysgs, gate, lw, min(128, n_part))


def _copy_kernel(x_ref, after_ref, o_ref):
    del after_ref
    o_ref[...] = x_ref[...]


def _order_after(x, after):
    return pl.pallas_call(
        _copy_kernel,
        in_specs=[pl.BlockSpec(memory_space=pltpu.VMEM), pl.BlockSpec(memory_space=pl.ANY)],
        out_specs=pl.BlockSpec(memory_space=pltpu.VMEM),
        out_shape=jax.ShapeDtypeStruct(x.shape, x.dtype),
        name="order_after",
    )(x, after)


def _project(x, w_in_b):
    n = x.shape[0] * x.shape[1]
    xb = x.reshape(n, D_MODEL).astype(BF16)
    tm = min(512, n)
    return _matmul(xb, w_in_b[:, :3 * D_ATTN], tm, 768), _matmul(xb, w_in_b[:, 3 * D_ATTN:], tm, C_SHIFT // 2)


def _layer(x, zqkv, zr, o_attn, shift0, s0, lw, ew, while_gathering=lambda slots: []):
    n_batch, seq, _ = x.shape
    n = n_batch * seq
    o_rwkv, s_new, shift_new = _rwkv(zr, shift0, s0, lw, n_batch, seq)
    x1, x1b, scores = _out_proj(o_attn, o_rwkv, x.reshape(n, D_MODEL), lw, min(256, n))
    y = _moe(x1, x1b, scores, lw, ew, MOE_TOKEN_PARTS, while_gathering)
    k = zqkv[:, D_ATTN:2 * D_ATTN].reshape(n_batch, seq, H_ATTN, HEAD_DIM)
    v = zqkv[:, 2 * D_ATTN:].reshape(n_batch, seq, H_ATTN, HEAD_DIM)
    return y.reshape(n_batch, seq, D_MODEL), k, v, s_new, shift_new


def kernel(x_prompt, x_sample, cache_k, cache_v, state_wkv, state_shift, page_table, rel_bias, w_in, mu_shift, w0, w2,
           a0, a2, g2, k_k, k_a, r_k, lnx_g, lnx_b, w_out, ln1_g, ln1_b, router_w, router_bias, e_gate, e_up, e_down,
           s_gate, s_up, s_down, ln2_g, ln2_b):
    assert w_in.shape[0] == DEPTH == 1
    n_b, seq, _ = x_prompt.shape
    db, dec_seq, _ = x_sample.shape
    past_len = page_table.shape[1] * PAGE_SIZE
    lw = {'mu_shift': mu_shift[0], 'w0': w0[0], 'w2': w2[0], 'a0': a0[0], 'a2': a2[0], 'g2': g2[0], 'k_k': k_k[0],
          'k_a': k_a[0], 'r_k': r_k[0], 'lnx_g': lnx_g[0], 'lnx_b': lnx_b[0], 'w_out': w_out[0], 'ln1_g': ln1_g[0],
          'ln1_b': ln1_b[0], 'router_w': router_w[0], 'router_bias': router_bias[0], 'ln2_g': ln2_g[0],
          'ln2_b': ln2_b[0]}
    ew = {'e_gate': e_gate[0], 'e_up': e_up[0], 'e_down': e_down[0],
          's_gate': s_gate, 's_up': s_up, 's_down': s_down}
    w_in_b = w_in[0].astype(BF16)

    zqkv_s, zr_s = _project(x_sample, w_in_b)
    idx = _sample_select(zqkv_s, cache_k, page_table, db, dec_seq, past_len)
    zqkv_p, zr_p = _project(x_prompt, w_in_b)
    o_attn_p = _moba_prompt(zqkv_p, _order_after(rel_bias, idx), n_b, seq)
    sample_attn = []

    def attend_sample(slots):
        sample_attn.append(
            _sample_attend(zqkv_s, idx, cache_k, cache_v, page_table, rel_bias, db, dec_seq, past_len, slots))
        return sample_attn

    y_p, k_p, v_p, w_p, s_p = _layer(
        x_prompt, zqkv_p, zr_p, o_attn_p, jnp.zeros((n_b, C_SHIFT), F32),
        jnp.zeros((n_b, H_RWKV, HEAD_DIM, HEAD_DIM), F32), lw, ew, attend_sample)
    y_s, k_s, v_s, w_s, s_s = _layer(x_sample, zqkv_s, zr_s, sample_attn[0], state_shift[0], state_wkv[0], lw, ew)
    return (y_p, y_s, k_p[None], v_p[None], w_p[None], s_p[None], k_s[None], v_s[None], w_s[None], s_s[None])
```

```python
import functools
import math

import numpy as np
import jax
import jax.numpy as jnp
from jax import lax
from jax.experimental import pallas as pl
from jax.experimental.pallas import tpu as pltpu

F32 = jnp.float32
BF16 = jnp.bfloat16

D_MODEL = 2048
HEAD_DIM = 64
D_ATTN = 1024
D_RWKV = 1024
H_ATTN = 16
H_RWKV = 16
N_PAIRS = 8
LANES = 128
BLOCK = 256
TOP_BLOCKS = 3
PAGE_SIZE = 128
N_BUCKETS = 32
MAX_DISTANCE = 128
DECAY_LORA = 64
AAA_LORA = 64
GATE_LORA = 128
C_SHIFT = 3 * D_RWKV + DECAY_LORA + AAA_LORA + GATE_LORA
LNX_EPS = 64e-5
LN_EPS = 1e-5
N_EXPERTS = 64
TOP_K = 8
N_GROUPS = 8
TOPK_GROUPS = 4
D_EXPERT = 512
ROUTED_SCALE = 2.5
DEPTH = 1
ALPHA = (2.0 * DEPTH) ** 0.25
CHUNK = 64
MOE_TM = 256
MOE_TOKEN_PARTS = 1
SHORT_CHUNK = 16
FAR_BLOCKS_PER_STEP = 2
VMEM_LIMIT = 56 * 1024 * 1024
NEG_INF = float("-inf")
LOG2E = math.log2(math.e)


def _cparams(*sem):
    return pltpu.CompilerParams(dimension_semantics=sem, vmem_limit_bytes=VMEM_LIMIT)


def _dot(a, b):
    return jnp.dot(a.astype(BF16), b.astype(BF16), preferred_element_type=F32)


def _dot_nt(a, b):
    return lax.dot_general(a.astype(BF16), b.astype(BF16), (((1,), (1,)), ((), ())), preferred_element_type=F32)


def _dot_tn(a, b):
    return lax.dot_general(a.astype(BF16), b.astype(BF16), (((0,), (0,)), ((), ())), preferred_element_type=F32)


def _dot_hi(a, b):
    return jnp.dot(a, b, precision=lax.Precision.HIGHEST, preferred_element_type=F32)


def _sigmoid(x):
    return 1.0 / (1.0 + jnp.exp(-x))


def _mm_kernel(x_ref, w_ref, o_ref):
    o_ref[...] = jnp.dot(x_ref[...], w_ref[...], preferred_element_type=F32)


def _matmul(x, w, tm, tn):
    m, k = x.shape
    n = w.shape[1]
    return pl.pallas_call(
        _mm_kernel,
        grid=(m // tm, n // tn),
        in_specs=[pl.BlockSpec((tm, k), lambda i, j: (i, 0)),
                  pl.BlockSpec((k, tn), lambda i, j: (0, j))],
        out_specs=pl.BlockSpec((tm, tn), lambda i, j: (i, j)),
        out_shape=jax.ShapeDtypeStruct((m, n), F32),
        compiler_params=_cparams("parallel", "parallel"),
        name="in_proj",
    )(x, w)


def _qkv_kernel(x_ref, w_ref, o_ref, kt_ref, vt_ref):
    j = pl.program_id(1)
    z = jnp.dot(x_ref[...], w_ref[...], preferred_element_type=F32)
    o_ref[...] = z

    @pl.when(j == 1)
    def _():
        kt_ref[0] = z.T

    @pl.when(j == 2)
    def _():
        vt_ref[0] = z.T


def _matmul_qkv(x, w, tm, n_batch, seq):
    m, k = x.shape
    tiles_per_seq = seq // tm
    assert seq % tm == 0 and w.shape[1] == 3 * D_ATTN
    t_spec = pl.BlockSpec((1, D_ATTN, tm), lambda i, j: (i // tiles_per_seq, 0, i % tiles_per_seq))
    t_shape = jax.ShapeDtypeStruct((n_batch, D_ATTN, seq), F32)
    return pl.pallas_call(
        _qkv_kernel,
        grid=(m // tm, 3),
        in_specs=[pl.BlockSpec((tm, k), lambda i, j: (i, 0)),
                  pl.BlockSpec((k, D_ATTN), lambda i, j: (0, j))],
        out_specs=[pl.BlockSpec((tm, D_ATTN), lambda i, j: (i, j)), t_spec, t_spec],
        out_shape=[jax.ShapeDtypeStruct((m, 3 * D_ATTN), F32), t_shape, t_shape],
        compiler_params=_cparams("parallel", "arbitrary"),
        name="in_proj_qkv",
    )(x, w)


def _bucket_table(max_dist):
    n = np.arange(max_dist + 1)
    max_exact = N_BUCKETS // 2
    nf = np.maximum(n, 1).astype(np.float64)
    large = max_exact + (np.log(nf / max_exact) / math.log(MAX_DISTANCE / max_exact)
                         * (N_BUCKETS - max_exact)).astype(np.int64)
    large = np.minimum(large, N_BUCKETS - 1)
    return np.where(n < max_exact, n, large).astype(np.int32)


def _relative_bias_log2(rel_bias):
    rb = rel_bias.astype(F32)
    return (rb - rb[N_BUCKETS - 1:]) * LOG2E


def _prompt_bucket_tiles():
    kpos = np.arange(BLOCK)[:, None]
    qpos = np.arange(BLOCK)[None, :]
    bt = _bucket_table(2 * BLOCK)
    d_own = qpos - kpos
    own = np.where(d_own >= 0, bt[np.maximum(d_own, 0)], -1)
    return jnp.asarray(np.stack([own, bt[BLOCK + qpos - kpos]]).astype(np.int32))


def _moba_prompt_kernel(rb_ref, q_ref, k_ref, v_ref, bkt_ref, o_ref, kb_s, vt_s, km_s, sel_s, bias_s, sa_s, sb_s,
                        *, n_blk):
    p = pl.program_id(1)
    i = pl.program_id(2)

    @pl.when(i == 0)
    def _():
        for kind in range(2):
            bk = bkt_ref[kind]
            for h in range(2):
                t = jnp.zeros((BLOCK, BLOCK), F32)
                for b in range(N_BUCKETS):
                    t = jnp.where(bk == b, rb_ref[b * H_ATTN + 2 * p + h], t)
                bias_s[h, kind] = jnp.where(bk < 0, NEG_INF, t)

        def fill(j, c):
            r0 = pl.multiple_of(j * BLOCK, BLOCK)
            kblk = k_ref[pl.ds(r0, BLOCK), :]
            vblk = v_ref[pl.ds(r0, BLOCK), :]
            kb_s[pl.ds(r0, BLOCK), :] = kblk.astype(BF16)
            vt_s[j] = vblk.T.astype(BF16)
            km_s[pl.ds(j, 1), :] = jnp.sum(kblk, axis=0, keepdims=True) * (1.0 / BLOCK)
            return c
        lax.fori_loop(0, n_blk, fill, 0)

    qt = q_ref[...].T
    row = lax.broadcasted_iota(jnp.int32, (LANES, BLOCK), 0)
    head0 = row < HEAD_DIM
    qts = [jnp.where(head0, qt, 0.0), jnp.where(head0, 0.0, qt)]
    qtb = [(x * (HEAD_DIM ** -0.5 * LOG2E)).astype(BF16) for x in qts]

    blk = lax.broadcasted_iota(jnp.int32, (n_blk, BLOCK), 0)
    valid = blk < i
    km = km_s[...]
    for h in range(2):
        g = jnp.where(valid, _dot_hi(km, qts[h]), NEG_INF)
        sel = jnp.zeros((n_blk, BLOCK), F32)
        for _ in range(TOP_BLOCKS):
            m = jnp.max(g, axis=0, keepdims=True)
            first = jnp.min(jnp.where(g == m, blk, n_blk), axis=0, keepdims=True)
            hit = blk == first
            sel = jnp.where(hit & valid, 1.0, sel)
            g = jnp.where(hit, NEG_INF, g)
        sel_s[h] = sel

    def scores(j, h):
        r0 = pl.multiple_of(j * BLOCK, BLOCK)
        return jnp.dot(kb_s[pl.ds(r0, BLOCK), :], qtb[h], preferred_element_type=F32)

    def pv(j, pb0, pb1):
        vt = vt_s[j]
        return jnp.concatenate(
            [jnp.dot(vt[:HEAD_DIM], pb0, preferred_element_type=F32),
             jnp.dot(vt[HEAD_DIM:], pb1, preferred_element_type=F32)], axis=0)

    def update(blocks, carry):
        m0, m1, l0, l1, acc = carry
        new = []
        for h, (m, l) in enumerate(((m0, l0), (m1, l1))):
            mn = m
            for blk_ in blocks:
                mn = jnp.where(blk_[3 + h], jnp.maximum(mn, jnp.max(blk_[1 + h], axis=0, keepdims=True)), mn)
            a = jnp.exp2(m - mn)
            ln = a * l
            pbs = []
            for blk_ in blocks:
                pr = jnp.exp2(blk_[1 + h] - jnp.where(blk_[3 + h], mn, jnp.inf))
                ln = ln + jnp.sum(pr, axis=0, keepdims=True)
                pbs.append(pr.astype(BF16))
            new.append((mn, a, ln, pbs))
        acc = jnp.where(head0, new[0][1], new[1][1]) * acc
        for n, blk_ in enumerate(blocks):
            acc = acc + pv(blk_[0], new[0][3][n], new[1][3][n])
        return new[0][0], new[1][0], new[0][2], new[1][2], acc

    jp = jnp.maximum(i - 1, 0)
    has_prev = i > 0
    keep0 = (sel_s[0, pl.ds(jp, 1), :] > 0.0) & has_prev
    keep1 = (sel_s[1, pl.ds(jp, 1), :] > 0.0) & has_prev
    always = jnp.full((1, BLOCK), True)
    row_init = jnp.full((1, BLOCK), NEG_INF, F32)
    carry = (row_init, row_init, jnp.zeros((1, BLOCK), F32), jnp.zeros((1, BLOCK), F32),
             jnp.zeros((LANES, BLOCK), F32))
    n_far = jnp.maximum(i - 1, 0)

    def issue_scores(t, dst):
        for u in range(FAR_BLOCKS_PER_STEP):
            jc = jnp.minimum(t * FAR_BLOCKS_PER_STEP + u, n_blk - 1)
            for h in range(2):
                dst[u, h] = scores(jc, h)

    def consume(t, src, carry):
        blocks = []
        for u in range(FAR_BLOCKS_PER_STEP):
            j = t * FAR_BLOCKS_PER_STEP + u
            in_range = j < n_far
            jc = jnp.minimum(j, n_blk - 1)
            blocks.append((jc, src[u, 0], src[u, 1],
                           (sel_s[0, pl.ds(jc, 1), :] > 0.0) & in_range,
                           (sel_s[1, pl.ds(jc, 1), :] > 0.0) & in_range))
        return update(blocks, carry)

    issue_scores(0, sa_s)
    carry = update([(i, scores(i, 0) + bias_s[0, 0], scores(i, 1) + bias_s[1, 0], always, always),
                    (jp, scores(jp, 0) + bias_s[0, 1], scores(jp, 1) + bias_s[1, 1], keep0, keep1)], carry)

    def far_body(it, carry):
        issue_scores(2 * it + 1, sb_s)
        carry = consume(2 * it, sa_s, carry)
        issue_scores(2 * it + 2, sa_s)
        return consume(2 * it + 1, sb_s, carry)

    n_iter = (n_far + 2 * FAR_BLOCKS_PER_STEP - 1) // (2 * FAR_BLOCKS_PER_STEP)
    m0, m1, l0, l1, acc = lax.fori_loop(0, n_iter, far_body, carry)
    ot = acc / jnp.where(head0, l0, l1)
    o_ref[...] = ot.T


def _moba_prompt(zqkv, rel_bias, n_batch, seq):
    n_blk = seq // BLOCK
    kern = functools.partial(_moba_prompt_kernel, n_blk=n_blk)
    stage = pltpu.VMEM((FAR_BLOCKS_PER_STEP, 2, BLOCK, BLOCK), F32)
    return pl.pallas_call(
        kern,
        grid=(n_batch, N_PAIRS, n_blk),
        in_specs=[
            pl.BlockSpec(memory_space=pltpu.SMEM),
            pl.BlockSpec((BLOCK, LANES), lambda b, p, i: (b * n_blk + i, p)),
            pl.BlockSpec((seq, LANES), lambda b, p, i: (b, N_PAIRS + p)),
            pl.BlockSpec((seq, LANES), lambda b, p, i: (b, 2 * N_PAIRS + p)),
            pl.BlockSpec((2, BLOCK, BLOCK), lambda b, p, i: (0, 0, 0)),
        ],
        out_specs=pl.BlockSpec((BLOCK, LANES), lambda b, p, i: (b * n_blk + i, p)),
        out_shape=jax.ShapeDtypeStruct((n_batch * seq, D_ATTN), F32),
        scratch_shapes=[pltpu.VMEM((seq, LANES), BF16),
                        pltpu.VMEM((n_blk, LANES, BLOCK), BF16),
                        pltpu.VMEM((n_blk, LANES), F32),
                        pltpu.VMEM((2, n_blk, BLOCK), F32),
                        pltpu.VMEM((2, 2, BLOCK, BLOCK), F32),
                        stage, stage],
        compiler_params=_cparams("parallel", "parallel", "arbitrary"),
        name="moba_prompt",
    )(_relative_bias_log2(rel_bias).reshape(-1), zqkv, zqkv, zqkv, _prompt_bucket_tiles())


PAGES_PER_STEP = 8


def _split_bf16(x):
    hi = x.astype(BF16)
    return hi, (x - hi.astype(F32)).astype(BF16)


def _sample_gate_kernel(pt_ref, q_ref, seg_ref, *refs, n_past_blk, n_steps, dec_seq):
    page_refs = refs[:PAGES_PER_STEP]
    idx_ref = refs[PAGES_PER_STEP]
    sum_s = refs[PAGES_PER_STEP + 1]
    s = pl.program_id(1)
    ppb = BLOCK // PAGE_SIZE
    ones = jnp.ones((16, PAGE_SIZE), BF16)
    for bl in range(PAGES_PER_STEP // ppb):
        x = page_refs[ppb * bl][0]
        for e in range(1, ppb):
            x = x + page_refs[ppb * bl + e][0]
        hi, lo = _split_bf16(x)
        tot = _dot_nt(ones, hi) + _dot_nt(ones, lo)
        sum_s[pl.ds(s * (PAGES_PER_STEP // ppb) + bl, 1), :] = tot[:1]

    @pl.when(s == n_steps - 1)
    def _():
        blk = lax.broadcasted_iota(jnp.int32, (n_past_blk, LANES), 0)
        ksum = sum_s[...]
        for qi in range(dec_seq):
            hi, lo = _split_bf16(ksum * q_ref[0, qi:qi + 1, :])
            g = _dot(hi, seg_ref[...]) + _dot(lo, seg_ref[...])
            rows = []
            for _ in range(TOP_BLOCKS):
                m = jnp.max(g, axis=0, keepdims=True)
                first = jnp.min(jnp.where(g == m, blk, n_past_blk), axis=0, keepdims=True)
                rows.append(first)
                g = jnp.where(blk == first, NEG_INF, g)
            rows.append(jnp.zeros((8 - TOP_BLOCKS, LANES), jnp.int32))
            idx_ref[0, qi] = jnp.concatenate(rows, axis=0)


def _sample_gate(page_table, q8, cache_k_pages, n_past_blk, dec_seq):
    db = q8.shape[0]
    n_pages = page_table.shape[1]
    n_steps = n_pages // PAGES_PER_STEP
    kern = functools.partial(_sample_gate_kernel, n_past_blk=n_past_blk, n_steps=n_steps, dec_seq=dec_seq)
    col = np.arange(H_ATTN * HEAD_DIM)[:, None] // HEAD_DIM
    seg = jnp.asarray(col == np.arange(LANES)[None, :], BF16)

    def page_spec(r):
        return pl.BlockSpec((1, H_ATTN * HEAD_DIM, PAGE_SIZE),
                            lambda b, s, pt: (pt[b, s * PAGES_PER_STEP + r], 0, 0))

    return pl.pallas_call(
        kern,
        grid_spec=pltpu.PrefetchScalarGridSpec(
            num_scalar_prefetch=1,
            grid=(db, n_steps),
            in_specs=[pl.BlockSpec((1, 8, H_ATTN * HEAD_DIM), lambda b, s, pt: (b, 0, 0)),
                      pl.BlockSpec((H_ATTN * HEAD_DIM, LANES), lambda b, s, pt: (0, 0))]
            + [page_spec(r) for r in range(PAGES_PER_STEP)],
            out_specs=pl.BlockSpec((1, dec_seq, 8, LANES), lambda b, s, pt: (b, 0, 0, 0)),
            scratch_shapes=[pltpu.VMEM((n_past_blk, H_ATTN * HEAD_DIM), F32)],
        ),
        out_shape=jax.ShapeDtypeStruct((db, dec_seq, 8, LANES), jnp.int32),
        compiler_params=_cparams("parallel", "arbitrary"),
        name="sample_gate",
    )(page_table, q8, seg, *([cache_k_pages] * PAGES_PER_STEP))


def _sample_attn_kernel(slab_ref, idx_ref, qt_ref, kn_ref, vn_ref, bown_ref, blast_ref, far_ref, *refs,
                        dec_seq, n_past_blk):
    n_slab = dec_seq * TOP_BLOCKS * (BLOCK // PAGE_SIZE)
    k_refs = refs[:n_slab]
    v_refs = refs[n_slab:2 * n_slab]
    o_ref = refs[2 * n_slab]
    b = pl.program_id(0)
    h = pl.program_id(1)
    ppb = BLOCK // PAGE_SIZE
    lane8 = lax.broadcasted_iota(jnp.int32, (HEAD_DIM, 8), 1)
    out = jnp.zeros((HEAD_DIM, 8), F32)
    kn = kn_ref[0, 0]
    vn = vn_ref[0, 0]
    far = far_ref[h]
    per_q = TOP_BLOCKS * ppb
    queries = range(dec_seq)
    qcs = [qt_ref[0, 0][:, qi:qi + 1] * (HEAD_DIM ** -0.5) for qi in queries]
    rows = []
    for qi in queries:
        for t in range(TOP_BLOCKS):
            sel_blk = idx_ref[((b * H_ATTN + h) * dec_seq + qi) * TOP_BLOCKS + t]
            is_last = sel_blk == n_past_blk - 1
            for e in range(ppb):
                slab = k_refs[qi * per_q + t * ppb + e][0]
                srow = jnp.sum(slab * qcs[qi], axis=0, keepdims=True)
                rows.append(srow + jnp.where(is_last, blast_ref[0, qi, e:e + 1, :], far))
    s_own = [jnp.sum(kn * qcs[qi], axis=0, keepdims=True) + bown_ref[0, qi:qi + 1, :] for qi in queries]
    row_max = [functools.reduce(jnp.maximum, rows[qi * per_q:(qi + 1) * per_q]) for qi in queries]
    m = [jnp.maximum(jnp.max(row_max[qi], axis=-1, keepdims=True), jnp.max(s_own[qi], axis=-1, keepdims=True))
         for qi in queries]
    p_own = [jnp.exp(s_own[qi] - m[qi]) for qi in queries]
    prs = [jnp.exp(rows[n] - m[n // per_q]) for n in range(len(rows))]
    l = [jnp.sum(functools.reduce(jnp.add, prs[qi * per_q:(qi + 1) * per_q]), axis=-1, keepdims=True)
         + jnp.sum(p_own[qi], axis=-1, keepdims=True) for qi in queries]
    acc = [functools.reduce(jnp.add, [v_refs[n][0] * prs[n] for n in range(qi * per_q, (qi + 1) * per_q)])
           for qi in queries]
    o = [jnp.sum(acc[qi], axis=-1, keepdims=True) + jnp.sum(vn * p_own[qi], axis=-1, keepdims=True)
         for qi in queries]
    for qi in queries:
        out = jnp.where(lane8 == qi, o[qi] / l[qi], out)
    o_ref[0, 0] = out


def _sample_attn(slabs, idx_flat, qt, knt, vnt, bias_own, bias_last, far, ck_slabs, cv_slabs, dec_seq, n_past_blk):
    db = qt.shape[0]
    ppb = BLOCK // PAGE_SIZE
    n_slab = dec_seq * TOP_BLOCKS * ppb
    kern = functools.partial(_sample_attn_kernel, dec_seq=dec_seq, n_past_blk=n_past_blk)

    def slab_spec(n):
        return pl.BlockSpec((1, HEAD_DIM, PAGE_SIZE),
                            lambda b, h, sl, ix: (sl[(b * H_ATTN + h) * n_slab + n], 0, 0))

    small = lambda shape: pl.BlockSpec(shape, lambda b, h, sl, ix: (b, h, 0, 0))
    return pl.pallas_call(
        kern,
        grid_spec=pltpu.PrefetchScalarGridSpec(
            num_scalar_prefetch=2,
            grid=(db, H_ATTN),
            in_specs=[small((1, 1, HEAD_DIM, 8)), small((1, 1, HEAD_DIM, 8)), small((1, 1, HEAD_DIM, 8)),
                      pl.BlockSpec((1, dec_seq, 8), lambda b, h, sl, ix: (h, 0, 0)),
                      pl.BlockSpec((1, dec_seq, ppb, PAGE_SIZE), lambda b, h, sl, ix: (h, 0, 0, 0)),
                      pl.BlockSpec(memory_space=pltpu.SMEM)]
            + [slab_spec(n) for n in range(n_slab)] * 2,
            out_specs=small((1, 1, HEAD_DIM, 8)),
        ),
        out_shape=jax.ShapeDtypeStruct((db, H_ATTN, HEAD_DIM, 8), F32),
        compiler_params=_cparams("parallel", "parallel"),
        name="sample_attn",
    )(slabs, idx_flat, qt, knt, vnt, bias_own, bias_last, far, *([ck_slabs] * n_slab), *([cv_slabs] * n_slab))


def _sample_select(zqkv_s, cache_k, page_table, db, dec_seq, past_len):
    n_past_blk = past_len // BLOCK
    assert past_len % BLOCK == 0 and n_past_blk >= TOP_BLOCKS
    pool = cache_k.shape[1]
    ckt = cache_k[0].transpose(0, 2, 3, 1)
    q = zqkv_s.reshape(db, dec_seq, 3 * D_ATTN)[..., :D_ATTN]
    q8 = jnp.pad(q, ((0, 0), (0, 8 - dec_seq), (0, 0)))
    idx = _sample_gate(page_table, q8, ckt.reshape(pool, H_ATTN * HEAD_DIM, PAGE_SIZE), n_past_blk, dec_seq)
    return idx[:, :, :TOP_BLOCKS, :H_ATTN].transpose(0, 3, 1, 2)


def _sample_attend(zqkv_s, idx, cache_k, cache_v, page_table, rel_bias, db, dec_seq, past_len, after):
    n_past_blk = past_len // BLOCK
    ppb = BLOCK // PAGE_SIZE
    pool = cache_k.shape[1]
    ckt = cache_k[0].transpose(0, 2, 3, 1)
    cvt = cache_v[0].transpose(0, 2, 3, 1)
    z3 = zqkv_s.reshape(db, dec_seq, 3 * D_ATTN)
    q = z3[..., :D_ATTN]
    pages = jnp.take_along_axis(page_table[:, None, None, None, :],
                                (idx[..., None] * ppb + jnp.arange(ppb)).reshape(db, H_ATTN, dec_seq, 1, -1),
                                axis=-1)
    slabs = (pages.reshape(db, H_ATTN, -1) * H_ATTN + jnp.arange(H_ATTN)[None, :, None]).reshape(-1).astype(jnp.int32)
    slabs = _order_after(slabs.reshape(-1, LANES), after).reshape(-1)

    def heads_t(t):
        t = t.reshape(db, dec_seq, H_ATTN, HEAD_DIM).transpose(0, 2, 3, 1)
        return jnp.pad(t, ((0, 0), (0, 0), (0, 0), (0, 8 - dec_seq)))

    bt = _bucket_table(BLOCK + dec_seq)
    rb = rel_bias.astype(F32)
    qi = np.arange(dec_seq)[:, None]
    t8 = np.arange(8)[None, :]
    own_ok = (t8 <= qi) & (t8 < dec_seq)
    bias_own = jnp.where(jnp.asarray(own_ok)[..., None], rb[bt[np.maximum(qi - t8, 0)]], NEG_INF).transpose(2, 0, 1)
    off = np.arange(BLOCK).reshape(ppb, PAGE_SIZE)[None]
    bias_last = rb[bt[BLOCK + qi[:, :, None] - off]].transpose(3, 0, 1, 2)
    o = _sample_attn(slabs, idx.reshape(-1).astype(jnp.int32), heads_t(q), heads_t(z3[..., D_ATTN:2 * D_ATTN]),
                     heads_t(z3[..., 2 * D_ATTN:]), bias_own, bias_last, rb[N_BUCKETS - 1],
                     ckt.reshape(pool * H_ATTN, HEAD_DIM, PAGE_SIZE), cvt.reshape(pool * H_ATTN, HEAD_DIM, PAGE_SIZE),
                     dec_seq, n_past_blk)
    return o[..., :dec_seq].transpose(0, 3, 1, 2).reshape(db * dec_seq, D_ATTN)


def _pair_ones():
    r = np.arange(LANES)
    return jnp.asarray((r[:, None] // HEAD_DIM) == (r[None, :] // HEAD_DIM), BF16)


def _head_sum(x, ones_pair):
    hi = x.astype(BF16)
    lo = (x - hi.astype(F32)).astype(BF16)
    outs = []
    for g in range(N_PAIRS):
        sl = slice(g * LANES, (g + 1) * LANES)
        outs.append(jnp.dot(hi[:, sl], ones_pair, preferred_element_type=F32)
                    + jnp.dot(lo[:, sl], ones_pair, preferred_element_type=F32))
    return jnp.concatenate(outs, axis=-1)


def _rwkv_prep_kernel(z_ref, prev_ref, mu_ref, w0_ref, w2_ref, a0_ref, a2_ref, g2_ref, kk_ref, ka_ref, ones_ref,
                      r_ref, ld_ref, k_ref, v_ref, kap_ref, b_ref, g_ref, *, prev_is_boundary_row):
    z = z_ref[...]
    if prev_is_boundary_row:
        first = lax.broadcasted_iota(jnp.int32, z.shape, 0) == 0
        prev = jnp.where(first, prev_ref[0], pltpu.roll(z, 1, axis=0))
    else:
        prev = prev_ref[...]
    zs = z + (prev - z) * mu_ref[...]
    o = D_RWKV
    r = zs[:, :o]
    k = zs[:, o:2 * o]
    v = zs[:, 2 * o:3 * o]
    lora = zs[:, 3 * o:3 * o + LANES]
    gl = zs[:, 3 * o + LANES:]
    x = w0_ref[...] + _dot(jnp.tanh(lora), w2_ref[...])
    nx = -x
    w = -(jnp.maximum(nx, 0.0) + jnp.log(1.0 + jnp.exp(-jnp.abs(nx)))) - 0.5
    logd = -jnp.exp(w)
    a = _sigmoid(a0_ref[...] + _dot(lora, a2_ref[...]))
    g = _dot(_sigmoid(gl), g2_ref[...])
    kk = k * kk_ref[...]
    nrm = jnp.sqrt(_head_sum(kk * kk, ones_ref[...]))
    kap = kk / jnp.maximum(nrm, 1e-12)
    k2 = k * (1.0 + (a - 1.0) * ka_ref[...])
    for p in range(N_PAIRS):
        sl = slice(p * LANES, (p + 1) * LANES)
        r_ref[p] = r[:, sl]
        ld_ref[p] = logd[:, sl]
        k_ref[p] = k2[:, sl]
        v_ref[p] = v[:, sl]
        kap_ref[p] = kap[:, sl]
        b_ref[p] = (kap * a)[:, sl]
        g_ref[p] = g[:, sl]


def _rwkv_prep(zr, prev, lw, tm):
    n = zr.shape[0]
    boundary = prev.ndim == 3
    prev_spec = (pl.BlockSpec((1, 1, C_SHIFT), lambda i: (i, 0, 0)) if boundary
                 else pl.BlockSpec((tm, C_SHIFT), lambda i: (i, 0)))
    row = lambda a: a.reshape(1, -1).astype(F32)
    zpad = jnp.zeros((DECAY_LORA, D_RWKV), F32)
    w2p = jnp.concatenate([lw['w2'], zpad], axis=0).astype(BF16)
    a2p = jnp.concatenate([zpad, lw['a2']], axis=0).astype(BF16)
    full = lambda shape: pl.BlockSpec(shape, lambda i: (0,) * len(shape))
    out_sd = jax.ShapeDtypeStruct((N_PAIRS, n, LANES), F32)
    out_spec = pl.BlockSpec((N_PAIRS, tm, LANES), lambda i: (0, i, 0))
    return pl.pallas_call(
        functools.partial(_rwkv_prep_kernel, prev_is_boundary_row=boundary),
        grid=(n // tm,),
        in_specs=[pl.BlockSpec((tm, C_SHIFT), lambda i: (i, 0)), prev_spec,
                  full((1, C_SHIFT)), full((1, D_RWKV)), full((LANES, D_RWKV)), full((1, D_RWKV)),
                  full((LANES, D_RWKV)), full((GATE_LORA, D_RWKV)), full((1, D_RWKV)), full((1, D_RWKV)),
                  full((LANES, LANES))],
        out_specs=[out_spec] * 7,
        out_shape=[out_sd] * 7,
        compiler_params=_cparams("parallel"),
        name="rwkv_prep",
    )(zr, prev, row(lw['mu_shift']), row(lw['w0']), w2p, row(lw['a0']), a2p, lw['g2'].astype(BF16),
      row(lw['k_k']), row(lw['k_a']), _pair_ones())


def _rwkv_lockstep_kernel(r_ref, ld_ref, k_ref, v_ref, kap_ref, b_ref, g_ref, s0_ref, lg_ref, lb_ref, rk_ref,
                          y_ref, s_ref, *, chunk):
    c = pl.program_id(1)

    @pl.when(c == 0)
    def _():
        s_ref[...] = s0_ref[...]

    heads = range(H_RWKV)
    ti = lax.broadcasted_iota(jnp.int32, (chunk, chunk), 0)
    si = lax.broadcasted_iota(jnp.int32, (chunk, chunk), 1)
    tril_incl = (si <= ti)
    tril_strict = (si < ti)
    ltri = tril_incl.astype(F32)
    eye = (si == ti).astype(F32)
    hsl = [slice((hh % 2) * HEAD_DIM, (hh % 2 + 1) * HEAD_DIM) for hh in heads]

    ld = [ld_ref[p] for p in range(N_PAIRS)]
    cum = [_dot_hi(ltri, x) for x in ld]
    kt, rt, bi, ki, bh, kh, gall = [], [], [], [], [], [], []
    for p in range(N_PAIRS):
        last = cum[p][chunk - 1:chunk, :]
        gam_inv = jnp.exp(-cum[p])
        gam_tail = jnp.exp(last - cum[p])
        kap2, b2, k2 = kap_ref[p], b_ref[p], k_ref[p]
        kt2 = (kap2 * jnp.exp(cum[p] - ld[p])).astype(BF16)
        rt2 = r_ref[p] * jnp.exp(cum[p])
        bi2 = (b2 * gam_inv).astype(BF16)
        ki2 = (k2 * gam_inv).astype(BF16)
        bh2 = (b2 * gam_tail).astype(BF16)
        kh2 = (k2 * gam_tail).astype(BF16)
        g2 = jnp.exp(last)
        for h in range(2):
            sl = hsl[h]
            kt.append(kt2[:, sl]); rt.append(rt2[:, sl]); bi.append(bi2[:, sl]); ki.append(ki2[:, sl])
            bh.append(bh2[:, sl]); kh.append(kh2[:, sl]); gall.append(g2[:, sl])
    vv = [v_ref[hh // 2][:, hsl[hh]].astype(BF16) for hh in heads]
    kr = [jnp.concatenate([kt[hh], rt[hh].astype(BF16)], axis=0) for hh in heads]
    xb = [_dot_nt(kr[hh], bi[hh]) for hh in heads]
    xk = [_dot_nt(kr[hh], ki[hh]) for hh in heads]
    a_ab = [jnp.where(tril_strict, xb[hh][:chunk], 0.0) for hh in heads]
    a_rb = [jnp.where(tril_incl, xb[hh][chunk:], 0.0).astype(BF16) for hh in heads]
    a_ak = [jnp.where(tril_strict, xk[hh][:chunk], 0.0).astype(BF16) for hh in heads]
    a_rk = [jnp.where(tril_incl, xk[hh][chunk:], 0.0).astype(BF16) for hh in heads]
    akv = [_dot(a_ak[hh], vv[hh]) for hh in heads]
    ab = [x.astype(BF16) for x in a_ab]
    tm = [eye - a_ab[hh] for hh in heads]
    pw = [_dot(ab[hh], ab[hh]) for hh in heads]
    n_sq = int(math.log2(chunk)) - 1
    for it in range(n_sq):
        pwb = [x.astype(BF16) for x in pw]
        tm = [tm[hh] + _dot(tm[hh], pwb[hh]) for hh in heads]
        if it < n_sq - 1:
            pw = [_dot(pwb[hh], pwb[hh]) for hh in heads]
    tmb = [x.astype(BF16) for x in tm]
    w = [-_dot(tmb[hh], kt[hh]) for hh in heads]
    u0 = [-_dot(tmb[hh], akv[hh]) for hh in heads]
    wb = [x.astype(BF16) for x in w]
    u0b = [x.astype(BF16) for x in u0]
    pm = [rt[hh] + _dot(a_rb[hh], wb[hh]) for hh in heads]
    y0 = [_dot(a_rb[hh], u0b[hh]) + _dot(a_rk[hh], vv[hh]) for hh in heads]
    s_old = [s_ref[0, hh] for hh in heads]
    sb = [x.astype(BF16) for x in s_old]
    y = [_dot_nt(pm[hh], sb[hh]) + y0[hh] for hh in heads]
    mt = [_dot_tn(bh[hh], wb[hh]) for hh in heads]
    nt = [_dot_tn(jnp.concatenate([u0b[hh], vv[hh]], axis=0), jnp.concatenate([bh[hh], kh[hh]], axis=0))
          for hh in heads]
    for hh in heads:
        s_ref[0, hh] = s_old[hh] * gall[hh] + _dot_nt(sb[hh], mt[hh]) + nt[hh]
    for p in range(N_PAIRS):
        outs = []
        for h in range(2):
            hh = 2 * p + h
            sl = hsl[h]
            mu = jnp.mean(y[hh], axis=-1, keepdims=True)
            var = jnp.mean(jnp.square(y[hh] - mu), axis=-1, keepdims=True)
            yn = (y[hh] - mu) * lax.rsqrt(var + LNX_EPS)
            rk = r_ref[p][:, sl] * k_ref[p][:, sl] * rk_ref[p][:, sl]
            bonus = jnp.sum(rk, axis=-1, keepdims=True) * v_ref[p][:, sl]
            outs.append(yn * lg_ref[p][:, sl] + lb_ref[p][:, sl] + bonus)
        y_ref[p] = jnp.concatenate(outs, axis=-1) * g_ref[p]


def _rwkv_chunks(prep, s0, lw, n_batch, seq, chunk):
    n_chunks = seq // chunk
    pairs = lambda a: a.reshape(N_PAIRS, 1, LANES).astype(F32)
    tile = pl.BlockSpec((N_PAIRS, chunk, LANES), lambda b, c: (0, b * n_chunks + c, 0))
    state = pl.BlockSpec((1, H_RWKV, HEAD_DIM, HEAD_DIM), lambda b, c: (b, 0, 0, 0))
    par = pl.BlockSpec((N_PAIRS, 1, LANES), lambda b, c: (0, 0, 0))
    return pl.pallas_call(
        functools.partial(_rwkv_lockstep_kernel, chunk=chunk),
        grid=(n_batch, n_chunks),
        in_specs=[tile] * 7 + [state, par, par, par],
        out_specs=[tile, state],
        out_shape=[jax.ShapeDtypeStruct((N_PAIRS, n_batch * seq, LANES), F32),
                   jax.ShapeDtypeStruct((n_batch, H_RWKV, HEAD_DIM, HEAD_DIM), F32)],
        compiler_params=_cparams("parallel", "arbitrary"),
        name="rwkv_chunks",
    )(*prep, s0.astype(F32), pairs(lw['lnx_g']), pairs(lw['lnx_b']), pairs(lw['r_k']))


def _rwkv(zr, shift0, s0, lw, n_batch, seq):
    z3 = zr.reshape(n_batch, seq, C_SHIFT)
    tm = min(256, n_batch * seq)
    if seq % tm == 0:
        last = z3.reshape(n_batch, seq // tm, tm, C_SHIFT)[:, :, tm - 1]
        prev = jnp.concatenate([shift0[:, None].astype(F32), last[:, :-1]], axis=1).reshape(-1, 1, C_SHIFT)
    else:
        prev = jnp.concatenate([shift0[:, None].astype(F32), z3[:, :-1]], axis=1).reshape(n_batch * seq, C_SHIFT)
    prep = _rwkv_prep(zr, prev, lw, tm)
    chunk = CHUNK if seq >= CHUNK else SHORT_CHUNK
    pad = (-seq) % chunk
    if pad:
        prep = [jnp.pad(a.reshape(N_PAIRS, n_batch, seq, LANES), ((0, 0), (0, 0), (0, pad), (0, 0)))
                .reshape(N_PAIRS, n_batch * (seq + pad), LANES) for a in prep]
    y, s_fin = _rwkv_chunks(prep, s0, lw, n_batch, seq + pad, chunk)
    if pad:
        y = y.reshape(N_PAIRS, n_batch, seq + pad, LANES)[:, :, :seq].reshape(N_PAIRS, n_batch * seq, LANES)
    return y, s_fin, z3[:, -1]


def _layer_norm(h, g, b):
    mu = jnp.mean(h, axis=-1, keepdims=True)
    d = h - mu
    var = jnp.mean(d * d, axis=-1, keepdims=True)
    return d * lax.rsqrt(var + LN_EPS) * g + b


def _out_proj_kernel(oa_ref, or_ref, x_ref, wa_ref, wr_ref, g_ref, b_ref, rw_ref, x1_ref, x1b_ref, sc_ref):
    orw = jnp.concatenate([or_ref[p] for p in range(N_PAIRS)], axis=-1)
    mix = _dot(oa_ref[...], wa_ref[...]) + _dot(orw, wr_ref[...])
    x1 = _layer_norm(ALPHA * x_ref[...] + mix, g_ref[...], b_ref[...])
    x1_ref[...] = x1
    x1b_ref[...] = x1.astype(BF16)
    sc_ref[...] = _sigmoid(lax.dot_general(rw_ref[...], x1, (((1,), (1,)), ((), ())),
                                           precision=lax.Precision.HIGHEST, preferred_element_type=F32))


def _out_proj(o_attn, o_rwkv, x, lw, tm):
    n = x.shape[0]
    wo = lw['w_out'].astype(BF16)
    full = lambda shape: pl.BlockSpec(shape, lambda i: (0,) * len(shape))
    return pl.pallas_call(
        _out_proj_kernel,
        grid=(n // tm,),
        in_specs=[pl.BlockSpec((tm, D_ATTN), lambda i: (i, 0)),
                  pl.BlockSpec((N_PAIRS, tm, LANES), lambda i: (0, i, 0)),
                  pl.BlockSpec((tm, D_MODEL), lambda i: (i, 0)),
                  full((D_ATTN, D_MODEL)), full((D_RWKV, D_MODEL)), full((1, D_MODEL)), full((1, D_MODEL)),
                  full((N_EXPERTS, D_MODEL))],
        out_specs=[pl.BlockSpec((tm, D_MODEL), lambda i: (i, 0)), pl.BlockSpec((tm, D_MODEL), lambda i: (i, 0)),
                   pl.BlockSpec((N_EXPERTS, tm), lambda i: (0, i))],
        out_shape=[jax.ShapeDtypeStruct((n, D_MODEL), F32), jax.ShapeDtypeStruct((n, D_MODEL), BF16),
                   jax.ShapeDtypeStruct((N_EXPERTS, n), F32)],
        compiler_params=_cparams("parallel"),
        name="out_proj_ln_router",
    )(o_attn, o_rwkv, x, wo[:D_ATTN], wo[D_ATTN:], lw['ln1_g'].reshape(1, -1), lw['ln1_b'].reshape(1, -1),
      lw['router_w'].astype(F32).T)


def _expert_kernel(be_ref, na_ref, x_ref, wg_ref, wu_ref, wd_ref, *rest):
    y_ref, wg_s, wu_s, wd_s = rest[-4:]
    g = pl.program_id(0)
    active = g < na_ref[0]

    @pl.when(active & ((g == 0) | (be_ref[g] != be_ref[jnp.maximum(g - 1, 0)])))
    def _():
        wg_s[...] = wg_ref[0].astype(BF16)
        wu_s[...] = wu_ref[0].astype(BF16)
        wd_s[...] = wd_ref[0].astype(BF16)

    @pl.when(active)
    def _():
        x = x_ref[...]
        hg = jnp.dot(x, wg_s[...], preferred_element_type=F32)
        hu = jnp.dot(x, wu_s[...], preferred_element_type=F32)
        h = hg * _sigmoid(hg) * hu
        y_ref[...] = jnp.dot(h.astype(BF16), wd_s[...], preferred_element_type=F32).astype(y_ref.dtype)

    @pl.when(g >= na_ref[0])
    def _():
        y_ref[...] = jnp.zeros_like(y_ref)


def _expert_blocks(xs, blk_e, n_active, wg, wu, wd, tm, out_dtype, after):
    rows = xs.shape[0]
    n_blocks = rows // tm
    d_e = wg.shape[-1]
    return pl.pallas_call(
        _expert_kernel,
        grid_spec=pltpu.PrefetchScalarGridSpec(
            num_scalar_prefetch=2,
            grid=(n_blocks,),
            in_specs=[pl.BlockSpec((tm, D_MODEL), lambda g, be, na: (g, 0)),
                      pl.BlockSpec((1, D_MODEL, d_e), lambda g, be, na: (be[g], 0, 0)),
                      pl.BlockSpec((1, D_MODEL, d_e), lambda g, be, na: (be[g], 0, 0)),
                      pl.BlockSpec((1, d_e, D_MODEL), lambda g, be, na: (be[g], 0, 0))]
            + [pl.BlockSpec(memory_space=pl.ANY)] * len(after),
            out_specs=pl.BlockSpec((tm, D_MODEL), lambda g, be, na: (g, 0)),
            scratch_shapes=[pltpu.VMEM((D_MODEL, d_e), BF16), pltpu.VMEM((D_MODEL, d_e), BF16),
                            pltpu.VMEM((d_e, D_MODEL), BF16)],
        ),
        out_shape=jax.ShapeDtypeStruct((rows, D_MODEL), out_dtype),
        compiler_params=_cparams("arbitrary"),
        name="expert_blocks",
    )(blk_e, n_active, xs, wg, wu, wd, *after)


def _first_max(x, idx, n):
    m = jnp.max(x, axis=0, keepdims=True)
    return m, jnp.min(jnp.where(x == m, idx, n), axis=0, keepdims=True)


def _route_kernel(sc_ref, bias_ref, tri_ref, e_ref, gate_ref, rank_ref, cnt_ref, carry_s):
    i = pl.program_id(0)
    tr = sc_ref.shape[1]
    per_group = N_EXPERTS // N_GROUPS

    @pl.when(i == 0)
    def _():
        carry_s[...] = jnp.zeros_like(carry_s)

    sc = sc_ref[...]
    choice = sc + bias_ref[...]
    in_g = lax.broadcasted_iota(jnp.int32, (per_group, tr), 0)
    g_iota = lax.broadcasted_iota(jnp.int32, (N_GROUPS, tr), 0)
    grp = jnp.zeros((N_GROUPS, tr), F32)
    for gi in range(N_GROUPS):
        cg = choice[gi * per_group:(gi + 1) * per_group]
        m1, i1 = _first_max(cg, in_g, per_group)
        m2 = jnp.max(jnp.where(in_g == i1, NEG_INF, cg), axis=0, keepdims=True)
        grp = jnp.where(g_iota == gi, m1 + m2, grp)
    keep = jnp.zeros((N_GROUPS, tr), F32)
    for _ in range(TOPK_GROUPS):
        _, f = _first_max(grp, g_iota, N_GROUPS)
        hit = g_iota == f
        keep = jnp.where(hit, 1.0, keep)
        grp = jnp.where(hit, NEG_INF, grp)
    masked = jnp.concatenate(
        [jnp.where(keep[gi:gi + 1] > 0.0, choice[gi * per_group:(gi + 1) * per_group], NEG_INF)
         for gi in range(N_GROUPS)], axis=0)
    e_iota = lax.broadcasted_iota(jnp.int32, (N_EXPERTS, tr), 0)
    hits, es, ss = [], [], []
    onehot = jnp.zeros((N_EXPERTS, tr), F32)
    for _ in range(TOP_K):
        _, f = _first_max(masked, e_iota, N_EXPERTS)
        hit = e_iota == f
        hits.append(hit)
        es.append(f)
        ss.append(jnp.sum(jnp.where(hit, sc, 0.0), axis=0, keepdims=True))
        onehot = jnp.where(hit, 1.0, onehot)
        masked = jnp.where(hit, NEG_INF, masked)
    total = ss[0]
    for s in ss[1:]:
        total = total + s
    e_ref[...] = jnp.concatenate(es, axis=0)
    gate_ref[...] = jnp.concatenate([s / total * ROUTED_SCALE for s in ss], axis=0)
    before = carry_s[...] + jnp.dot(onehot.astype(BF16), tri_ref[...], preferred_element_type=F32)
    rank_ref[...] = jnp.concatenate(
        [jnp.sum(jnp.where(h, before, 0.0), axis=0, keepdims=True) for h in hits], axis=0).astype(jnp.int32)
    carry = carry_s[...] + jnp.sum(onehot, axis=1, keepdims=True)
    carry_s[...] = carry
    cnt_ref[...] = jnp.broadcast_to(carry, cnt_ref.shape).astype(jnp.int32)


def _route(scores_t, router_bias, tok0, n):
    tr = min(256, n)
    assert tok0 % tr == 0 and n % tr == 0
    tri = jnp.asarray(np.arange(tr)[:, None] < np.arange(tr)[None, :], BF16)
    col = pl.BlockSpec((TOP_K, tr), lambda i: (0, i))
    top_e, gate, rank, counts = pl.pallas_call(
        _route_kernel,
        grid=(n // tr,),
        in_specs=[pl.BlockSpec((N_EXPERTS, tr), lambda i: (0, i + tok0 // tr)),
                  pl.BlockSpec((N_EXPERTS, 1), lambda i: (0, 0)),
                  pl.BlockSpec((tr, tr), lambda i: (0, 0))],
        out_specs=[col, col, col, pl.BlockSpec((N_EXPERTS, LANES), lambda i: (0, 0))],
        out_shape=[jax.ShapeDtypeStruct((TOP_K, n), jnp.int32), jax.ShapeDtypeStruct((TOP_K, n), F32),
                   jax.ShapeDtypeStruct((TOP_K, n), jnp.int32), jax.ShapeDtypeStruct((N_EXPERTS, LANES), jnp.int32)],
        scratch_shapes=[pltpu.VMEM((N_EXPERTS, 1), F32)],
        compiler_params=_cparams("arbitrary"),
        name="route",
    )(scores_t, router_bias.astype(F32).reshape(N_EXPERTS, 1), tri)
    counts = counts[:, 0]
    pcounts = (counts + MOE_TM - 1) // MOE_TM * MOE_TM
    pends = jnp.cumsum(pcounts)
    starts = pends - pcounts
    dest = rank
    for e in range(N_EXPERTS):
        dest = dest + jnp.where(top_e == e, starts[e], 0)
    n_blocks = -(-(n * TOP_K) // MOE_TM) + N_EXPERTS
    slot_tok = jnp.zeros((n_blocks * MOE_TM,), jnp.int32).at[dest.reshape(-1)].set(
        jnp.tile(jnp.arange(tok0, tok0 + n, dtype=jnp.int32), TOP_K))
    blk_start = jnp.arange(n_blocks, dtype=jnp.int32) * MOE_TM
    blk_e = jnp.minimum(jnp.sum((pends[None, :] <= blk_start[:, None]).astype(jnp.int32), axis=1), N_EXPERTS - 1)
    n_active = (pends[-1] // MOE_TM).astype(jnp.int32).reshape(1)
    return gate.T, dest.astype(jnp.int32), slot_tok, blk_e, n_active


def _combine_kernel(x1_ref, sh_ref, gate_ref, g_ref, b_ref, *rest, tiles_per_part):
    ys_refs, o_ref = rest[:-1], rest[-1]
    i = pl.program_id(0)
    gate = gate_ref[...]
    for part, ys_ref in enumerate(ys_refs):
        @pl.when(i // tiles_per_part == part)
        def _():
            f = sh_ref[...]
            for j in range(TOP_K):
                f = f + ys_ref[j].astype(F32) * gate[:, j:j + 1]
            o_ref[...] = _layer_norm(ALPHA * x1_ref[...] + f, g_ref[...], b_ref[...])


def _combine(x1, shared, ysgs, gate, lw, tm):
    n = x1.shape[0]
    tiles_per_part = n // len(ysgs) // tm
    row = pl.BlockSpec((tm, D_MODEL), lambda i: (i, 0))
    par = pl.BlockSpec((1, D_MODEL), lambda i: (0, 0))

    def part_spec(part):
        return pl.BlockSpec((TOP_K, tm, D_MODEL),
                            lambda i: (0, jnp.clip(i - part * tiles_per_part, 0, tiles_per_part - 1), 0))

    return pl.pallas_call(
        functools.partial(_combine_kernel, tiles_per_part=tiles_per_part),
        grid=(n // tm,),
        in_specs=[row, row, pl.BlockSpec((tm, TOP_K), lambda i: (i, 0)), par, par]
        + [part_spec(part) for part in range(len(ysgs))],
        out_specs=row,
        out_shape=jax.ShapeDtypeStruct((n, D_MODEL), F32),
        compiler_params=_cparams("arbitrary"),
        name="moe_combine_ln",
    )(x1, shared, gate, lw['ln2_g'].reshape(1, -1), lw['ln2_b'].reshape(1, -1), *ysgs)


def _moe(x1, x1b, scores_t, lw, ew, n_parts, while_gathering):
    n = x1.shape[0]
    n_part = n // n_parts
    routes = [_route(scores_t, lw['router_bias'], part * n_part, n_part) for part in range(n_parts)]
    wait_for = while_gathering(routes[0][2])
    ysgs, ys = [], None
    for part, (_, dest, slot_tok, blk_e, n_active) in enumerate(routes):
        xs = x1b[slot_tok]
        after = [routes[part + 1][2]] if part + 1 < n_parts else []
        after = after + ([ys] if ys is not None else wait_for)
        ys = _expert_blocks(xs, blk_e, n_active, ew['e_gate'], ew['e_up'], ew['e_down'], MOE_TM, BF16, after)
        ysgs.append(ys[dest.reshape(-1)].reshape(TOP_K, n_part, D_MODEL))
    tm_s = min(MOE_TM, n)
    shared = _expert_blocks(x1b, jnp.zeros((n // tm_s,), jnp.int32), jnp.full((1,), n // tm_s, jnp.int32),
                            ew['s_gate'], ew['s_up'], ew['s_down'], tm_s, F32, [ys])
    gate = jnp.concatenate([r[0] for r in routes], axis=0)
    return _combine(x1, shared, ysgs, gate, lw, min(128, n_part))


def _copy_kernel(x_ref, after_ref, o_ref):
    del after_ref
    o_ref[...] = x_ref[...]


def _order_after(x, after):
    return pl.pallas_call(
        _copy_kernel,
        in_specs=[pl.BlockSpec(memory_space=pltpu.VMEM), pl.BlockSpec(memory_space=pl.ANY)],
        out_specs=pl.BlockSpec(memory_space=pltpu.VMEM),
        out_shape=jax.ShapeDtypeStruct(x.shape, x.dtype),
        name="order_after",
    )(x, after)


def _project(x, w_in_b):
    n_batch, seq, _ = x.shape
    n = n_batch * seq
    xb = x.reshape(n, D_MODEL).astype(BF16)
    tm = min(512, n)
    zr = _matmul(xb, w_in_b[:, 3 * D_ATTN:], tm, C_SHIFT // 2)
    if seq % tm == 0:
        zqkv, kt, vt = _matmul_qkv(xb, w_in_b[:, :3 * D_ATTN], tm, n_batch, seq)
        k, v = (t.reshape(n_batch, H_ATTN, HEAD_DIM, seq).transpose(0, 3, 1, 2) for t in (kt, vt))
    else:
        zqkv = _matmul(xb, w_in_b[:, :3 * D_ATTN], tm, 768)
        k = zqkv[:, D_ATTN:2 * D_ATTN].reshape(n_batch, seq, H_ATTN, HEAD_DIM)
        v = zqkv[:, 2 * D_ATTN:].reshape(n_batch, seq, H_ATTN, HEAD_DIM)
    return zqkv, zr, k, v


def _layer(x, zr, o_attn, shift0, s0, lw, ew, while_gathering=lambda slots: []):
    n_batch, seq, _ = x.shape
    n = n_batch * seq
    o_rwkv, s_new, shift_new = _rwkv(zr, shift0, s0, lw, n_batch, seq)
    x1, x1b, scores = _out_proj(o_attn, o_rwkv, x.reshape(n, D_MODEL), lw, min(256, n))
    y = _moe(x1, x1b, scores, lw, ew, MOE_TOKEN_PARTS, while_gathering)
    return y.reshape(n_batch, seq, D_MODEL), s_new, shift_new


def kernel(x_prompt, x_sample, cache_k, cache_v, state_wkv, state_shift, page_table, rel_bias, w_in, mu_shift, w0, w2,
           a0, a2, g2, k_k, k_a, r_k, lnx_g, lnx_b, w_out, ln1_g, ln1_b, router_w, router_bias, e_gate, e_up, e_down,
           s_gate, s_up, s_down, ln2_g, ln2_b):
    assert w_in.shape[0] == DEPTH == 1
    n_b, seq, _ = x_prompt.shape
    db, dec_seq, _ = x_sample.shape
    past_len = page_table.shape[1] * PAGE_SIZE
    lw = {'mu_shift': mu_shift[0], 'w0': w0[0], 'w2': w2[0], 'a0': a0[0], 'a2': a2[0], 'g2': g2[0], 'k_k': k_k[0],
          'k_a': k_a[0], 'r_k': r_k[0], 'lnx_g': lnx_g[0], 'lnx_b': lnx_b[0], 'w_out': w_out[0], 'ln1_g': ln1_g[0],
          'ln1_b': ln1_b[0], 'router_w': router_w[0], 'router_bias': router_bias[0], 'ln2_g': ln2_g[0],
          'ln2_b': ln2_b[0]}
    ew = {'e_gate': e_gate[0], 'e_up': e_up[0], 'e_down': e_down[0],
          's_gate': s_gate, 's_up': s_up, 's_down': s_down}
    w_in_b = w_in[0].astype(BF16)

    zqkv_s, zr_s, k_s, v_s = _project(x_sample, w_in_b)
    idx = _sample_select(zqkv_s, cache_k, page_table, db, dec_seq, past_len)
    zqkv_p, zr_p, k_p, v_p = _project(x_prompt, w_in_b)
    o_attn_p = _moba_prompt(zqkv_p, _order_after(rel_bias, idx), n_b, seq)
    sample_attn = []

    def attend_sample(slots):
        sample_attn.append(
            _sample_attend(zqkv_s, idx, cache_k, cache_v, page_table, rel_bias, db, dec_seq, past_len, slots))
        return sample_attn

    y_p, w_p, s_p = _layer(
        x_prompt, zr_p, o_attn_p, jnp.zeros((n_b, C_SHIFT), F32),
        jnp.zeros((n_b, H_RWKV, HEAD_DIM, HEAD_DIM), F32), lw, ew, attend_sample)
    y_s, w_s, s_s = _layer(x_sample, zr_s, sample_attn[0], state_shift[0], state_wkv[0], lw, ew)
    return (y_p, y_s, k_p[None], v_p[None], w_p[None], s_p[None], k_s[None], v_s[None], w_s[None], s_s[None])
```

```python
import functools
import math

import numpy as np
import jax
import jax.numpy as jnp
from jax import lax
from jax.experimental import pallas as pl
from jax.experimental.pallas import tpu as pltpu

F32 = jnp.float32
BF16 = jnp.bfloat16

D_MODEL = 2048
HEAD_DIM = 64
D_ATTN = 1024
D_RWKV = 1024
H_ATTN = 16
H_RWKV = 16
N_PAIRS = 8
LANES = 128
BLOCK = 256
TOP_BLOCKS = 3
PAGE_SIZE = 128
N_BUCKETS = 32
MAX_DISTANCE = 128
DECAY_LORA = 64
AAA_LORA = 64
GATE_LORA = 128
C_SHIFT = 3 * D_RWKV + DECAY_LORA + AAA_LORA + GATE_LORA
LNX_EPS = 64e-5
LN_EPS = 1e-5
N_EXPERTS = 64
TOP_K = 8
N_GROUPS = 8
TOPK_GROUPS = 4
D_EXPERT = 512
ROUTED_SCALE = 2.5
DEPTH = 1
ALPHA = (2.0 * DEPTH) ** 0.25
CHUNK = 64
MOE_TM = 256
MOE_TOKEN_PARTS = 1
SHORT_CHUNK = 16
FAR_BLOCKS_PER_STEP = 2
VMEM_LIMIT = 56 * 1024 * 1024
NEG_INF = float("-inf")
LOG2E = math.log2(math.e)


def _cparams(*sem):
    return pltpu.CompilerParams(dimension_semantics=sem, vmem_limit_bytes=VMEM_LIMIT)


def _dot(a, b):
    return jnp.dot(a.astype(BF16), b.astype(BF16), preferred_element_type=F32)


def _dot_nt(a, b):
    return lax.dot_general(a.astype(BF16), b.astype(BF16), (((1,), (1,)), ((), ())), preferred_element_type=F32)


def _dot_tn(a, b):
    return lax.dot_general(a.astype(BF16), b.astype(BF16), (((0,), (0,)), ((), ())), preferred_element_type=F32)


def _dot_hi(a, b):
    return jnp.dot(a, b, precision=lax.Precision.HIGHEST, preferred_element_type=F32)


def _sigmoid(x):
    return 1.0 / (1.0 + jnp.exp(-x))


def _mm_kernel(x_ref, w_ref, o_ref):
    o_ref[...] = jnp.dot(x_ref[...], w_ref[...], preferred_element_type=F32)


def _matmul(x, w, tm, tn):
    m, k = x.shape
    n = w.shape[1]
    return pl.pallas_call(
        _mm_kernel,
        grid=(m // tm, n // tn),
        in_specs=[pl.BlockSpec((tm, k), lambda i, j: (i, 0)),
                  pl.BlockSpec((k, tn), lambda i, j: (0, j))],
        out_specs=pl.BlockSpec((tm, tn), lambda i, j: (i, j)),
        out_shape=jax.ShapeDtypeStruct((m, n), F32),
        compiler_params=_cparams("parallel", "parallel"),
        name="in_proj",
    )(x, w)


def _qkv_kernel(x_ref, w_ref, o_ref, kt_ref, vt_ref):
    j = pl.program_id(1)
    z = jnp.dot(x_ref[...], w_ref[...], preferred_element_type=F32)
    o_ref[...] = z

    @pl.when(j == 1)
    def _():
        kt_ref[0] = z.T

    @pl.when(j == 2)
    def _():
        vt_ref[0] = z.T


def _matmul_qkv(x, w, tm, n_batch, seq):
    m, k = x.shape
    tiles_per_seq = seq // tm
    assert seq % tm == 0 and w.shape[1] == 3 * D_ATTN
    t_spec = pl.BlockSpec((1, D_ATTN, tm), lambda i, j: (i // tiles_per_seq, 0, i % tiles_per_seq))
    t_shape = jax.ShapeDtypeStruct((n_batch, D_ATTN, seq), F32)
    return pl.pallas_call(
        _qkv_kernel,
        grid=(m // tm, 3),
        in_specs=[pl.BlockSpec((tm, k), lambda i, j: (i, 0)),
                  pl.BlockSpec((k, D_ATTN), lambda i, j: (0, j))],
        out_specs=[pl.BlockSpec((tm, D_ATTN), lambda i, j: (i, j)), t_spec, t_spec],
        out_shape=[jax.ShapeDtypeStruct((m, 3 * D_ATTN), F32), t_shape, t_shape],
        compiler_params=_cparams("parallel", "arbitrary"),
        name="in_proj_qkv",
    )(x, w)


def _bucket_table(max_dist):
    n = np.arange(max_dist + 1)
    max_exact = N_BUCKETS // 2
    nf = np.maximum(n, 1).astype(np.float64)
    large = max_exact + (np.log(nf / max_exact) / math.log(MAX_DISTANCE / max_exact)
                         * (N_BUCKETS - max_exact)).astype(np.int64)
    large = np.minimum(large, N_BUCKETS - 1)
    return np.where(n < max_exact, n, large).astype(np.int32)


def _relative_bias_log2(rel_bias):
    rb = rel_bias.astype(F32)
    return (rb - rb[N_BUCKETS - 1:]) * LOG2E


def _prompt_bucket_tiles():
    kpos = np.arange(BLOCK)[:, None]
    qpos = np.arange(BLOCK)[None, :]
    bt = _bucket_table(2 * BLOCK)
    d_own = qpos - kpos
    own = np.where(d_own >= 0, bt[np.maximum(d_own, 0)], -1)
    return jnp.asarray(np.stack([own, bt[BLOCK + qpos - kpos]]).astype(np.int32))


def _moba_prompt_kernel(rb_ref, q_ref, k_ref, v_ref, bkt_ref, o_ref, kb_s, vt_s, km_s, sel_s, bias_s, sa_s, sb_s,
                        *, n_blk):
    p = pl.program_id(1)
    i = pl.program_id(2)

    @pl.when(i == 0)
    def _():
        for kind in range(2):
            bk = bkt_ref[kind]
            for h in range(2):
                t = jnp.zeros((BLOCK, BLOCK), F32)
                for b in range(N_BUCKETS):
                    t = jnp.where(bk == b, rb_ref[b * H_ATTN + 2 * p + h], t)
                bias_s[h, kind] = jnp.where(bk < 0, NEG_INF, t)

        def fill(j, c):
            r0 = pl.multiple_of(j * BLOCK, BLOCK)
            kblk = k_ref[pl.ds(r0, BLOCK), :]
            vblk = v_ref[pl.ds(r0, BLOCK), :]
            kb_s[pl.ds(r0, BLOCK), :] = kblk.astype(BF16)
            vt_s[j] = vblk.T.astype(BF16)
            km_s[pl.ds(j, 1), :] = jnp.sum(kblk, axis=0, keepdims=True) * (1.0 / BLOCK)
            return c
        lax.fori_loop(0, n_blk, fill, 0)

    qt = q_ref[...].T
    row = lax.broadcasted_iota(jnp.int32, (LANES, BLOCK), 0)
    head0 = row < HEAD_DIM
    qts = [jnp.where(head0, qt, 0.0), jnp.where(head0, 0.0, qt)]
    qtb = [(x * (HEAD_DIM ** -0.5 * LOG2E)).astype(BF16) for x in qts]

    blk = lax.broadcasted_iota(jnp.int32, (n_blk, BLOCK), 0)
    valid = blk < i
    km = km_s[...]
    for h in range(2):
        g = jnp.where(valid, _dot_hi(km, qts[h]), NEG_INF)
        sel = jnp.zeros((n_blk, BLOCK), F32)
        for _ in range(TOP_BLOCKS):
            m = jnp.max(g, axis=0, keepdims=True)
            first = jnp.min(jnp.where(g == m, blk, n_blk), axis=0, keepdims=True)
            hit = blk == first
            sel = jnp.where(hit & valid, 1.0, sel)
            g = jnp.where(hit, NEG_INF, g)
        sel_s[h] = sel

    def scores(j, h):
        r0 = pl.multiple_of(j * BLOCK, BLOCK)
        return jnp.dot(kb_s[pl.ds(r0, BLOCK), :], qtb[h], preferred_element_type=F32)

    def pv(j, pb0, pb1):
        vt = vt_s[j]
        return jnp.concatenate(
            [jnp.dot(vt[:HEAD_DIM], pb0, preferred_element_type=F32),
             jnp.dot(vt[HEAD_DIM:], pb1, preferred_element_type=F32)], axis=0)

    def update(blocks, carry):
        m0, m1, l0, l1, acc = carry
        new = []
        for h, (m, l) in enumerate(((m0, l0), (m1, l1))):
            mn = m
            for blk_ in blocks:
                mn = jnp.where(blk_[3 + h], jnp.maximum(mn, jnp.max(blk_[1 + h], axis=0, keepdims=True)), mn)
            a = jnp.exp2(m - mn)
            ln = a * l
            pbs = []
            for blk_ in blocks:
                pr = jnp.exp2(blk_[1 + h] - jnp.where(blk_[3 + h], mn, jnp.inf))
                ln = ln + jnp.sum(pr, axis=0, keepdims=True)
                pbs.append(pr.astype(BF16))
            new.append((mn, a, ln, pbs))
        acc = jnp.where(head0, new[0][1], new[1][1]) * acc
        for n, blk_ in enumerate(blocks):
            acc = acc + pv(blk_[0], new[0][3][n], new[1][3][n])
        return new[0][0], new[1][0], new[0][2], new[1][2], acc

    jp = jnp.maximum(i - 1, 0)
    has_prev = i > 0
    keep0 = (sel_s[0, pl.ds(jp, 1), :] > 0.0) & has_prev
    keep1 = (sel_s[1, pl.ds(jp, 1), :] > 0.0) & has_prev
    always = jnp.full((1, BLOCK), True)
    row_init = jnp.full((1, BLOCK), NEG_INF, F32)
    carry = (row_init, row_init, jnp.zeros((1, BLOCK), F32), jnp.zeros((1, BLOCK), F32),
             jnp.zeros((LANES, BLOCK), F32))
    n_far = jnp.maximum(i - 1, 0)

    def issue_scores(t, dst):
        for u in range(FAR_BLOCKS_PER_STEP):
            jc = jnp.minimum(t * FAR_BLOCKS_PER_STEP + u, n_blk - 1)
            for h in range(2):
                dst[u, h] = scores(jc, h)

    def consume(t, src, carry):
        blocks = []
        for u in range(FAR_BLOCKS_PER_STEP):
            j = t * FAR_BLOCKS_PER_STEP + u
            in_range = j < n_far
            jc = jnp.minimum(j, n_blk - 1)
            blocks.append((jc, src[u, 0], src[u, 1],
                           (sel_s[0, pl.ds(jc, 1), :] > 0.0) & in_range,
                           (sel_s[1, pl.ds(jc, 1), :] > 0.0) & in_range))
        return update(blocks, carry)

    issue_scores(0, sa_s)
    carry = update([(i, scores(i, 0) + bias_s[0, 0], scores(i, 1) + bias_s[1, 0], always, always),
                    (jp, scores(jp, 0) + bias_s[0, 1], scores(jp, 1) + bias_s[1, 1], keep0, keep1)], carry)

    def far_body(it, carry):
        issue_scores(2 * it + 1, sb_s)
        carry = consume(2 * it, sa_s, carry)
        issue_scores(2 * it + 2, sa_s)
        return consume(2 * it + 1, sb_s, carry)

    n_iter = (n_far + 2 * FAR_BLOCKS_PER_STEP - 1) // (2 * FAR_BLOCKS_PER_STEP)
    m0, m1, l0, l1, acc = lax.fori_loop(0, n_iter, far_body, carry)
    ot = acc / jnp.where(head0, l0, l1)
    o_ref[...] = ot.T


def _moba_prompt(zqkv, rel_bias, n_batch, seq):
    n_blk = seq // BLOCK
    kern = functools.partial(_moba_prompt_kernel, n_blk=n_blk)
    stage = pltpu.VMEM((FAR_BLOCKS_PER_STEP, 2, BLOCK, BLOCK), F32)
    return pl.pallas_call(
        kern,
        grid=(n_batch, N_PAIRS, n_blk),
        in_specs=[
            pl.BlockSpec(memory_space=pltpu.SMEM),
            pl.BlockSpec((BLOCK, LANES), lambda b, p, i: (b * n_blk + i, p)),
            pl.BlockSpec((seq, LANES), lambda b, p, i: (b, N_PAIRS + p)),
            pl.BlockSpec((seq, LANES), lambda b, p, i: (b, 2 * N_PAIRS + p)),
            pl.BlockSpec((2, BLOCK, BLOCK), lambda b, p, i: (0, 0, 0)),
        ],
        out_specs=pl.BlockSpec((BLOCK, LANES), lambda b, p, i: (b * n_blk + i, p)),
        out_shape=jax.ShapeDtypeStruct((n_batch * seq, D_ATTN), F32),
        scratch_shapes=[pltpu.VMEM((seq, LANES), BF16),
                        pltpu.VMEM((n_blk, LANES, BLOCK), BF16),
                        pltpu.VMEM((n_blk, LANES), F32),
                        pltpu.VMEM((2, n_blk, BLOCK), F32),
                        pltpu.VMEM((2, 2, BLOCK, BLOCK), F32),
                        stage, stage],
        compiler_params=_cparams("parallel", "parallel", "arbitrary"),
        name="moba_prompt",
    )(_relative_bias_log2(rel_bias).reshape(-1), zqkv, zqkv, zqkv, _prompt_bucket_tiles())


PAGES_PER_STEP = 8


def _split_bf16(x):
    hi = x.astype(BF16)
    return hi, (x - hi.astype(F32)).astype(BF16)


def _sample_gate_kernel(pt_ref, q_ref, seg_ref, *refs, n_past_blk, n_steps, dec_seq):
    page_refs = refs[:PAGES_PER_STEP]
    idx_ref = refs[PAGES_PER_STEP]
    sum_s = refs[PAGES_PER_STEP + 1]
    s = pl.program_id(1)
    ppb = BLOCK // PAGE_SIZE
    ones = jnp.ones((16, PAGE_SIZE), BF16)
    for bl in range(PAGES_PER_STEP // ppb):
        x = page_refs[ppb * bl][0]
        for e in range(1, ppb):
            x = x + page_refs[ppb * bl + e][0]
        hi, lo = _split_bf16(x)
        tot = _dot_nt(ones, hi) + _dot_nt(ones, lo)
        sum_s[pl.ds(s * (PAGES_PER_STEP // ppb) + bl, 1), :] = tot[:1]

    @pl.when(s == n_steps - 1)
    def _():
        blk = lax.broadcasted_iota(jnp.int32, (n_past_blk, LANES), 0)
        ksum = sum_s[...]
        for qi in range(dec_seq):
            hi, lo = _split_bf16(ksum * q_ref[0, qi:qi + 1, :])
            g = _dot(hi, seg_ref[...]) + _dot(lo, seg_ref[...])
            rows = []
            for _ in range(TOP_BLOCKS):
                m = jnp.max(g, axis=0, keepdims=True)
                first = jnp.min(jnp.where(g == m, blk, n_past_blk), axis=0, keepdims=True)
                rows.append(first)
                g = jnp.where(blk == first, NEG_INF, g)
            rows.append(jnp.zeros((8 - TOP_BLOCKS, LANES), jnp.int32))
            idx_ref[0, qi] = jnp.concatenate(rows, axis=0)


def _sample_gate(page_table, q8, cache_k_pages, n_past_blk, dec_seq):
    db = q8.shape[0]
    n_pages = page_table.shape[1]
    n_steps = n_pages // PAGES_PER_STEP
    kern = functools.partial(_sample_gate_kernel, n_past_blk=n_past_blk, n_steps=n_steps, dec_seq=dec_seq)
    col = np.arange(H_ATTN * HEAD_DIM)[:, None] // HEAD_DIM
    seg = jnp.asarray(col == np.arange(LANES)[None, :], BF16)

    def page_spec(r):
        return pl.BlockSpec((1, H_ATTN * HEAD_DIM, PAGE_SIZE),
                            lambda b, s, pt: (pt[b, s * PAGES_PER_STEP + r], 0, 0))

    return pl.pallas_call(
        kern,
        grid_spec=pltpu.PrefetchScalarGridSpec(
            num_scalar_prefetch=1,
            grid=(db, n_steps),
            in_specs=[pl.BlockSpec((1, 8, H_ATTN * HEAD_DIM), lambda b, s, pt: (b, 0, 0)),
                      pl.BlockSpec((H_ATTN * HEAD_DIM, LANES), lambda b, s, pt: (0, 0))]
            + [page_spec(r) for r in range(PAGES_PER_STEP)],
            out_specs=pl.BlockSpec((1, dec_seq, 8, LANES), lambda b, s, pt: (b, 0, 0, 0)),
            scratch_shapes=[pltpu.VMEM((n_past_blk, H_ATTN * HEAD_DIM), F32)],
        ),
        out_shape=jax.ShapeDtypeStruct((db, dec_seq, 8, LANES), jnp.int32),
        compiler_params=_cparams("parallel", "arbitrary"),
        name="sample_gate",
    )(page_table, q8, seg, *([cache_k_pages] * PAGES_PER_STEP))


def _sample_attn_kernel(slab_ref, idx_ref, qt_ref, kn_ref, vn_ref, bown_ref, blast_ref, far_ref, *refs,
                        dec_seq, n_past_blk):
    n_slab = dec_seq * TOP_BLOCKS * (BLOCK // PAGE_SIZE)
    k_refs = refs[:n_slab]
    v_refs = refs[n_slab:2 * n_slab]
    o_ref = refs[2 * n_slab]
    b = pl.program_id(0)
    h = pl.program_id(1)
    ppb = BLOCK // PAGE_SIZE
    lane8 = lax.broadcasted_iota(jnp.int32, (HEAD_DIM, 8), 1)
    out = jnp.zeros((HEAD_DIM, 8), F32)
    kn = kn_ref[0, 0]
    vn = vn_ref[0, 0]
    far = far_ref[h]
    per_q = TOP_BLOCKS * ppb
    queries = range(dec_seq)
    qcs = [qt_ref[0, 0][:, qi:qi + 1] * (HEAD_DIM ** -0.5) for qi in queries]
    rows = []
    for qi in queries:
        for t in range(TOP_BLOCKS):
            sel_blk = idx_ref[((b * H_ATTN + h) * dec_seq + qi) * TOP_BLOCKS + t]
            is_last = sel_blk == n_past_blk - 1
            for e in range(ppb):
                slab = k_refs[qi * per_q + t * ppb + e][0]
                srow = jnp.sum(slab * qcs[qi], axis=0, keepdims=True)
                rows.append(srow + jnp.where(is_last, blast_ref[0, qi, e:e + 1, :], far))
    s_own = [jnp.sum(kn * qcs[qi], axis=0, keepdims=True) + bown_ref[0, qi:qi + 1, :] for qi in queries]
    row_max = [functools.reduce(jnp.maximum, rows[qi * per_q:(qi + 1) * per_q]) for qi in queries]
    m = [jnp.maximum(jnp.max(row_max[qi], axis=-1, keepdims=True), jnp.max(s_own[qi], axis=-1, keepdims=True))
         for qi in queries]
    p_own = [jnp.exp(s_own[qi] - m[qi]) for qi in queries]
    prs = [jnp.exp(rows[n] - m[n // per_q]) for n in range(len(rows))]
    l = [jnp.sum(functools.reduce(jnp.add, prs[qi * per_q:(qi + 1) * per_q]), axis=-1, keepdims=True)
         + jnp.sum(p_own[qi], axis=-1, keepdims=True) for qi in queries]
    acc = [functools.reduce(jnp.add, [v_refs[n][0] * prs[n] for n in range(qi * per_q, (qi + 1) * per_q)])
           for qi in queries]
    o = [jnp.sum(acc[qi], axis=-1, keepdims=True) + jnp.sum(vn * p_own[qi], axis=-1, keepdims=True)
         for qi in queries]
    for qi in queries:
        out = jnp.where(lane8 == qi, o[qi] / l[qi], out)
    o_ref[0, 0] = out


def _sample_attn(slabs, idx_flat, qt, knt, vnt, bias_own, bias_last, far, ck_slabs, cv_slabs, dec_seq, n_past_blk):
    db = qt.shape[0]
    ppb = BLOCK // PAGE_SIZE
    n_slab = dec_seq * TOP_BLOCKS * ppb
    kern = functools.partial(_sample_attn_kernel, dec_seq=dec_seq, n_past_blk=n_past_blk)

    def slab_spec(n):
        return pl.BlockSpec((1, HEAD_DIM, PAGE_SIZE),
                            lambda b, h, sl, ix: (sl[(b * H_ATTN + h) * n_slab + n], 0, 0))

    small = lambda shape: pl.BlockSpec(shape, lambda b, h, sl, ix: (b, h, 0, 0))
    return pl.pallas_call(
        kern,
        grid_spec=pltpu.PrefetchScalarGridSpec(
            num_scalar_prefetch=2,
            grid=(db, H_ATTN),
            in_specs=[small((1, 1, HEAD_DIM, 8)), small((1, 1, HEAD_DIM, 8)), small((1, 1, HEAD_DIM, 8)),
                      pl.BlockSpec((1, dec_seq, 8), lambda b, h, sl, ix: (h, 0, 0)),
                      pl.BlockSpec((1, dec_seq, ppb, PAGE_SIZE), lambda b, h, sl, ix: (h, 0, 0, 0)),
                      pl.BlockSpec(memory_space=pltpu.SMEM)]
            + [slab_spec(n) for n in range(n_slab)] * 2,
            out_specs=small((1, 1, HEAD_DIM, 8)),
        ),
        out_shape=jax.ShapeDtypeStruct((db, H_ATTN, HEAD_DIM, 8), F32),
        compiler_params=_cparams("parallel", "parallel"),
        name="sample_attn",
    )(slabs, idx_flat, qt, knt, vnt, bias_own, bias_last, far, *([ck_slabs] * n_slab), *([cv_slabs] * n_slab))


def _sample_select(zqkv_s, cache_k, page_table, db, dec_seq, past_len):
    n_past_blk = past_len // BLOCK
    assert past_len % BLOCK == 0 and n_past_blk >= TOP_BLOCKS
    pool = cache_k.shape[1]
    ckt = cache_k[0].transpose(0, 2, 3, 1)
    q = zqkv_s.reshape(db, dec_seq, 3 * D_ATTN)[..., :D_ATTN]
    q8 = jnp.pad(q, ((0, 0), (0, 8 - dec_seq), (0, 0)))
    idx = _sample_gate(page_table, q8, ckt.reshape(pool, H_ATTN * HEAD_DIM, PAGE_SIZE), n_past_blk, dec_seq)
    return idx[:, :, :TOP_BLOCKS, :H_ATTN].transpose(0, 3, 1, 2)


def _sample_attend(zqkv_s, idx, cache_k, cache_v, page_table, rel_bias, db, dec_seq, past_len, after):
    n_past_blk = past_len // BLOCK
    ppb = BLOCK // PAGE_SIZE
    pool = cache_k.shape[1]
    ckt = cache_k[0].transpose(0, 2, 3, 1)
    cvt = cache_v[0].transpose(0, 2, 3, 1)
    z3 = zqkv_s.reshape(db, dec_seq, 3 * D_ATTN)
    q = z3[..., :D_ATTN]
    pages = jnp.take_along_axis(page_table[:, None, None, None, :],
                                (idx[..., None] * ppb + jnp.arange(ppb)).reshape(db, H_ATTN, dec_seq, 1, -1),
                                axis=-1)
    slabs = (pages.reshape(db, H_ATTN, -1) * H_ATTN + jnp.arange(H_ATTN)[None, :, None]).reshape(-1).astype(jnp.int32)
    slabs = _order_after(slabs.reshape(-1, LANES), after).reshape(-1)

    def heads_t(t):
        t = t.reshape(db, dec_seq, H_ATTN, HEAD_DIM).transpose(0, 2, 3, 1)
        return jnp.pad(t, ((0, 0), (0, 0), (0, 0), (0, 8 - dec_seq)))

    bt = _bucket_table(BLOCK + dec_seq)
    rb = rel_bias.astype(F32)
    qi = np.arange(dec_seq)[:, None]
    t8 = np.arange(8)[None, :]
    own_ok = (t8 <= qi) & (t8 < dec_seq)
    bias_own = jnp.where(jnp.asarray(own_ok)[..., None], rb[bt[np.maximum(qi - t8, 0)]], NEG_INF).transpose(2, 0, 1)
    off = np.arange(BLOCK).reshape(ppb, PAGE_SIZE)[None]
    bias_last = rb[bt[BLOCK + qi[:, :, None] - off]].transpose(3, 0, 1, 2)
    o = _sample_attn(slabs, idx.reshape(-1).astype(jnp.int32), heads_t(q), heads_t(z3[..., D_ATTN:2 * D_ATTN]),
                     heads_t(z3[..., 2 * D_ATTN:]), bias_own, bias_last, rb[N_BUCKETS - 1],
                     ckt.reshape(pool * H_ATTN, HEAD_DIM, PAGE_SIZE), cvt.reshape(pool * H_ATTN, HEAD_DIM, PAGE_SIZE),
                     dec_seq, n_past_blk)
    return o[..., :dec_seq].transpose(0, 3, 1, 2).reshape(db * dec_seq, D_ATTN)


def _pair_ones():
    r = np.arange(LANES)
    return jnp.asarray((r[:, None] // HEAD_DIM) == (r[None, :] // HEAD_DIM), BF16)


def _head_sum(x, ones_pair):
    hi = x.astype(BF16)
    lo = (x - hi.astype(F32)).astype(BF16)
    outs = []
    for g in range(N_PAIRS):
        sl = slice(g * LANES, (g + 1) * LANES)
        outs.append(jnp.dot(hi[:, sl], ones_pair, preferred_element_type=F32)
                    + jnp.dot(lo[:, sl], ones_pair, preferred_element_type=F32))
    return jnp.concatenate(outs, axis=-1)


def _rwkv_prep_kernel(z_ref, prev_ref, mu_ref, w0_ref, w2_ref, a0_ref, a2_ref, g2_ref, kk_ref, ka_ref, ones_ref,
                      r_ref, ld_ref, k_ref, v_ref, kap_ref, b_ref, g_ref, *, prev_is_boundary_row):
    z = z_ref[...]
    if prev_is_boundary_row:
        first = lax.broadcasted_iota(jnp.int32, z.shape, 0) == 0
        prev = jnp.where(first, prev_ref[0], pltpu.roll(z, 1, axis=0))
    else:
        prev = prev_ref[...]
    zs = z + (prev - z) * mu_ref[...]
    o = D_RWKV
    r = zs[:, :o]
    k = zs[:, o:2 * o]
    v = zs[:, 2 * o:3 * o]
    lora = zs[:, 3 * o:3 * o + LANES]
    gl = zs[:, 3 * o + LANES:]
    x = w0_ref[...] + _dot(jnp.tanh(lora), w2_ref[...])
    nx = -x
    w = -(jnp.maximum(nx, 0.0) + jnp.log(1.0 + jnp.exp(-jnp.abs(nx)))) - 0.5
    logd = -jnp.exp(w)
    a = _sigmoid(a0_ref[...] + _dot(lora, a2_ref[...]))
    g = _dot(_sigmoid(gl), g2_ref[...])
    kk = k * kk_ref[...]
    nrm = jnp.sqrt(_head_sum(kk * kk, ones_ref[...]))
    kap = kk / jnp.maximum(nrm, 1e-12)
    k2 = k * (1.0 + (a - 1.0) * ka_ref[...])
    for p in range(N_PAIRS):
        sl = slice(p * LANES, (p + 1) * LANES)
        r_ref[p] = r[:, sl]
        ld_ref[p] = logd[:, sl]
        k_ref[p] = k2[:, sl]
        v_ref[p] = v[:, sl]
        kap_ref[p] = kap[:, sl]
        b_ref[p] = (kap * a)[:, sl]
        g_ref[p] = g[:, sl]


def _rwkv_prep(zr, prev, lw, tm):
    n = zr.shape[0]
    boundary = prev.ndim == 3
    prev_spec = (pl.BlockSpec((1, 1, C_SHIFT), lambda i: (i, 0, 0)) if boundary
                 else pl.BlockSpec((tm, C_SHIFT), lambda i: (i, 0)))
    row = lambda a: a.reshape(1, -1).astype(F32)
    zpad = jnp.zeros((DECAY_LORA, D_RWKV), F32)
    w2p = jnp.concatenate([lw['w2'], zpad], axis=0).astype(BF16)
    a2p = jnp.concatenate([zpad, lw['a2']], axis=0).astype(BF16)
    full = lambda shape: pl.BlockSpec(shape, lambda i: (0,) * len(shape))
    out_sd = jax.ShapeDtypeStruct((N_PAIRS, n, LANES), F32)
    out_spec = pl.BlockSpec((N_PAIRS, tm, LANES), lambda i: (0, i, 0))
    return pl.pallas_call(
        functools.partial(_rwkv_prep_kernel, prev_is_boundary_row=boundary),
        grid=(n // tm,),
        in_specs=[pl.BlockSpec((tm, C_SHIFT), lambda i: (i, 0)), prev_spec,
                  full((1, C_SHIFT)), full((1, D_RWKV)), full((LANES, D_RWKV)), full((1, D_RWKV)),
                  full((LANES, D_RWKV)), full((GATE_LORA, D_RWKV)), full((1, D_RWKV)), full((1, D_RWKV)),
                  full((LANES, LANES))],
        out_specs=[out_spec] * 7,
        out_shape=[out_sd] * 7,
        compiler_params=_cparams("parallel"),
        name="rwkv_prep",
    )(zr, prev, row(lw['mu_shift']), row(lw['w0']), w2p, row(lw['a0']), a2p, lw['g2'].astype(BF16),
      row(lw['k_k']), row(lw['k_a']), _pair_ones())


def _rwkv_lockstep_kernel(r_ref, ld_ref, k_ref, v_ref, kap_ref, b_ref, g_ref, s0_ref, lg_ref, lb_ref, rk_ref,
                          y_ref, s_ref, *, chunk):
    c = pl.program_id(1)

    @pl.when(c == 0)
    def _():
        s_ref[...] = s0_ref[...]

    heads = range(H_RWKV)
    ti = lax.broadcasted_iota(jnp.int32, (chunk, chunk), 0)
    si = lax.broadcasted_iota(jnp.int32, (chunk, chunk), 1)
    tril_incl = (si <= ti)
    tril_strict = (si < ti)
    ltri = tril_incl.astype(F32)
    eye = (si == ti).astype(F32)
    hsl = [slice((hh % 2) * HEAD_DIM, (hh % 2 + 1) * HEAD_DIM) for hh in heads]

    ld = [ld_ref[p] for p in range(N_PAIRS)]
    cum = [_dot_hi(ltri, x) for x in ld]
    kt, rt, bi, ki, bh, kh, gall = [], [], [], [], [], [], []
    for p in range(N_PAIRS):
        last = cum[p][chunk - 1:chunk, :]
        gam_inv = jnp.exp(-cum[p])
        gam_tail = jnp.exp(last - cum[p])
        kap2, b2, k2 = kap_ref[p], b_ref[p], k_ref[p]
        kt2 = (kap2 * jnp.exp(cum[p] - ld[p])).astype(BF16)
        rt2 = r_ref[p] * jnp.exp(cum[p])
        bi2 = (b2 * gam_inv).astype(BF16)
        ki2 = (k2 * gam_inv).astype(BF16)
        bh2 = (b2 * gam_tail).astype(BF16)
        kh2 = (k2 * gam_tail).astype(BF16)
        g2 = jnp.exp(last)
        for h in range(2):
            sl = hsl[h]
            kt.append(kt2[:, sl]); rt.append(rt2[:, sl]); bi.append(bi2[:, sl]); ki.append(ki2[:, sl])
            bh.append(bh2[:, sl]); kh.append(kh2[:, sl]); gall.append(g2[:, sl])
    vv = [v_ref[hh // 2][:, hsl[hh]].astype(BF16) for hh in heads]
    kr = [jnp.concatenate([kt[hh], rt[hh].astype(BF16)], axis=0) for hh in heads]
    xb = [_dot_nt(kr[hh], bi[hh]) for hh in heads]
    xk = [_dot_nt(kr[hh], ki[hh]) for hh in heads]
    a_ab = [jnp.where(tril_strict, xb[hh][:chunk], 0.0) for hh in heads]
    a_rb = [jnp.where(tril_incl, xb[hh][chunk:], 0.0).astype(BF16) for hh in heads]
    a_ak = [jnp.where(tril_strict, xk[hh][:chunk], 0.0).astype(BF16) for hh in heads]
    a_rk = [jnp.where(tril_incl, xk[hh][chunk:], 0.0).astype(BF16) for hh in heads]
    akv = [_dot(a_ak[hh], vv[hh]) for hh in heads]
    ab = [x.astype(BF16) for x in a_ab]
    tm = [eye - a_ab[hh] for hh in heads]
    pw = [_dot(ab[hh], ab[hh]) for hh in heads]
    n_sq = int(math.log2(chunk)) - 1
    for it in range(n_sq):
        pwb = [x.astype(BF16) for x in pw]
        tm = [tm[hh] + _dot(tm[hh], pwb[hh]) for hh in heads]
        if it < n_sq - 1:
            pw = [_dot(pwb[hh], pwb[hh]) for hh in heads]
    tmb = [x.astype(BF16) for x in tm]
    w = [-_dot(tmb[hh], kt[hh]) for hh in heads]
    u0 = [-_dot(tmb[hh], akv[hh]) for hh in heads]
    wb = [x.astype(BF16) for x in w]
    u0b = [x.astype(BF16) for x in u0]
    pm = [rt[hh] + _dot(a_rb[hh], wb[hh]) for hh in heads]
    y0 = [_dot(a_rb[hh], u0b[hh]) + _dot(a_rk[hh], vv[hh]) for hh in heads]
    s_old = [s_ref[0, hh] for hh in heads]
    sb = [x.astype(BF16) for x in s_old]
    y = [_dot_nt(pm[hh], sb[hh]) + y0[hh] for hh in heads]
    mt = [_dot_tn(bh[hh], wb[hh]) for hh in heads]
    nt = [_dot_tn(jnp.concatenate([u0b[hh], vv[hh]], axis=0), jnp.concatenate([bh[hh], kh[hh]], axis=0))
          for hh in heads]
    for hh in heads:
        s_ref[0, hh] = s_old[hh] * gall[hh] + _dot_nt(sb[hh], mt[hh]) + nt[hh]
    for p in range(N_PAIRS):
        outs = []
        for h in range(2):
            hh = 2 * p + h
            sl = hsl[h]
            mu = jnp.mean(y[hh], axis=-1, keepdims=True)
            var = jnp.mean(jnp.square(y[hh] - mu), axis=-1, keepdims=True)
            yn = (y[hh] - mu) * lax.rsqrt(var + LNX_EPS)
            rk = r_ref[p][:, sl] * k_ref[p][:, sl] * rk_ref[p][:, sl]
            bonus = jnp.sum(rk, axis=-1, keepdims=True) * v_ref[p][:, sl]
            outs.append(yn * lg_ref[p][:, sl] + lb_ref[p][:, sl] + bonus)
        y_ref[p] = jnp.concatenate(outs, axis=-1) * g_ref[p]


def _rwkv_chunks(prep, s0, lw, n_batch, seq, chunk):
    n_chunks = seq // chunk
    pairs = lambda a: a.reshape(N_PAIRS, 1, LANES).astype(F32)
    tile = pl.BlockSpec((N_PAIRS, chunk, LANES), lambda b, c: (0, b * n_chunks + c, 0))
    state = pl.BlockSpec((1, H_RWKV, HEAD_DIM, HEAD_DIM), lambda b, c: (b, 0, 0, 0))
    par = pl.BlockSpec((N_PAIRS, 1, LANES), lambda b, c: (0, 0, 0))
    return pl.pallas_call(
        functools.partial(_rwkv_lockstep_kernel, chunk=chunk),
        grid=(n_batch, n_chunks),
        in_specs=[tile] * 7 + [state, par, par, par],
        out_specs=[tile, state],
        out_shape=[jax.ShapeDtypeStruct((N_PAIRS, n_batch * seq, LANES), F32),
                   jax.ShapeDtypeStruct((n_batch, H_RWKV, HEAD_DIM, HEAD_DIM), F32)],
        compiler_params=_cparams("parallel", "arbitrary"),
        name="rwkv_chunks",
    )(*prep, s0.astype(F32), pairs(lw['lnx_g']), pairs(lw['lnx_b']), pairs(lw['r_k']))


def _rwkv(zr, shift0, s0, lw, n_batch, seq):
    z3 = zr.reshape(n_batch, seq, C_SHIFT)
    tm = min(256, n_batch * seq)
    if seq % tm == 0:
        last = z3.reshape(n_batch, seq // tm, tm, C_SHIFT)[:, :, tm - 1]
        prev = jnp.concatenate([shift0[:, None].astype(F32), last[:, :-1]], axis=1).reshape(-1, 1, C_SHIFT)
    else:
        prev = jnp.concatenate([shift0[:, None].astype(F32), z3[:, :-1]], axis=1).reshape(n_batch * seq, C_SHIFT)
    prep = _rwkv_prep(zr, prev, lw, tm)
    chunk = CHUNK if seq >= CHUNK else SHORT_CHUNK
    pad = (-seq) % chunk
    if pad:
        prep = [jnp.pad(a.reshape(N_PAIRS, n_batch, seq, LANES), ((0, 0), (0, 0), (0, pad), (0, 0)))
                .reshape(N_PAIRS, n_batch * (seq + pad), LANES) for a in prep]
    y, s_fin = _rwkv_chunks(prep, s0, lw, n_batch, seq + pad, chunk)
    if pad:
        y = y.reshape(N_PAIRS, n_batch, seq + pad, LANES)[:, :, :seq].reshape(N_PAIRS, n_batch * seq, LANES)
    return y, s_fin, z3[:, -1]


def _layer_norm(h, g, b):
    mu = jnp.mean(h, axis=-1, keepdims=True)
    d = h - mu
    var = jnp.mean(d * d, axis=-1, keepdims=True)
    return d * lax.rsqrt(var + LN_EPS) * g + b


def _out_proj_kernel(oa_ref, or_ref, x_ref, wa_ref, wr_ref, g_ref, b_ref, rw_ref, x1_ref, x1b_ref, sc_ref):
    orw = jnp.concatenate([or_ref[p] for p in range(N_PAIRS)], axis=-1)
    mix = _dot(oa_ref[...], wa_ref[...]) + _dot(orw, wr_ref[...])
    x1 = _layer_norm(ALPHA * x_ref[...] + mix, g_ref[...], b_ref[...])
    x1_ref[...] = x1
    x1b_ref[...] = x1.astype(BF16)
    sc_ref[...] = _sigmoid(lax.dot_general(rw_ref[...], x1, (((1,), (1,)), ((), ())),
                                           precision=lax.Precision.HIGHEST, preferred_element_type=F32))


def _out_proj(o_attn, o_rwkv, x, lw, tm):
    n = x.shape[0]
    wo = lw['w_out'].astype(BF16)
    full = lambda shape: pl.BlockSpec(shape, lambda i: (0,) * len(shape))
    return pl.pallas_call(
        _out_proj_kernel,
        grid=(n // tm,),
        in_specs=[pl.BlockSpec((tm, D_ATTN), lambda i: (i, 0)),
                  pl.BlockSpec((N_PAIRS, tm, LANES), lambda i: (0, i, 0)),
                  pl.BlockSpec((tm, D_MODEL), lambda i: (i, 0)),
                  full((D_ATTN, D_MODEL)), full((D_RWKV, D_MODEL)), full((1, D_MODEL)), full((1, D_MODEL)),
                  full((N_EXPERTS, D_MODEL))],
        out_specs=[pl.BlockSpec((tm, D_MODEL), lambda i: (i, 0)), pl.BlockSpec((tm, D_MODEL), lambda i: (i, 0)),
                   pl.BlockSpec((N_EXPERTS, tm), lambda i: (0, i))],
        out_shape=[jax.ShapeDtypeStruct((n, D_MODEL), F32), jax.ShapeDtypeStruct((n, D_MODEL), BF16),
                   jax.ShapeDtypeStruct((N_EXPERTS, n), F32)],
        compiler_params=_cparams("parallel"),
        name="out_proj_ln_router",
    )(o_attn, o_rwkv, x, wo[:D_ATTN], wo[D_ATTN:], lw['ln1_g'].reshape(1, -1), lw['ln1_b'].reshape(1, -1),
      lw['router_w'].astype(F32).T)


def _expert_kernel(be_ref, na_ref, x_ref, wg_ref, wu_ref, wd_ref, *rest):
    y_ref, wg_s, wu_s, wd_s = rest[-4:]
    g = pl.program_id(0)
    active = g < na_ref[0]

    @pl.when(active & ((g == 0) | (be_ref[g] != be_ref[jnp.maximum(g - 1, 0)])))
    def _():
        wg_s[...] = wg_ref[0].astype(BF16)
        wu_s[...] = wu_ref[0].astype(BF16)
        wd_s[...] = wd_ref[0].astype(BF16)

    @pl.when(active)
    def _():
        x = x_ref[...]
        hg = jnp.dot(x, wg_s[...], preferred_element_type=F32)
        hu = jnp.dot(x, wu_s[...], preferred_element_type=F32)
        h = hg * _sigmoid(hg) * hu
        y_ref[...] = jnp.dot(h.astype(BF16), wd_s[...], preferred_element_type=F32).astype(y_ref.dtype)

    @pl.when(g >= na_ref[0])
    def _():
        y_ref[...] = jnp.zeros_like(y_ref)


def _expert_blocks(xs, blk_e, n_active, wg, wu, wd, tm, out_dtype, after):
    rows = xs.shape[0]
    n_blocks = rows // tm
    d_e = wg.shape[-1]
    return pl.pallas_call(
        _expert_kernel,
        grid_spec=pltpu.PrefetchScalarGridSpec(
            num_scalar_prefetch=2,
            grid=(n_blocks,),
            in_specs=[pl.BlockSpec((tm, D_MODEL), lambda g, be, na: (g, 0)),
                      pl.BlockSpec((1, D_MODEL, d_e), lambda g, be, na: (be[g], 0, 0)),
                      pl.BlockSpec((1, D_MODEL, d_e), lambda g, be, na: (be[g], 0, 0)),
                      pl.BlockSpec((1, d_e, D_MODEL), lambda g, be, na: (be[g], 0, 0))]
            + [pl.BlockSpec(memory_space=pl.ANY)] * len(after),
            out_specs=pl.BlockSpec((tm, D_MODEL), lambda g, be, na: (g, 0)),
            scratch_shapes=[pltpu.VMEM((D_MODEL, d_e), BF16), pltpu.VMEM((D_MODEL, d_e), BF16),
                            pltpu.VMEM((d_e, D_MODEL), BF16)],
        ),
        out_shape=jax.ShapeDtypeStruct((rows, D_MODEL), out_dtype),
        compiler_params=_cparams("arbitrary"),
        name="expert_blocks",
    )(blk_e, n_active, xs, wg, wu, wd, *after)


def _first_max(x, idx, n):
    m = jnp.max(x, axis=0, keepdims=True)
    return m, jnp.min(jnp.where(x == m, idx, n), axis=0, keepdims=True)


def _route_kernel(sc_ref, bias_ref, tri_ref, e_ref, gate_ref, rank_ref, cnt_ref, carry_s):
    i = pl.program_id(0)
    tr = sc_ref.shape[1]
    per_group = N_EXPERTS // N_GROUPS

    @pl.when(i == 0)
    def _():
        carry_s[...] = jnp.zeros_like(carry_s)

    sc = sc_ref[...]
    choice = sc + bias_ref[...]
    in_g = lax.broadcasted_iota(jnp.int32, (per_group, tr), 0)
    g_iota = lax.broadcasted_iota(jnp.int32, (N_GROUPS, tr), 0)
    grp = jnp.zeros((N_GROUPS, tr), F32)
    for gi in range(N_GROUPS):
        cg = choice[gi * per_group:(gi + 1) * per_group]
        m1, i1 = _first_max(cg, in_g, per_group)
        m2 = jnp.max(jnp.where(in_g == i1, NEG_INF, cg), axis=0, keepdims=True)
        grp = jnp.where(g_iota == gi, m1 + m2, grp)
    keep = jnp.zeros((N_GROUPS, tr), F32)
    for _ in range(TOPK_GROUPS):
        _, f = _first_max(grp, g_iota, N_GROUPS)
        hit = g_iota == f
        keep = jnp.where(hit, 1.0, keep)
        grp = jnp.where(hit, NEG_INF, grp)
    masked = jnp.concatenate(
        [jnp.where(keep[gi:gi + 1] > 0.0, choice[gi * per_group:(gi + 1) * per_group], NEG_INF)
         for gi in range(N_GROUPS)], axis=0)
    e_iota = lax.broadcasted_iota(jnp.int32, (N_EXPERTS, tr), 0)
    hits, es, ss = [], [], []
    onehot = jnp.zeros((N_EXPERTS, tr), F32)
    for _ in range(TOP_K):
        _, f = _first_max(masked, e_iota, N_EXPERTS)
        hit = e_iota == f
        hits.append(hit)
        es.append(f)
        ss.append(jnp.sum(jnp.where(hit, sc, 0.0), axis=0, keepdims=True))
        onehot = jnp.where(hit, 1.0, onehot)
        masked = jnp.where(hit, NEG_INF, masked)
    total = ss[0]
    for s in ss[1:]:
        total = total + s
    e_ref[...] = jnp.concatenate(es, axis=0)
    gate_ref[...] = jnp.concatenate([s / total * ROUTED_SCALE for s in ss], axis=0)
    before = carry_s[...] + jnp.dot(onehot.astype(BF16), tri_ref[...], preferred_element_type=F32)
    rank_ref[...] = jnp.concatenate(
        [jnp.sum(jnp.where(h, before, 0.0), axis=0, keepdims=True) for h in hits], axis=0).astype(jnp.int32)
    carry = carry_s[...] + jnp.sum(onehot, axis=1, keepdims=True)
    carry_s[...] = carry
    cnt_ref[...] = jnp.broadcast_to(carry, cnt_ref.shape).astype(jnp.int32)


def _route(scores_t, router_bias, tok0, n):
    tr = min(256, n)
    assert tok0 % tr == 0 and n % tr == 0
    tri = jnp.asarray(np.arange(tr)[:, None] < np.arange(tr)[None, :], BF16)
    col = pl.BlockSpec((TOP_K, tr), lambda i: (0, i))
    top_e, gate, rank, counts = pl.pallas_call(
        _route_kernel,
        grid=(n // tr,),
        in_specs=[pl.BlockSpec((N_EXPERTS, tr), lambda i: (0, i + tok0 // tr)),
                  pl.BlockSpec((N_EXPERTS, 1), lambda i: (0, 0)),
                  pl.BlockSpec((tr, tr), lambda i: (0, 0))],
        out_specs=[col, col, col, pl.BlockSpec((N_EXPERTS, LANES), lambda i: (0, 0))],
        out_shape=[jax.ShapeDtypeStruct((TOP_K, n), jnp.int32), jax.ShapeDtypeStruct((TOP_K, n), F32),
                   jax.ShapeDtypeStruct((TOP_K, n), jnp.int32), jax.ShapeDtypeStruct((N_EXPERTS, LANES), jnp.int32)],
        scratch_shapes=[pltpu.VMEM((N_EXPERTS, 1), F32)],
        compiler_params=_cparams("arbitrary"),
        name="route",
    )(scores_t, router_bias.astype(F32).reshape(N_EXPERTS, 1), tri)
    counts = counts[:, 0]
    pcounts = (counts + MOE_TM - 1) // MOE_TM * MOE_TM
    pends = jnp.cumsum(pcounts)
    starts = pends - pcounts
    dest = rank
    for e in range(N_EXPERTS):
        dest = dest + jnp.where(top_e == e, starts[e], 0)
    n_blocks = -(-(n * TOP_K) // MOE_TM) + N_EXPERTS
    slot_tok = (tok0 + jnp.arange(n_blocks * MOE_TM, dtype=jnp.int32) % n).at[dest.reshape(-1)].set(
        jnp.tile(jnp.arange(tok0, tok0 + n, dtype=jnp.int32), TOP_K))
    blk_start = jnp.arange(n_blocks, dtype=jnp.int32) * MOE_TM
    blk_e = jnp.minimum(jnp.sum((pends[None, :] <= blk_start[:, None]).astype(jnp.int32), axis=1), N_EXPERTS - 1)
    n_active = (pends[-1] // MOE_TM).astype(jnp.int32).reshape(1)
    return gate.T, dest.astype(jnp.int32), slot_tok, blk_e, n_active


def _combine_kernel(x1_ref, sh_ref, gate_ref, g_ref, b_ref, *rest, tiles_per_part):
    ys_refs, o_ref = rest[:-1], rest[-1]
    i = pl.program_id(0)
    gate = gate_ref[...]
    for part, ys_ref in enumerate(ys_refs):
        @pl.when(i // tiles_per_part == part)
        def _():
            f = sh_ref[...]
            for j in range(TOP_K):
                f = f + ys_ref[j].astype(F32) * gate[:, j:j + 1]
            o_ref[...] = _layer_norm(ALPHA * x1_ref[...] + f, g_ref[...], b_ref[...])


def _combine(x1, shared, ysgs, gate, lw, tm):
    n = x1.shape[0]
    tiles_per_part = n // len(ysgs) // tm
    row = pl.BlockSpec((tm, D_MODEL), lambda i: (i, 0))
    par = pl.BlockSpec((1, D_MODEL), lambda i: (0, 0))

    def part_spec(part):
        return pl.BlockSpec((TOP_K, tm, D_MODEL),
                            lambda i: (0, jnp.clip(i - part * tiles_per_part, 0, tiles_per_part - 1), 0))

    return pl.pallas_call(
        functools.partial(_combine_kernel, tiles_per_part=tiles_per_part),
        grid=(n // tm,),
        in_specs=[row, row, pl.BlockSpec((tm, TOP_K), lambda i: (i, 0)), par, par]
        + [part_spec(part) for part in range(len(ysgs))],
        out_specs=row,
        out_shape=jax.ShapeDtypeStruct((n, D_MODEL), F32),
        compiler_params=_cparams("arbitrary"),
        name="moe_combine_ln",
    )(x1, shared, gate, lw['ln2_g'].reshape(1, -1), lw['ln2_b'].reshape(1, -1), *ysgs)


def _moe(x1, x1b, scores_t, lw, ew, n_parts, while_gathering):
    n = x1.shape[0]
    n_part = n // n_parts
    routes = [_route(scores_t, lw['router_bias'], part * n_part, n_part) for part in range(n_parts)]
    wait_for = while_gathering(routes[0][2])
    ysgs, ys = [], None
    for part, (_, dest, slot_tok, blk_e, n_active) in enumerate(routes):
        xs = x1b[slot_tok]
        after = [routes[part + 1][2]] if part + 1 < n_parts else []
        after = after + ([ys] if ys is not None else wait_for)
        ys = _expert_blocks(xs, blk_e, n_active, ew['e_gate'], ew['e_up'], ew['e_down'], MOE_TM, BF16, after)
        ysgs.append(ys[dest.reshape(-1)].reshape(TOP_K, n_part, D_MODEL))
    tm_s = min(MOE_TM, n)
    shared = _expert_blocks(x1b, jnp.zeros((n // tm_s,), jnp.int32), jnp.full((1,), n // tm_s, jnp.int32),
                            ew['s_gate'], ew['s_up'], ew['s_down'], tm_s, F32, [ys])
    gate = jnp.concatenate([r[0] for r in routes], axis=0)
    return _combine(x1, shared, ysgs, gate, lw, min(128, n_part))


def _copy_kernel(x_ref, after_ref, o_ref):
    del after_ref
    o_ref[...] = x_ref[...]


def _order_after(x, after):
    return pl.pallas_call(
        _copy_kernel,
        in_specs=[pl.BlockSpec(memory_space=pltpu.VMEM), pl.BlockSpec(memory_space=pl.ANY)],
        out_specs=pl.BlockSpec(memory_space=pltpu.VMEM),
        out_shape=jax.ShapeDtypeStruct(x.shape, x.dtype),
        name="order_after",
    )(x, after)


def _project(x, w_in_b):
    n_batch, seq, _ = x.shape
    n = n_batch * seq
    xb = x.reshape(n, D_MODEL).astype(BF16)
    tm = min(512, n)
    zr = _matmul(xb, w_in_b[:, 3 * D_ATTN:], tm, C_SHIFT // 2)
    if seq % tm == 0:
        zqkv, kt, vt = _matmul_qkv(xb, w_in_b[:, :3 * D_ATTN], tm, n_batch, seq)
        k, v = (t.reshape(n_batch, H_ATTN, HEAD_DIM, seq).transpose(0, 3, 1, 2) for t in (kt, vt))
    else:
        zqkv = _matmul(xb, w_in_b[:, :3 * D_ATTN], tm, 768)
        k = zqkv[:, D_ATTN:2 * D_ATTN].reshape(n_batch, seq, H_ATTN, HEAD_DIM)
        v = zqkv[:, 2 * D_ATTN:].reshape(n_batch, seq, H_ATTN, HEAD_DIM)
    return zqkv, zr, k, v


def _layer(x, zr, o_attn, shift0, s0, lw, ew, while_gathering=lambda slots: []):
    n_batch, seq, _ = x.shape
    n = n_batch * seq
    o_rwkv, s_new, shift_new = _rwkv(zr, shift0, s0, lw, n_batch, seq)
    x1, x1b, scores = _out_proj(o_attn, o_rwkv, x.reshape(n, D_MODEL), lw, min(256, n))
    y = _moe(x1, x1b, scores, lw, ew, MOE_TOKEN_PARTS, while_gathering)
    return y.reshape(n_batch, seq, D_MODEL), s_new, shift_new


def kernel(x_prompt, x_sample, cache_k, cache_v, state_wkv, state_shift, page_table, rel_bias, w_in, mu_shift, w0, w2,
           a0, a2, g2, k_k, k_a, r_k, lnx_g, lnx_b, w_out, ln1_g, ln1_b, router_w, router_bias, e_gate, e_up, e_down,
           s_gate, s_up, s_down, ln2_g, ln2_b):
    assert w_in.shape[0] == DEPTH == 1
    n_b, seq, _ = x_prompt.shape
    db, dec_seq, _ = x_sample.shape
    past_len = page_table.shape[1] * PAGE_SIZE
    lw = {'mu_shift': mu_shift[0], 'w0': w0[0], 'w2': w2[0], 'a0': a0[0], 'a2': a2[0], 'g2': g2[0], 'k_k': k_k[0],
          'k_a': k_a[0], 'r_k': r_k[0], 'lnx_g': lnx_g[0], 'lnx_b': lnx_b[0], 'w_out': w_out[0], 'ln1_g': ln1_g[0],
          'ln1_b': ln1_b[0], 'router_w': router_w[0], 'router_bias': router_bias[0], 'ln2_g': ln2_g[0],
          'ln2_b': ln2_b[0]}
    ew = {'e_gate': e_gate[0], 'e_up': e_up[0], 'e_down': e_down[0],
          's_gate': s_gate, 's_up': s_up, 's_down': s_down}
    w_in_b = w_in[0].astype(BF16)

    zqkv_s, zr_s, k_s, v_s = _project(x_sample, w_in_b)
    idx = _sample_select(zqkv_s, cache_k, page_table, db, dec_seq, past_len)
    zqkv_p, zr_p, k_p, v_p = _project(x_prompt, w_in_b)
    o_attn_p = _moba_prompt(zqkv_p, _order_after(rel_bias, idx), n_b, seq)
    sample_attn = []

    def attend_sample(slots):
        sample_attn.append(
            _sample_attend(zqkv_s, idx, cache_k, cache_v, page_table, rel_bias, db, dec_seq, past_len, slots))
        return sample_attn

    y_p, w_p, s_p = _layer(
        x_prompt, zr_p, o_attn_p, jnp.zeros((n_b, C_SHIFT), F32),
        jnp.zeros((n_b, H_RWKV, HEAD_DIM, HEAD_DIM), F32), lw, ew, attend_sample)
    y_s, w_s, s_s = _layer(x_sample, zr_s, sample_attn[0], state_shift[0], state_wkv[0], lw, ew)
    return (y_p, y_s, k_p[None], v_p[None], w_p[None], s_p[None], k_s[None], v_s[None], w_s[None], s_s[None])
```
